```python
import jax, jax.numpy as jnp
from jax import lax
import numpy as np

D_MODEL = 1024
BATCH = 8
SEQ = 2048
DEPTH = 4

POOL_GROUPS = 4
POOL_GROUP_DIM = 128
POOL_WIDTH = POOL_GROUPS * POOL_GROUP_DIM
POOL_WINDOWS = (2, 4, 8, 16)
LRU_HEADS = 10
LRU_HEAD_DIM = 128
LRU_WIDTH = LRU_HEADS * LRU_HEAD_DIM
CONV_WIDTH = 4
LRU_C = 8.0
D_FF = 2816
EPS = 1e-6
IN_WIDTH = POOL_WIDTH + 2 * LRU_WIDTH + 2 * D_MODEL

kernel_name = "macaron_pool_rglru_gated_hybrid"


def rmsnorm(x, g):
    xf = x.astype(jnp.float32)
    var = jnp.mean(xf * xf, axis=-1, keepdims=True)
    return (xf * lax.rsqrt(var + EPS) * g.astype(jnp.float32)).astype(x.dtype)


def swiglu_ffn(h, w_up, w_down):
    u = h @ w_up
    a, b = jnp.split(u, 2, axis=-1)
    return (jax.nn.silu(a) * b) @ w_down


def causal_pool_minus_self(u, window):
    b, s, c = u.shape
    uf = u.astype(jnp.float32)
    cs = jnp.cumsum(uf, axis=1)
    cs_pad = jnp.concatenate([jnp.zeros((b, 1, c), jnp.float32), cs], axis=1)
    prev = jnp.concatenate([jnp.zeros((b, window - 1, c), jnp.float32), cs_pad[:, : s - window + 1]], axis=1)
    count = jnp.minimum(jnp.arange(1, s + 1, dtype=jnp.float32), float(window))[None, :, None]
    return ((cs - prev) / count - uf).astype(u.dtype)


def pool_mixer(u, w_grp, b_grp, scale):
    b, s, _ = u.shape
    ug = u.reshape(b, s, POOL_GROUPS, POOL_GROUP_DIM)
    pooled = jnp.stack([causal_pool_minus_self(ug[:, :, g], POOL_WINDOWS[g]) for g in range(POOL_GROUPS)], axis=2)
    mixed = jnp.einsum('bsgc,gcd->bsgd', pooled, w_grp) + b_grp
    return mixed.reshape(b, s, POOL_WIDTH) * scale


def causal_depthwise_conv(u, w, bias):
    s = u.shape[1]
    up = jnp.pad(u, ((0, 0), (CONV_WIDTH - 1, 0), (0, 0)))
    y = sum(up[:, k:k + s] * w[k] for k in range(CONV_WIDTH))
    return y + bias


def _lru_combine(left, right):
    a_l, b_l = left
    a_r, b_r = right
    return a_l * a_r, a_r * b_l + b_r


def rg_lru(u, w_a, b_a, w_x, b_x, lam):
    b, s, _ = u.shape
    uh = u.reshape(b, s, LRU_HEADS, LRU_HEAD_DIM)
    r = jax.nn.sigmoid(jnp.einsum('bshd,hde->bshe', uh, w_a) + b_a).reshape(b, s, LRU_WIDTH)
    i = jax.nn.sigmoid(jnp.einsum('bshd,hde->bshe', uh, w_x) + b_x).reshape(b, s, LRU_WIDTH)
    log_a = -LRU_C * r.astype(jnp.float32) * jax.nn.softplus(-lam.astype(jnp.float32))
    a = jnp.exp(log_a)
    mult = jnp.sqrt(-jnp.expm1(2.0 * log_a))
    bx = mult * (i * u).astype(jnp.float32)
    _, h = lax.associative_scan(_lru_combine, (a, bx), axis=1)
    return h.astype(u.dtype)


def hybrid_mixer(h, w_in, pool_w, pool_b, pool_scale, w_pool_up, conv_w, conv_b,
                 lru_w_a, lru_b_a, lru_w_x, lru_b_x, lru_lambda, w_lru_up, w_out):
    proj = h @ w_in
    s1 = POOL_WIDTH
    s2 = s1 + LRU_WIDTH
    s3 = s2 + LRU_WIDTH
    u_pool, u_lru, u_gelu, g_logits = proj[..., :s1], proj[..., s1:s2], proj[..., s2:s3], proj[..., s3:]
    y_pool = pool_mixer(u_pool, pool_w, pool_b, pool_scale) @ w_pool_up
    v = causal_depthwise_conv(u_lru, conv_w, conv_b)
    y_lru = (rg_lru(v, lru_w_a, lru_b_a, lru_w_x, lru_b_x, lru_lambda) * jax.nn.gelu(u_gelu)) @ w_lru_up
    g = jax.nn.sigmoid(g_logits)
    g_pool, g_lru = g[..., :D_MODEL], g[..., D_MODEL:]
    return (g_pool * y_pool + g_lru * y_lru) @ w_out


def _fwd_setup_inputs(seed: int = 0) -> dict:
    key = jax.random.key(seed)
    ks = jax.random.split(key, 26)
    f32 = jnp.float32

    def nrm(k, shape, fan_in):
        return jax.random.normal(k, shape, f32) * (fan_in ** -0.5)

    def gain(k, shape):
        return 1.0 + 0.02 * jax.random.normal(k, shape, f32)

    def small(k, shape):
        return 0.01 * jax.random.normal(k, shape, f32)

    L = DEPTH
    a_c = jax.random.uniform(ks[17], (L, LRU_WIDTH), f32, 0.9, 0.999)
    sig = a_c ** (1.0 / LRU_C)
    lam = jnp.log(sig) - jnp.log1p(-sig)
    return {
        "x": jax.random.normal(ks[0], (BATCH, SEQ, D_MODEL), f32),
        "norm_ffn1": gain(ks[1], (L, D_MODEL)),
        "ffn1_w_up": nrm(ks[2], (L, D_MODEL, 2 * D_FF), D_MODEL),
        "ffn1_w_down": nrm(ks[3], (L, D_FF, D_MODEL), D_FF),
        "norm_mix": gain(ks[4], (L, D_MODEL)),
        "w_in": nrm(ks[5], (L, D_MODEL, IN_WIDTH), D_MODEL),
        "pool_w": nrm(ks[6], (L, POOL_GROUPS, POOL_GROUP_DIM, POOL_GROUP_DIM), POOL_GROUP_DIM),
        "pool_b": small(ks[7], (L, POOL_GROUPS, POOL_GROUP_DIM)),
        "pool_scale": 1.0 + 0.1 * jax.random.normal(ks[8], (L, POOL_WIDTH), f32),
        "w_pool_up": nrm(ks[9], (L, POOL_WIDTH, D_MODEL), POOL_WIDTH),
        "conv_w": nrm(ks[10], (L, CONV_WIDTH, LRU_WIDTH), CONV_WIDTH),
        "conv_b": small(ks[11], (L, LRU_WIDTH)),
        "lru_w_a": nrm(ks[12], (L, LRU_HEADS, LRU_HEAD_DIM, LRU_HEAD_DIM), LRU_HEAD_DIM),
        "lru_b_a": small(ks[13], (L, LRU_HEADS, LRU_HEAD_DIM)),
        "lru_w_x": nrm(ks[14], (L, LRU_HEADS, LRU_HEAD_DIM, LRU_HEAD_DIM), LRU_HEAD_DIM),
        "lru_b_x": small(ks[15], (L, LRU_HEADS, LRU_HEAD_DIM)),
        "lru_lambda": lam,
        "w_lru_up": nrm(ks[16], (L, LRU_WIDTH, D_MODEL), LRU_WIDTH),
        "w_out": nrm(ks[18], (L, D_MODEL, D_MODEL), D_MODEL),
        "norm_ffn2": gain(ks[19], (L, D_MODEL)),
        "ffn2_w_up": nrm(ks[20], (L, D_MODEL, 2 * D_FF), D_MODEL),
        "ffn2_w_down": nrm(ks[21], (L, D_FF, D_MODEL), D_FF),
        "final_norm": gain(ks[22], (D_MODEL,)),
    }


def _fwd_reference(x, norm_ffn1, ffn1_w_up, ffn1_w_down, norm_mix, w_in, pool_w, pool_b, pool_scale,
              w_pool_up, conv_w, conv_b, lru_w_a, lru_b_a, lru_w_x, lru_b_x, lru_lambda, w_lru_up,
              w_out, norm_ffn2, ffn2_w_up, ffn2_w_down, final_norm):
    for l in range(DEPTH):
        x = x + 0.5 * swiglu_ffn(rmsnorm(x, norm_ffn1[l]), ffn1_w_up[l], ffn1_w_down[l])
        x = x + hybrid_mixer(rmsnorm(x, norm_mix[l]), w_in[l], pool_w[l], pool_b[l], pool_scale[l],
                             w_pool_up[l], conv_w[l], conv_b[l], lru_w_a[l], lru_b_a[l],
                             lru_w_x[l], lru_b_x[l], lru_lambda[l], w_lru_up[l], w_out[l])
        x = x + 0.5 * swiglu_ffn(rmsnorm(x, norm_ffn2[l]), ffn2_w_up[l], ffn2_w_down[l])
    return rmsnorm(x, final_norm)


import jax as _jax
import jax.numpy as _jnp

TWIN_FORMAT = 'train_step'
FWD_PARAMS = ['x', 'norm_ffn1', 'ffn1_w_up', 'ffn1_w_down', 'norm_mix', 'w_in', 'pool_w', 'pool_b', 'pool_scale', 'w_pool_up', 'conv_w', 'conv_b', 'lru_w_a', 'lru_b_a', 'lru_w_x', 'lru_b_x', 'lru_lambda', 'w_lru_up', 'w_out', 'norm_ffn2', 'ffn2_w_up', 'ffn2_w_down', 'final_norm']
TWIN_WEIGHTS = ['norm_ffn1', 'ffn1_w_up', 'ffn1_w_down', 'norm_mix', 'w_in', 'pool_w', 'pool_b', 'pool_scale', 'w_pool_up', 'conv_w', 'conv_b', 'lru_w_a', 'lru_b_a', 'lru_w_x', 'lru_b_x', 'lru_lambda', 'w_lru_up', 'w_out', 'norm_ffn2', 'ffn2_w_up', 'ffn2_w_down', 'final_norm']
TWIN_DIFF_INPUT = 'x'
TWIN_INPUTS = ['x', 'norm_ffn1', 'ffn1_w_up', 'ffn1_w_down', 'norm_mix', 'w_in', 'pool_w', 'pool_b', 'pool_scale', 'w_pool_up', 'conv_w', 'conv_b', 'lru_w_a', 'lru_b_a', 'lru_w_x', 'lru_b_x', 'lru_lambda', 'w_lru_up', 'w_out', 'norm_ffn2', 'ffn2_w_up', 'ffn2_w_down', 'final_norm', 'loss_target', 'm_norm_ffn1', 'm_ffn1_w_up', 'm_ffn1_w_down', 'm_norm_mix', 'm_w_in', 'm_pool_w', 'm_pool_b', 'm_pool_scale', 'm_w_pool_up', 'm_conv_w', 'm_conv_b', 'm_lru_w_a', 'm_lru_b_a', 'm_lru_w_x', 'm_lru_b_x', 'm_lru_lambda', 'm_w_lru_up', 'm_w_out', 'm_norm_ffn2', 'm_ffn2_w_up', 'm_ffn2_w_down', 'm_final_norm', 'v_norm_ffn1', 'v_ffn1_w_up', 'v_ffn1_w_down', 'v_norm_mix', 'v_w_in', 'v_pool_w', 'v_pool_b', 'v_pool_scale', 'v_w_pool_up', 'v_conv_w', 'v_conv_b', 'v_lru_w_a', 'v_lru_b_a', 'v_lru_w_x', 'v_lru_b_x', 'v_lru_lambda', 'v_w_lru_up', 'v_w_out', 'v_norm_ffn2', 'v_ffn2_w_up', 'v_ffn2_w_down', 'v_final_norm']
TWIN_OUTPUTS = ['loss', 'grad_x', 'grad_norm_ffn1', 'grad_ffn1_w_up', 'grad_ffn1_w_down', 'grad_norm_mix', 'grad_w_in', 'grad_pool_w', 'grad_pool_b', 'grad_pool_scale', 'grad_w_pool_up', 'grad_conv_w', 'grad_conv_b', 'grad_lru_w_a', 'grad_lru_b_a', 'grad_lru_w_x', 'grad_lru_b_x', 'grad_lru_lambda', 'grad_w_lru_up', 'grad_w_out', 'grad_norm_ffn2', 'grad_ffn2_w_up', 'grad_ffn2_w_down', 'grad_final_norm', 'delta_norm_ffn1', 'delta_ffn1_w_up', 'delta_ffn1_w_down', 'delta_norm_mix', 'delta_w_in', 'delta_pool_w', 'delta_pool_b', 'delta_pool_scale', 'delta_w_pool_up', 'delta_conv_w', 'delta_conv_b', 'delta_lru_w_a', 'delta_lru_b_a', 'delta_lru_w_x', 'delta_lru_b_x', 'delta_lru_lambda', 'delta_w_lru_up', 'delta_w_out', 'delta_norm_ffn2', 'delta_ffn2_w_up', 'delta_ffn2_w_down', 'delta_final_norm', 'new_m_norm_ffn1', 'new_m_ffn1_w_up', 'new_m_ffn1_w_down', 'new_m_norm_mix', 'new_m_w_in', 'new_m_pool_w', 'new_m_pool_b', 'new_m_pool_scale', 'new_m_w_pool_up', 'new_m_conv_w', 'new_m_conv_b', 'new_m_lru_w_a', 'new_m_lru_b_a', 'new_m_lru_w_x', 'new_m_lru_b_x', 'new_m_lru_lambda', 'new_m_w_lru_up', 'new_m_w_out', 'new_m_norm_ffn2', 'new_m_ffn2_w_up', 'new_m_ffn2_w_down', 'new_m_final_norm', 'new_v_norm_ffn1', 'new_v_ffn1_w_up', 'new_v_ffn1_w_down', 'new_v_norm_mix', 'new_v_w_in', 'new_v_pool_w', 'new_v_pool_b', 'new_v_pool_scale', 'new_v_w_pool_up', 'new_v_conv_w', 'new_v_conv_b', 'new_v_lru_w_a', 'new_v_lru_b_a', 'new_v_lru_w_x', 'new_v_lru_b_x', 'new_v_lru_lambda', 'new_v_w_lru_up', 'new_v_w_out', 'new_v_norm_ffn2', 'new_v_ffn2_w_up', 'new_v_ffn2_w_down', 'new_v_final_norm']
TWIN_LEAF_KINDS = {'loss': 'loss', 'grad_x': 'grad_x', 'grad_norm_ffn1': 'grad_w', 'grad_ffn1_w_up': 'grad_w', 'grad_ffn1_w_down': 'grad_w', 'grad_norm_mix': 'grad_w', 'grad_w_in': 'grad_w', 'grad_pool_w': 'grad_w', 'grad_pool_b': 'grad_w', 'grad_pool_scale': 'grad_w', 'grad_w_pool_up': 'grad_w', 'grad_conv_w': 'grad_w', 'grad_conv_b': 'grad_w', 'grad_lru_w_a': 'grad_w', 'grad_lru_b_a': 'grad_w', 'grad_lru_w_x': 'grad_w', 'grad_lru_b_x': 'grad_w', 'grad_lru_lambda': 'grad_w', 'grad_w_lru_up': 'grad_w', 'grad_w_out': 'grad_w', 'grad_norm_ffn2': 'grad_w', 'grad_ffn2_w_up': 'grad_w', 'grad_ffn2_w_down': 'grad_w', 'grad_final_norm': 'grad_w', 'delta_norm_ffn1': 'delta_w', 'delta_ffn1_w_up': 'delta_w', 'delta_ffn1_w_down': 'delta_w', 'delta_norm_mix': 'delta_w', 'delta_w_in': 'delta_w', 'delta_pool_w': 'delta_w', 'delta_pool_b': 'delta_w', 'delta_pool_scale': 'delta_w', 'delta_w_pool_up': 'delta_w', 'delta_conv_w': 'delta_w', 'delta_conv_b': 'delta_w', 'delta_lru_w_a': 'delta_w', 'delta_lru_b_a': 'delta_w', 'delta_lru_w_x': 'delta_w', 'delta_lru_b_x': 'delta_w', 'delta_lru_lambda': 'delta_w', 'delta_w_lru_up': 'delta_w', 'delta_w_out': 'delta_w', 'delta_norm_ffn2': 'delta_w', 'delta_ffn2_w_up': 'delta_w', 'delta_ffn2_w_down': 'delta_w', 'delta_final_norm': 'delta_w', 'new_m_norm_ffn1': 'new_m', 'new_m_ffn1_w_up': 'new_m', 'new_m_ffn1_w_down': 'new_m', 'new_m_norm_mix': 'new_m', 'new_m_w_in': 'new_m', 'new_m_pool_w': 'new_m', 'new_m_pool_b': 'new_m', 'new_m_pool_scale': 'new_m', 'new_m_w_pool_up': 'new_m', 'new_m_conv_w': 'new_m', 'new_m_conv_b': 'new_m', 'new_m_lru_w_a': 'new_m', 'new_m_lru_b_a': 'new_m', 'new_m_lru_w_x': 'new_m', 'new_m_lru_b_x': 'new_m', 'new_m_lru_lambda': 'new_m', 'new_m_w_lru_up': 'new_m', 'new_m_w_out': 'new_m', 'new_m_norm_ffn2': 'new_m', 'new_m_ffn2_w_up': 'new_m', 'new_m_ffn2_w_down': 'new_m', 'new_m_final_norm': 'new_m', 'new_v_norm_ffn1': 'new_v', 'new_v_ffn1_w_up': 'new_v', 'new_v_ffn1_w_down': 'new_v', 'new_v_norm_mix': 'new_v', 'new_v_w_in': 'new_v', 'new_v_pool_w': 'new_v', 'new_v_pool_b': 'new_v', 'new_v_pool_scale': 'new_v', 'new_v_w_pool_up': 'new_v', 'new_v_conv_w': 'new_v', 'new_v_conv_b': 'new_v', 'new_v_lru_w_a': 'new_v', 'new_v_lru_b_a': 'new_v', 'new_v_lru_w_x': 'new_v', 'new_v_lru_b_x': 'new_v', 'new_v_lru_lambda': 'new_v', 'new_v_w_lru_up': 'new_v', 'new_v_w_out': 'new_v', 'new_v_norm_ffn2': 'new_v', 'new_v_ffn2_w_up': 'new_v', 'new_v_ffn2_w_down': 'new_v', 'new_v_final_norm': 'new_v'}


def _forward(args):
    return _fwd_reference(*[args[k] for k in FWD_PARAMS])


def _output_shape():
    out = _jax.eval_shape(lambda: _forward(_fwd_setup_inputs(0)))
    return out.shape, out.dtype

N_MICROBATCH = 1
ADAM_LR = 0.001
ADAM_B1 = 0.9
ADAM_B2 = 0.999
ADAM_EPS = 1e-08
ADAM_WD = 0.01
ADAM_STEP = 10
PER_EXAMPLE_BATCH_AXIS = {'x': 0, 'loss_target': 0}
SHARED_INPUTS = []
_WEIGHT_DTYPES = {'norm_ffn1': _jnp.float32, 'ffn1_w_up': _jnp.float32, 'ffn1_w_down': _jnp.float32, 'norm_mix': _jnp.float32, 'w_in': _jnp.float32, 'pool_w': _jnp.float32, 'pool_b': _jnp.float32, 'pool_scale': _jnp.float32, 'w_pool_up': _jnp.float32, 'conv_w': _jnp.float32, 'conv_b': _jnp.float32, 'lru_w_a': _jnp.float32, 'lru_b_a': _jnp.float32, 'lru_w_x': _jnp.float32, 'lru_b_x': _jnp.float32, 'lru_lambda': _jnp.float32, 'w_lru_up': _jnp.float32, 'w_out': _jnp.float32, 'norm_ffn2': _jnp.float32, 'ffn2_w_up': _jnp.float32, 'ffn2_w_down': _jnp.float32, 'final_norm': _jnp.float32}
MOMENT_SCALE = {'norm_ffn1': 5.136649e-02, 'ffn1_w_up': 2.140696e-02, 'ffn1_w_down': 3.491050e-02, 'norm_mix': 6.399274e-02, 'w_in': 2.935980e-02, 'pool_w': 7.068994e-02, 'pool_b': 1.468270e-01, 'pool_scale': 6.888277e-02, 'w_pool_up': 4.983153e-02, 'conv_w': 2.605641e-02, 'conv_b': 2.746328e-01, 'lru_w_a': 7.801586e-03, 'lru_b_a': 6.491118e-03, 'lru_w_x': 1.398923e-02, 'lru_b_x': 9.551401e-03, 'lru_lambda': 1.263369e-02, 'w_lru_up': 2.846612e-02, 'w_out': 5.670945e-02, 'norm_ffn2': 4.336204e-02, 'ffn2_w_up': 1.821975e-02, 'ffn2_w_down': 2.975034e-02, 'final_norm': 1.601480e+01}


def _to_microbatches(a, axis):
    t = _jnp.moveaxis(a, axis, 0)
    t = t.reshape((N_MICROBATCH, t.shape[0] // N_MICROBATCH) + t.shape[1:])
    return _jnp.moveaxis(t, 1, axis + 1)


def setup_inputs(seed: int = 0) -> dict:
    inp = _fwd_setup_inputs(seed)
    key = _jax.random.fold_in(_jax.random.key(seed), 7919)
    shape, _ = _output_shape()
    out = dict(inp)
    out["loss_target"] = _jax.random.normal(_jax.random.fold_in(key, 0), shape, _jnp.float32)
    for i, name in enumerate(TWIN_WEIGHTS):
        w = inp[name].astype(_jnp.float32)
        if MOMENT_SCALE is None:
            s = _jnp.sqrt(_jnp.mean(_jnp.square(w)) + 1e-30)
        else:
            s = MOMENT_SCALE[name]
        km, kv = _jax.random.split(_jax.random.fold_in(key, i + 1))
        out[name] = w
        out["m_" + name] = s * _jax.random.normal(km, w.shape, _jnp.float32)
        out["v_" + name] = (s * s) * _jax.random.uniform(kv, w.shape, _jnp.float32, 0.5, 1.5)
    if N_MICROBATCH > 1:
        for name, axis in PER_EXAMPLE_BATCH_AXIS.items():
            out[name] = _to_microbatches(out[name], axis)
    return {'x': out['x'], 'norm_ffn1': out['norm_ffn1'], 'ffn1_w_up': out['ffn1_w_up'], 'ffn1_w_down': out['ffn1_w_down'], 'norm_mix': out['norm_mix'], 'w_in': out['w_in'], 'pool_w': out['pool_w'], 'pool_b': out['pool_b'], 'pool_scale': out['pool_scale'], 'w_pool_up': out['w_pool_up'], 'conv_w': out['conv_w'], 'conv_b': out['conv_b'], 'lru_w_a': out['lru_w_a'], 'lru_b_a': out['lru_b_a'], 'lru_w_x': out['lru_w_x'], 'lru_b_x': out['lru_b_x'], 'lru_lambda': out['lru_lambda'], 'w_lru_up': out['w_lru_up'], 'w_out': out['w_out'], 'norm_ffn2': out['norm_ffn2'], 'ffn2_w_up': out['ffn2_w_up'], 'ffn2_w_down': out['ffn2_w_down'], 'final_norm': out['final_norm'], 'loss_target': out['loss_target'], 'm_norm_ffn1': out['m_norm_ffn1'], 'm_ffn1_w_up': out['m_ffn1_w_up'], 'm_ffn1_w_down': out['m_ffn1_w_down'], 'm_norm_mix': out['m_norm_mix'], 'm_w_in': out['m_w_in'], 'm_pool_w': out['m_pool_w'], 'm_pool_b': out['m_pool_b'], 'm_pool_scale': out['m_pool_scale'], 'm_w_pool_up': out['m_w_pool_up'], 'm_conv_w': out['m_conv_w'], 'm_conv_b': out['m_conv_b'], 'm_lru_w_a': out['m_lru_w_a'], 'm_lru_b_a': out['m_lru_b_a'], 'm_lru_w_x': out['m_lru_w_x'], 'm_lru_b_x': out['m_lru_b_x'], 'm_lru_lambda': out['m_lru_lambda'], 'm_w_lru_up': out['m_w_lru_up'], 'm_w_out': out['m_w_out'], 'm_norm_ffn2': out['m_norm_ffn2'], 'm_ffn2_w_up': out['m_ffn2_w_up'], 'm_ffn2_w_down': out['m_ffn2_w_down'], 'm_final_norm': out['m_final_norm'], 'v_norm_ffn1': out['v_norm_ffn1'], 'v_ffn1_w_up': out['v_ffn1_w_up'], 'v_ffn1_w_down': out['v_ffn1_w_down'], 'v_norm_mix': out['v_norm_mix'], 'v_w_in': out['v_w_in'], 'v_pool_w': out['v_pool_w'], 'v_pool_b': out['v_pool_b'], 'v_pool_scale': out['v_pool_scale'], 'v_w_pool_up': out['v_w_pool_up'], 'v_conv_w': out['v_conv_w'], 'v_conv_b': out['v_conv_b'], 'v_lru_w_a': out['v_lru_w_a'], 'v_lru_b_a': out['v_lru_b_a'], 'v_lru_w_x': out['v_lru_w_x'], 'v_lru_b_x': out['v_lru_b_x'], 'v_lru_lambda': out['v_lru_lambda'], 'v_w_lru_up': out['v_w_lru_up'], 'v_w_out': out['v_w_out'], 'v_norm_ffn2': out['v_norm_ffn2'], 'v_ffn2_w_up': out['v_ffn2_w_up'], 'v_ffn2_w_down': out['v_ffn2_w_down'], 'v_final_norm': out['v_final_norm']}


def _loss(weights, diff, rest, loss_target):
    with _jax.named_scope("forward"):
        args = {**rest, TWIN_DIFF_INPUT: diff, **{k: w.astype(_WEIGHT_DTYPES[k]) for k, w in weights.items()}}
        y = _forward(args)
    with _jax.named_scope("loss_head"):
        err = _jnp.square(y.astype(_jnp.float32) - loss_target)
        return 0.5 * _jnp.sum(_jnp.mean(err, axis=-1)) if err.ndim else 0.5 * err


def _adamw(w, g, m, v):
    m = ADAM_B1 * m + (1.0 - ADAM_B1) * g
    v = ADAM_B2 * v + (1.0 - ADAM_B2) * _jnp.square(g)
    m_hat = m / (1.0 - ADAM_B1 ** ADAM_STEP)
    v_hat = v / (1.0 - ADAM_B2 ** ADAM_STEP)
    delta = -ADAM_LR * (m_hat / (_jnp.sqrt(v_hat) + ADAM_EPS) + ADAM_WD * w)
    return delta, m, v


def reference(x, norm_ffn1, ffn1_w_up, ffn1_w_down, norm_mix, w_in, pool_w, pool_b, pool_scale, w_pool_up, conv_w, conv_b, lru_w_a, lru_b_a, lru_w_x, lru_b_x, lru_lambda, w_lru_up, w_out, norm_ffn2, ffn2_w_up, ffn2_w_down, final_norm, loss_target, m_norm_ffn1, m_ffn1_w_up, m_ffn1_w_down, m_norm_mix, m_w_in, m_pool_w, m_pool_b, m_pool_scale, m_w_pool_up, m_conv_w, m_conv_b, m_lru_w_a, m_lru_b_a, m_lru_w_x, m_lru_b_x, m_lru_lambda, m_w_lru_up, m_w_out, m_norm_ffn2, m_ffn2_w_up, m_ffn2_w_down, m_final_norm, v_norm_ffn1, v_ffn1_w_up, v_ffn1_w_down, v_norm_mix, v_w_in, v_pool_w, v_pool_b, v_pool_scale, v_w_pool_up, v_conv_w, v_conv_b, v_lru_w_a, v_lru_b_a, v_lru_w_x, v_lru_b_x, v_lru_lambda, v_w_lru_up, v_w_out, v_norm_ffn2, v_ffn2_w_up, v_ffn2_w_down, v_final_norm):
    given = dict(x=x, norm_ffn1=norm_ffn1, ffn1_w_up=ffn1_w_up, ffn1_w_down=ffn1_w_down, norm_mix=norm_mix, w_in=w_in, pool_w=pool_w, pool_b=pool_b, pool_scale=pool_scale, w_pool_up=w_pool_up, conv_w=conv_w, conv_b=conv_b, lru_w_a=lru_w_a, lru_b_a=lru_b_a, lru_w_x=lru_w_x, lru_b_x=lru_b_x, lru_lambda=lru_lambda, w_lru_up=w_lru_up, w_out=w_out, norm_ffn2=norm_ffn2, ffn2_w_up=ffn2_w_up, ffn2_w_down=ffn2_w_down, final_norm=final_norm, loss_target=loss_target, m_norm_ffn1=m_norm_ffn1, m_ffn1_w_up=m_ffn1_w_up, m_ffn1_w_down=m_ffn1_w_down, m_norm_mix=m_norm_mix, m_w_in=m_w_in, m_pool_w=m_pool_w, m_pool_b=m_pool_b, m_pool_scale=m_pool_scale, m_w_pool_up=m_w_pool_up, m_conv_w=m_conv_w, m_conv_b=m_conv_b, m_lru_w_a=m_lru_w_a, m_lru_b_a=m_lru_b_a, m_lru_w_x=m_lru_w_x, m_lru_b_x=m_lru_b_x, m_lru_lambda=m_lru_lambda, m_w_lru_up=m_w_lru_up, m_w_out=m_w_out, m_norm_ffn2=m_norm_ffn2, m_ffn2_w_up=m_ffn2_w_up, m_ffn2_w_down=m_ffn2_w_down, m_final_norm=m_final_norm, v_norm_ffn1=v_norm_ffn1, v_ffn1_w_up=v_ffn1_w_up, v_ffn1_w_down=v_ffn1_w_down, v_norm_mix=v_norm_mix, v_w_in=v_w_in, v_pool_w=v_pool_w, v_pool_b=v_pool_b, v_pool_scale=v_pool_scale, v_w_pool_up=v_w_pool_up, v_conv_w=v_conv_w, v_conv_b=v_conv_b, v_lru_w_a=v_lru_w_a, v_lru_b_a=v_lru_b_a, v_lru_w_x=v_lru_w_x, v_lru_b_x=v_lru_b_x, v_lru_lambda=v_lru_lambda, v_w_lru_up=v_w_lru_up, v_w_out=v_w_out, v_norm_ffn2=v_norm_ffn2, v_ffn2_w_up=v_ffn2_w_up, v_ffn2_w_down=v_ffn2_w_down, v_final_norm=v_final_norm)
    weights = {n: given[n] for n in TWIN_WEIGHTS}
    shared = {n: given[n] for n in SHARED_INPUTS}
    per_example = {n: given[n] for n in ['x']}
    grad_fn = _jax.value_and_grad(_loss, argnums=(0, 1))

    def one_microbatch(ex, loss_target):
        ex = dict(ex)
        diff = ex.pop(TWIN_DIFF_INPUT)
        return grad_fn(weights, diff, {**shared, **ex}, loss_target)

    if N_MICROBATCH == 1:
        loss, (grad_w, grad_x) = one_microbatch(per_example, given["loss_target"])
    else:
        def body(carry, xs):
            loss_sum, grad_sum = carry
            l_k, (gw_k, gx_k) = one_microbatch(xs[0], xs[1])
            with _jax.named_scope("update"):
                return (loss_sum + l_k, _jax.tree.map(_jnp.add, grad_sum, gw_k)), gx_k

        init = (_jnp.zeros((), _jnp.float32), _jax.tree.map(_jnp.zeros_like, weights))
        (loss, grad_w), grad_x = _jax.lax.scan(body, init, (per_example, given["loss_target"]))
    with _jax.named_scope("update"):
        delta_w, new_m, new_v = {}, {}, {}
        for n in TWIN_WEIGHTS:
            delta_w[n], new_m[n], new_v[n] = _adamw(weights[n], grad_w[n], given["m_" + n], given["v_" + n])
    return (loss, grad_x, *[grad_w[n] for n in TWIN_WEIGHTS], *[delta_w[n] for n in TWIN_WEIGHTS],
            *[new_m[n] for n in TWIN_WEIGHTS], *[new_v[n] for n in TWIN_WEIGHTS])
```

```python
import functools

import jax
import jax.numpy as jnp
from jax import lax
from jax.experimental import pallas as pl
from jax.experimental.pallas import tpu as pltpu

F32 = jnp.float32
MM = jnp.bfloat16
XFER = jnp.bfloat16

EPS = 1e-6
LRU_C = 8.0
POOL_WINDOWS = (2, 4, 8, 16)
CONV_WIDTH = 4
ADAM_LR, ADAM_B1, ADAM_B2, ADAM_EPS, ADAM_WD, ADAM_STEP = 0.001, 0.9, 0.999, 1e-08, 0.01, 10

N_CHIPS = 4
N_DEV = 8
LANES = 128
SHIFT_PAD = 8
TOKEN_TILE = 256
VMEM_LIMIT = 52 * 1024 * 1024
MESH = pl.DeviceIdType.MESH
ANY = pl.BlockSpec(memory_space=pl.ANY)


def _params(sem=None):
    return pltpu.CompilerParams(dimension_semantics=sem, vmem_limit_bytes=VMEM_LIMIT)


def _dot(a, b):
    return jnp.dot(a.astype(MM), b.astype(MM), preferred_element_type=F32)


def _dot_nt(a, b):
    return lax.dot_general(a.astype(MM), b.astype(MM), (((1,), (1,)), ((), ())), preferred_element_type=F32)


def _dot_tn(a, b):
    return lax.dot_general(a.astype(MM), b.astype(MM), (((0,), (0,)), ((), ())), preferred_element_type=F32)


def _rms(x, g):
    r = lax.rsqrt(jnp.mean(x * x, axis=-1, keepdims=True) + EPS)
    xh = x * r
    return r, xh, xh * g


def _rms_bwd(dh, xh, r, g):
    dxh = dh * g
    return r * (dxh - xh * jnp.mean(dxh * xh, axis=-1, keepdims=True))


def _accumulate(ref, val, first):
    @pl.when(first)
    def _():
        ref[...] = val

    @pl.when(jnp.logical_not(first))
    def _():
        ref[...] += val


def _row_block(rows, cols, itemsize=4, budget=1 << 20):
    best = None
    for rb in range(16, rows + 1, 16):
        if rows % rb == 0 and rb * cols * itemsize <= budget:
            best = rb
    return best if best is not None else rows


def _norm_matmul_fwd(x, gain, w, name):
    T, D = x.shape
    K, _, tn = w.shape
    tm = min(TOKEN_TILE, T)

    def body(x_ref, g_ref, w_ref, u_ref):
        _, _, h = _rms(x_ref[...], g_ref[...])
        u_ref[...] = _dot(h, w_ref[...])

    return pl.pallas_call(
        body, name=name, grid=(K, T // tm),
        in_specs=[pl.BlockSpec((tm, D), lambda k, i: (i, 0)), pl.BlockSpec((1, D), lambda k, i: (0, 0)),
                  pl.BlockSpec((None, D, tn), lambda k, i: (k, 0, 0))],
        out_specs=pl.BlockSpec((tm, tn), lambda k, i: (i, k)),
        out_shape=jax.ShapeDtypeStruct((T, K * tn), F32),
        compiler_params=_params(("arbitrary", "arbitrary")),
    )(x, gain, w)


def _norm_matmul_bwd(x, gain, du, du_spec, w, dres, name):
    T, D = x.shape
    K, _, tn = w.shape
    tm = min(TOKEN_TILE, T)
    ni = T // tm

    def body(x_ref, g_ref, du_ref, w_ref, dres_ref, dx_ref, dw_ref, dg_ref, dh_acc, dg_acc):
        k, i = pl.program_id(0), pl.program_id(1)
        g = g_ref[...]
        r, xh, h = _rms(x_ref[...], g)
        du_t = du_ref[...].astype(MM)
        rows = pl.ds(pl.multiple_of(i * tm, tm), tm)
        part = _dot_nt(du_t, w_ref[...])

        @pl.when(k == 0)
        def _():
            dh_acc[rows, :] = part

        @pl.when(k > 0)
        def _():
            dh_acc[rows, :] += part

        _accumulate(dw_ref, _dot_tn(h, du_t), i == 0)

        @pl.when(k == K - 1)
        def _():
            dh = dh_acc[rows, :]
            _accumulate(dg_acc, jnp.sum(dh * xh, axis=0, keepdims=True), i == 0)
            dx_ref[...] = dres_ref[...] + _rms_bwd(dh, xh, r, g)

            @pl.when(i == ni - 1)
            def _():
                dg_ref[...] = dg_acc[...]

    def last(k, i):
        return (jnp.where(k == K - 1, i, 0), 0)

    return pl.pallas_call(
        body, name=name, grid=(K, ni),
        in_specs=[pl.BlockSpec((tm, D), lambda k, i: (i, 0)), pl.BlockSpec((1, D), lambda k, i: (0, 0)),
                  du_spec(tm, tn), pl.BlockSpec((None, D, tn), lambda k, i: (k, 0, 0)),
                  pl.BlockSpec((tm, D), last)],
        out_specs=[pl.BlockSpec((tm, D), last), pl.BlockSpec((None, D, tn), lambda k, i: (k, 0, 0)),
                   pl.BlockSpec((1, D), lambda k, i: (0, 0))],
        out_shape=[jax.ShapeDtypeStruct((T, D), F32), jax.ShapeDtypeStruct((K, D, tn), F32),
                   jax.ShapeDtypeStruct((1, D), F32)],
        scratch_shapes=[pltpu.VMEM((T, D), F32), pltpu.VMEM((1, D), F32)],
        compiler_params=_params(("arbitrary", "arbitrary")),
    )(x, gain, du, w, dres)


def _swiglu_down_fwd(u, x, wd, name):
    T, D = x.shape
    Fh = wd.shape[0]
    tm = min(TOKEN_TILE, T)

    def body(a_ref, b_ref, x_ref, wd_ref, o_ref):
        a = a_ref[...]
        s = a * jax.nn.sigmoid(a) * b_ref[...]
        o_ref[...] = x_ref[...] + 0.5 * _dot(s, wd_ref[...])

    return pl.pallas_call(
        body, name=name, grid=(T // tm,),
        in_specs=[pl.BlockSpec((tm, Fh), lambda i: (i, 0)), pl.BlockSpec((tm, Fh), lambda i: (i, 1)),
                  pl.BlockSpec((tm, D), lambda i: (i, 0)), pl.BlockSpec((Fh, D), lambda i: (0, 0))],
        out_specs=pl.BlockSpec((tm, D), lambda i: (i, 0)),
        out_shape=jax.ShapeDtypeStruct((T, D), F32),
        compiler_params=_params(("arbitrary",)),
    )(u, u, x, wd)


def _swiglu_down_bwd(u, dxn, wd, name):
    T, D = dxn.shape
    Fh = wd.shape[0]
    tm = min(TOKEN_TILE, T)
    nj = 2 if Fh % (2 * LANES) == 0 else 1
    tf = Fh // nj

    def body(a_ref, b_ref, dxn_ref, wd_ref, du_ref, dwd_ref):
        i = pl.program_id(1)
        a, b = a_ref[...], b_ref[...]
        dyh = (0.5 * dxn_ref[...]).astype(MM)
        ds = _dot_nt(dyh, wd_ref[...])
        sig = jax.nn.sigmoid(a)
        sa = a * sig
        _accumulate(dwd_ref, _dot_tn(sa * b, dyh), i == 0)
        du_ref[0] = (ds * b * (sig * (1.0 + a * (1.0 - sig)))).astype(MM)
        du_ref[1] = (ds * sa).astype(MM)

    return pl.pallas_call(
        body, name=name, grid=(nj, T // tm),
        in_specs=[pl.BlockSpec((tm, tf), lambda j, i: (i, j)), pl.BlockSpec((tm, tf), lambda j, i: (i, j + nj)),
                  pl.BlockSpec((tm, D), lambda j, i: (i, 0)), pl.BlockSpec((tf, D), lambda j, i: (j, 0))],
        out_specs=[pl.BlockSpec((2, tm, tf), lambda j, i: (0, i, j)), pl.BlockSpec((tf, D), lambda j, i: (j, 0))],
        out_shape=[jax.ShapeDtypeStruct((2, T, Fh), MM), jax.ShapeDtypeStruct((Fh, D), F32)],
        compiler_params=_params(("arbitrary", "arbitrary")),
    )(u, u, dxn, wd)


def _mix_branches(pm, zl, gp_logit, gr_logit, wpu_ref, wlu_ref):
    y_pool = jnp.concatenate([_dot(pm, wpu_ref[k]) for k in range(N_CHIPS)], axis=1)
    y_lru = _dot(zl, wlu_ref[...])
    return y_pool, y_lru, jax.nn.sigmoid(gp_logit), jax.nn.sigmoid(gr_logit)


def _mix_out_specs(tm, D, pw, lw, cs):
    gate0 = (pw + 2 * lw) // D
    return [pl.BlockSpec((tm, D), lambda i: (i, gate0)), pl.BlockSpec((tm, D), lambda i: (i, gate0 + 1)),
            pl.BlockSpec((tm, pw), lambda i: (i, 0)), pl.BlockSpec((tm, lw), lambda i: (i, 0)),
            pl.BlockSpec((N_CHIPS, pw, cs), lambda i: (0, 0, 0)), pl.BlockSpec((lw, D), lambda i: (0, 0)),
            pl.BlockSpec((D, D), lambda i: (0, 0))]


def _mix_out_fwd(x, proj, pm, zl, wpu, wlu, wo, name):
    T, D = x.shape
    pw, lw, cs = pm.shape[1], zl.shape[1], wpu.shape[2]
    assert (pw + 2 * lw) % D == 0
    tm = min(TOKEN_TILE, T)

    def body(x_ref, gp_ref, gr_ref, pm_ref, zl_ref, wpu_ref, wlu_ref, wo_ref, o_ref):
        y_pool, y_lru, gp, gr = _mix_branches(pm_ref[...], zl_ref[...], gp_ref[...], gr_ref[...], wpu_ref, wlu_ref)
        o_ref[...] = x_ref[...] + _dot(gp * y_pool + gr * y_lru, wo_ref[...])

    return pl.pallas_call(
        body, name=name, grid=(T // tm,),
        in_specs=[pl.BlockSpec((tm, D), lambda i: (i, 0))] + _mix_out_specs(tm, D, pw, lw, cs),
        out_specs=pl.BlockSpec((tm, D), lambda i: (i, 0)),
        out_shape=jax.ShapeDtypeStruct((T, D), F32),
        compiler_params=_params(("arbitrary",)),
    )(x, proj, proj, pm, zl, wpu, wlu, wo)


def _mix_out_bwd(dxn, proj, pm, zl, wpu, wlu, wo, name):
    T, D = dxn.shape
    pw, lw, cs = pm.shape[1], zl.shape[1], wpu.shape[2]
    tm = min(TOKEN_TILE, T)
    ni = T // tm

    def body(dxn_ref, gp_ref, gr_ref, pm_ref, zl_ref, wpu_ref, wlu_ref, wo_ref,
             dgl_ref, dpm_ref, dzl_ref, dwo_hbm, dwpu_hbm, dwlu_hbm, acc_o, acc_pu, acc_lu, sem):
        i = pl.program_id(0)
        pm, zl = pm_ref[...], zl_ref[...]
        y_pool, y_lru, gp, gr = _mix_branches(pm, zl, gp_ref[...], gr_ref[...], wpu_ref, wlu_ref)
        dxn_t = dxn_ref[...].astype(MM)
        dmerged = _dot_nt(dxn_t, wo_ref[...])
        _accumulate(acc_o, _dot_tn(gp * y_pool + gr * y_lru, dxn_t), i == 0)
        dgl_ref[:, :D] = (dmerged * y_pool * (gp * (1.0 - gp))).astype(MM)
        dgl_ref[:, D:] = (dmerged * y_lru * (gr * (1.0 - gr))).astype(MM)
        dyp = (dmerged * gp).astype(MM)
        dyl = (dmerged * gr).astype(MM)
        dpm = None
        for k in range(N_CHIPS):
            dyp_k = dyp[:, k * cs:(k + 1) * cs]
            part = _dot_nt(dyp_k, wpu_ref[k])
            dpm = part if dpm is None else dpm + part
            _accumulate(acc_pu.at[k], _dot_tn(pm, dyp_k), i == 0)
        dpm_ref[...] = dpm
        dzl_ref[...] = _dot_nt(dyl, wlu_ref[...])
        _accumulate(acc_lu, _dot_tn(zl, dyl), i == 0)

        @pl.when(i == ni - 1)
        def _():
            copies = [pltpu.make_async_copy(acc_o, dwo_hbm, sem.at[0]), pltpu.make_async_copy(acc_pu, dwpu_hbm, sem.at[1]),
                      pltpu.make_async_copy(acc_lu, dwlu_hbm, sem.at[2])]
            for cp in copies:
                cp.start()
            for cp in copies:
                cp.wait()

    return pl.pallas_call(
        body, name=name, grid=(ni,),
        in_specs=[pl.BlockSpec((tm, D), lambda i: (i, 0))] + _mix_out_specs(tm, D, pw, lw, cs),
        out_specs=[pl.BlockSpec((tm, 2 * D), lambda i: (i, 0)), pl.BlockSpec((tm, pw), lambda i: (i, 0)),
                   pl.BlockSpec((tm, lw), lambda i: (i, 0)), ANY, ANY, ANY],
        out_shape=[jax.ShapeDtypeStruct((T, 2 * D), MM), jax.ShapeDtypeStruct((T, pw), F32),
                   jax.ShapeDtypeStruct((T, lw), F32), jax.ShapeDtypeStruct((D, D), F32),
                   jax.ShapeDtypeStruct((N_CHIPS, pw, cs), F32), jax.ShapeDtypeStruct((lw, D), F32)],
        scratch_shapes=[pltpu.VMEM((D, D), F32), pltpu.VMEM((N_CHIPS, pw, cs), F32), pltpu.VMEM((lw, D), F32),
                        pltpu.SemaphoreType.DMA((3,))],
        compiler_params=_params(("arbitrary",)),
    )(dxn, proj, proj, pm, zl, wpu, wlu, wo)


def _shifted(buf, val, shift, S):
    buf[pl.ds(SHIFT_PAD, S), :] = val
    return buf[pl.ds(SHIFT_PAD - shift, S), :]


def _zero_pads(buf, S):
    z = jnp.zeros((SHIFT_PAD, buf.shape[1]), F32)
    buf[pl.ds(0, SHIFT_PAD), :] = z
    buf[pl.ds(SHIFT_PAD + S, SHIFT_PAD), :] = z


def _window_sum(buf, val, window, S, lead=False):
    acc, width = val, 1
    while width < window:
        acc = acc + _shifted(buf, acc, -width if lead else width, S)
        width *= 2
    return acc


def _pool_count(S, window):
    t = lax.broadcasted_iota(jnp.int32, (S, LANES), 0)
    return jnp.minimum(t + 1, window).astype(F32)


def _pool_fwd_group(u, g, window, pw_ref, pb_ref, buf, S):
    pooled = _window_sum(buf, u, window, S) / _pool_count(S, window) - u
    return pooled, _dot(pooled, pw_ref[g]) + pb_ref[g]


def _pool_fwd(proj, pool_w, pool_b, pool_scale, name):
    S = proj.shape[0]
    G = pool_w.shape[0]
    pw = G * LANES

    def body(u_ref, pw_ref, pb_ref, ps_ref, pm_ref, buf):
        _zero_pads(buf, S)
        for g in range(G):
            cols = slice(g * LANES, (g + 1) * LANES)
            _, mixed = _pool_fwd_group(u_ref[:, cols], g, POOL_WINDOWS[g], pw_ref, pb_ref, buf, S)
            pm_ref[:, cols] = (mixed * ps_ref[:, cols]).astype(MM)

    return pl.pallas_call(
        body, name=name, grid=(1,),
        in_specs=[pl.BlockSpec((S, pw), lambda i: (0, 0)), pl.BlockSpec((G, LANES, LANES), lambda i: (0, 0, 0)),
                  pl.BlockSpec((G, 1, LANES), lambda i: (0, 0, 0)), pl.BlockSpec((1, pw), lambda i: (0, 0))],
        out_specs=pl.BlockSpec((S, pw), lambda i: (0, 0)),
        out_shape=jax.ShapeDtypeStruct((S, pw), MM),
        scratch_shapes=[pltpu.VMEM((S + 2 * SHIFT_PAD, LANES), F32)],
        compiler_params=_params(("arbitrary",)),
    )(proj, pool_w, pool_b, pool_scale)


def _pool_bwd(proj, dpm, pool_w, pool_b, pool_scale, name):
    S = proj.shape[0]
    G = pool_w.shape[0]
    pw = G * LANES

    def body(u_ref, dpm_ref, pw_ref, pb_ref, ps_ref, du_ref, dpw_ref, dpb_ref, dps_ref, buf):
        _zero_pads(buf, S)
        for g in range(G):
            cols = slice(g * LANES, (g + 1) * LANES)
            window = POOL_WINDOWS[g]
            pooled, mixed = _pool_fwd_group(u_ref[:, cols], g, window, pw_ref, pb_ref, buf, S)
            d_out = dpm_ref[:, cols]
            dmixed = d_out * ps_ref[:, cols]
            dps_ref[:, cols] = jnp.sum(d_out * mixed, axis=0, keepdims=True)
            dpb_ref[g] = jnp.sum(dmixed, axis=0, keepdims=True)
            dpw_ref[g] = _dot_tn(pooled, dmixed)
            dpooled = _dot_nt(dmixed, pw_ref[g])
            q = dpooled / _pool_count(S, window)
            du_ref[:, cols] = (_window_sum(buf, q, window, S, lead=True) - dpooled).astype(MM)

    return pl.pallas_call(
        body, name=name, grid=(1,),
        in_specs=[pl.BlockSpec((S, pw), lambda i: (0, 0)), pl.BlockSpec((S, pw), lambda i: (0, 0)),
                  pl.BlockSpec((G, LANES, LANES), lambda i: (0, 0, 0)),
                  pl.BlockSpec((G, 1, LANES), lambda i: (0, 0, 0)), pl.BlockSpec((1, pw), lambda i: (0, 0))],
        out_specs=[pl.BlockSpec((S, pw), lambda i: (0, 0)), pl.BlockSpec((G, LANES, LANES), lambda i: (0, 0, 0)),
                   pl.BlockSpec((G, 1, LANES), lambda i: (0, 0, 0)), pl.BlockSpec((1, pw), lambda i: (0, 0))],
        out_shape=[jax.ShapeDtypeStruct((S, pw), MM), jax.ShapeDtypeStruct((G, LANES, LANES), F32),
                   jax.ShapeDtypeStruct((G, 1, LANES), F32), jax.ShapeDtypeStruct((1, pw), F32)],
        scratch_shapes=[pltpu.VMEM((S + 2 * SHIFT_PAD, LANES), F32)],
        compiler_params=_params(("arbitrary",)),
    )(proj, dpm, pool_w, pool_b, pool_scale)


def _scan(a, b, bufs, S, reverse=False):
    pad = S // 2
    seq = pl.ds(pad, S)
    A, B = bufs[:2], bufs[2:]
    A[0][seq, :] = a
    B[0][seq, :] = b
    cur, d = 0, 1
    while d < S:
        sh = pl.ds(pad + d if reverse else pad - d, S)
        a_c = A[cur][seq, :]
        B[1 - cur][seq, :] = a_c * B[cur][sh, :] + B[cur][seq, :]
        if 2 * d < S:
            A[1 - cur][seq, :] = a_c * A[cur][sh, :]
        cur, d = 1 - cur, 2 * d
    return B[cur][seq, :]


def _init_scan_pads(bufs, S):
    pad = S // 2
    for n, buf in enumerate(bufs):
        fill = jnp.full((pad, LANES), 1.0 if n < 2 else 0.0, F32)
        buf[pl.ds(0, pad), :] = fill
        buf[pl.ds(pad + S, pad), :] = fill


def _gelu_and_grad(x):
    c = 0.7978845608028654
    x2 = x * x
    th = jnp.tanh(c * (x + 0.044715 * (x2 * x)))
    gelu = 0.5 * x * (1.0 + th)
    grad = 0.5 * (1.0 + th) + 0.5 * x * (1.0 - th * th) * (c * (1.0 + 3 * 0.044715 * x2))
    return gelu, grad


def _lru_head_fwd(ul, cw, cb, wa, ba, wx, bx, lam, sbuf, bufs, S):
    sbuf[pl.ds(SHIFT_PAD, S), :] = ul
    v = None
    for k in range(CONV_WIDTH):
        term = sbuf[pl.ds(SHIFT_PAD - (CONV_WIDTH - 1 - k), S), :] * cw[k:k + 1, :]
        v = term if v is None else v + term
    v = v + cb
    r = jax.nn.sigmoid(_dot(v, wa) + ba)
    ig = jax.nn.sigmoid(_dot(v, wx) + bx)
    sp = jax.nn.softplus(-lam)
    log_a = -LRU_C * r * sp
    a = jnp.exp(log_a)
    mult = jnp.sqrt(-jnp.tanh(log_a) * (1.0 + a * a))
    h = _scan(a, mult * (ig * v), bufs, S)
    return v, r, ig, sp, a, mult, h


def _lru_specs(S, H, lw, pw):
    b0 = pw // LANES
    return [pl.BlockSpec((S, LANES), lambda h: (0, b0 + h)), pl.BlockSpec((S, LANES), lambda h: (0, b0 + H + h)),
            pl.BlockSpec((CONV_WIDTH, LANES), lambda h: (0, h)), pl.BlockSpec((1, LANES), lambda h: (0, h)),
            pl.BlockSpec((None, LANES, LANES), lambda h: (h, 0, 0)), pl.BlockSpec((None, 1, LANES), lambda h: (h, 0, 0)),
            pl.BlockSpec((None, LANES, LANES), lambda h: (h, 0, 0)), pl.BlockSpec((None, 1, LANES), lambda h: (h, 0, 0)),
            pl.BlockSpec((1, LANES), lambda h: (0, h))]


def _lru_scratch(S):
    return [pltpu.VMEM((S + 2 * SHIFT_PAD, LANES), F32)] + [pltpu.VMEM((2 * S, LANES), F32)] * 4


def _lru_fwd(proj, conv_w, conv_b, wa, ba, wx, bx, lam, pw, name):
    S = proj.shape[0]
    H = wa.shape[0]
    lw = H * LANES

    def body(ul_ref, ug_ref, cw_ref, cb_ref, wa_ref, ba_ref, wx_ref, bx_ref, lam_ref, zl_ref, sbuf, *bufs):
        @pl.when(pl.program_id(0) == 0)
        def _():
            _zero_pads(sbuf, S)
            _init_scan_pads(bufs, S)

        h = _lru_head_fwd(ul_ref[...], cw_ref[...], cb_ref[...], wa_ref[...], ba_ref[...], wx_ref[...], bx_ref[...],
                          lam_ref[...], sbuf, bufs, S)[-1]
        zl_ref[...] = (h * jax.nn.gelu(ug_ref[...])).astype(MM)

    return pl.pallas_call(
        body, name=name, grid=(H,),
        in_specs=_lru_specs(S, H, lw, pw),
        out_specs=pl.BlockSpec((S, LANES), lambda h: (0, h)),
        out_shape=jax.ShapeDtypeStruct((S, lw), MM),
        scratch_shapes=_lru_scratch(S),
        compiler_params=_params(("arbitrary",)),
    )(proj, proj, conv_w, conv_b, wa, ba, wx, bx, lam)


def _lru_bwd(proj, dzl, conv_w, conv_b, wa, ba, wx, bx, lam, pw, name):
    S = proj.shape[0]
    H = wa.shape[0]
    lw = H * LANES

    def body(ul_ref, ug_ref, cw_ref, cb_ref, wa_ref, ba_ref, wx_ref, bx_ref, lam_ref, dzl_ref,
             dul_ref, dug_ref, dcw_ref, dcb_ref, dwa_ref, dba_ref, dwx_ref, dbx_ref, dlam_ref, sbuf, *bufs):
        @pl.when(pl.program_id(0) == 0)
        def _():
            _zero_pads(sbuf, S)
            _init_scan_pads(bufs, S)

        ul, cw, lam = ul_ref[...], cw_ref[...], lam_ref[...]
        wa, wx = wa_ref[...], wx_ref[...]
        v, r, ig, sp, a, mult, h = _lru_head_fwd(ul, cw, cb_ref[...], wa, ba_ref[...], wx, bx_ref[...], lam, sbuf, bufs, S)
        gelu, gelu_grad = _gelu_and_grad(ug_ref[...])
        dzl_t = dzl_ref[...]
        dug_ref[...] = (dzl_t * h * gelu_grad).astype(MM)
        a_next = _shifted(sbuf, a, -1, S)
        lam_t = _scan(a_next, dzl_t * gelu, bufs, S, reverse=True)
        da = lam_t * _shifted(sbuf, h, 1, S)
        d_iv = lam_t * mult
        d_log_a = da * a - (lam_t * (ig * v)) * (a * a) / mult
        dlam_ref[...] = jnp.sum(d_log_a * r, axis=0, keepdims=True) * (LRU_C * jax.nn.sigmoid(-lam))
        dra = (d_log_a * (-LRU_C * sp)) * (r * (1.0 - r))
        dia = (d_iv * v) * (ig * (1.0 - ig))
        dba_ref[...] = jnp.sum(dra, axis=0, keepdims=True)
        dbx_ref[...] = jnp.sum(dia, axis=0, keepdims=True)
        dwa_ref[...] = _dot_tn(v, dra)
        dwx_ref[...] = _dot_tn(v, dia)
        dv = d_iv * ig + _dot_nt(dra, wa) + _dot_nt(dia, wx)
        dcb_ref[...] = jnp.sum(dv, axis=0, keepdims=True)
        sbuf[pl.ds(SHIFT_PAD, S), :] = ul
        for k in range(CONV_WIDTH):
            dcw_ref[k:k + 1, :] = jnp.sum(dv * sbuf[pl.ds(SHIFT_PAD - (CONV_WIDTH - 1 - k), S), :], axis=0, keepdims=True)
        sbuf[pl.ds(SHIFT_PAD, S), :] = dv
        dul = None
        for k in range(CONV_WIDTH):
            term = sbuf[pl.ds(SHIFT_PAD + (CONV_WIDTH - 1 - k), S), :] * cw[k:k + 1, :]
            dul = term if dul is None else dul + term
        dul_ref[...] = dul.astype(MM)

    head_mat = pl.BlockSpec((None, LANES, LANES), lambda h: (h, 0, 0))
    head_vec = pl.BlockSpec((None, 1, LANES), lambda h: (h, 0, 0))
    col = pl.BlockSpec((S, LANES), lambda h: (0, h))
    row = pl.BlockSpec((1, LANES), lambda h: (0, h))
    return pl.pallas_call(
        body, name=name, grid=(H,),
        in_specs=_lru_specs(S, H, lw, pw) + [col],
        out_specs=[col, col, pl.BlockSpec((CONV_WIDTH, LANES), lambda h: (0, h)), row, head_mat, head_vec, head_mat,
                   head_vec, row],
        out_shape=[jax.ShapeDtypeStruct((S, lw), MM), jax.ShapeDtypeStruct((S, lw), MM),
                   jax.ShapeDtypeStruct((CONV_WIDTH, lw), F32), jax.ShapeDtypeStruct((1, lw), F32),
                   jax.ShapeDtypeStruct((H, LANES, LANES), F32), jax.ShapeDtypeStruct((H, 1, LANES), F32),
                   jax.ShapeDtypeStruct((H, LANES, LANES), F32), jax.ShapeDtypeStruct((H, 1, LANES), F32),
                   jax.ShapeDtypeStruct((1, lw), F32)],
        scratch_shapes=_lru_scratch(S),
        compiler_params=_params(("arbitrary",)),
    )(proj, proj, conv_w, conv_b, wa, ba, wx, bx, lam, dzl)


def _loss_head(x, gain, target, name):
    T, D = x.shape
    tm = min(TOKEN_TILE, T)
    ni = T // tm

    def body(x_ref, g_ref, t_ref, loss_ref, dx_ref, dg_ref, loss_acc, dg_acc):
        i = pl.program_id(0)
        g = g_ref[...]
        r, xh, y = _rms(x_ref[...], g)
        err = y - t_ref[...]
        part = 0.5 * jnp.sum(jnp.mean(err * err, axis=-1, keepdims=True), axis=0, keepdims=True)
        _accumulate(loss_acc, jnp.broadcast_to(part, (1, LANES)), i == 0)
        dy = err * (1.0 / D)
        _accumulate(dg_acc, jnp.sum(dy * xh, axis=0, keepdims=True), i == 0)
        dx_ref[...] = _rms_bwd(dy, xh, r, g)

        @pl.when(i == ni - 1)
        def _():
            loss_ref[...] = loss_acc[...]
            dg_ref[...] = dg_acc[...]

    return pl.pallas_call(
        body, name=name, grid=(ni,),
        in_specs=[pl.BlockSpec((tm, D), lambda i: (i, 0)), pl.BlockSpec((1, D), lambda i: (0, 0)),
                  pl.BlockSpec((tm, D), lambda i: (i, 0))],
        out_specs=[pl.BlockSpec((1, LANES), lambda i: (0, 0)), pl.BlockSpec((tm, D), lambda i: (i, 0)),
                   pl.BlockSpec((1, D), lambda i: (0, 0))],
        out_shape=[jax.ShapeDtypeStruct((1, LANES), F32), jax.ShapeDtypeStruct((T, D), F32),
                   jax.ShapeDtypeStruct((1, D), F32)],
        scratch_shapes=[pltpu.VMEM((1, LANES), F32), pltpu.VMEM((1, D), F32)],
        compiler_params=_params(("arbitrary",)),
    )(x, gain, target)


def _pair_sum(a, recv, c, name):
    n, R, C = a.shape
    hr = R // 2

    def body(c_ref, a_ref, r_ref, pf_ref, pb_ref):
        s = a_ref[...] + r_ref[...]
        pf_ref[...] = s
        pb_ref[...] = s.astype(XFER)

    piece = pl.BlockSpec((None, hr, C), lambda k, c_ref: (k, 0, 0))
    return pl.pallas_call(
        body, name=name,
        grid_spec=pltpu.PrefetchScalarGridSpec(
            num_scalar_prefetch=1, grid=(n,),
            in_specs=[pl.BlockSpec((None, hr, C), lambda k, c_ref: (k, c_ref[0], 0)), piece],
            out_specs=[piece, piece]),
        out_shape=[jax.ShapeDtypeStruct((n, hr, C), F32), jax.ShapeDtypeStruct((n, hr, C), XFER)],
        compiler_params=_params(("arbitrary",)),
    )(c, a, recv)


def _chip_sum(own, others, name):
    hr, C = own.shape
    rb = _row_block(hr, C)

    def body(o_ref, q_ref, g_ref):
        g_ref[...] = ((o_ref[...] + q_ref[0].astype(F32)) + q_ref[1].astype(F32)) + q_ref[2].astype(F32)

    return pl.pallas_call(
        body, name=name, grid=(hr // rb,),
        in_specs=[pl.BlockSpec((rb, C), lambda i: (i, 0)), pl.BlockSpec((N_CHIPS - 1, rb, C), lambda i: (0, i, 0))],
        out_specs=pl.BlockSpec((rb, C), lambda i: (i, 0)),
        out_shape=jax.ShapeDtypeStruct((hr, C), F32),
        compiler_params=_params(("arbitrary",)),
    )(own, others)


def _adamw(w, g, m, v, name):
    L, R, C = w.shape
    rb = _row_block(R, C)

    def body(w_ref, g_ref, m_ref, v_ref, go_ref, d_ref, mo_ref, vo_ref):
        g_t = g_ref[...]
        m_t = ADAM_B1 * m_ref[...] + (1.0 - ADAM_B1) * g_t
        v_t = ADAM_B2 * v_ref[...] + (1.0 - ADAM_B2) * (g_t * g_t)
        m_hat = m_t / (1.0 - ADAM_B1 ** ADAM_STEP)
        v_hat = v_t / (1.0 - ADAM_B2 ** ADAM_STEP)
        go_ref[...] = g_t
        d_ref[...] = -ADAM_LR * (m_hat / (jnp.sqrt(v_hat) + ADAM_EPS) + ADAM_WD * w_ref[...])
        mo_ref[...] = m_t
        vo_ref[...] = v_t

    blk = pl.BlockSpec((None, rb, C), lambda l, i: (l, i, 0))
    return pl.pallas_call(
        body, name=name, grid=(L, R // rb), in_specs=[blk] * 4, out_specs=[blk] * 4,
        out_shape=[jax.ShapeDtypeStruct((L, R, C), F32)] * 4,
        compiler_params=_params(("arbitrary", "arbitrary")),
    )(w, g, m, v)


def _place():
    x, y, c = lax.axis_index("x"), lax.axis_index("y"), lax.axis_index("c")
    others = [(1 - x, y), (x, 1 - y), (1 - x, 1 - y)]
    return x, y, c, 2 * x + y, others


def _half(c, rows):
    return pl.ds(pl.multiple_of(c * (rows // 2), 16), rows // 2)


def _gather_weights(shards, layer, small, name):
    n = len(shards)
    n_out = n + (small is not None)

    def body(*refs):
        ins, outs = refs[:n_out], refs[n_out:2 * n_out]
        send, recv, fsend, frecv, local = refs[2 * n_out:]
        x, y, c, k, others = _place()
        sib = (x, y, 1 - c)

        def copy(a, r, to, src_slot, rows, sems):
            whole = rows is None
            src = ins[a].at[layer] if src_slot is None else outs[a].at[src_slot]
            dst = outs[a].at[k if src_slot is None else src_slot]
            if not whole:
                src, dst = src.at[rows], dst.at[rows]
            return pltpu.make_async_remote_copy(src_ref=src, dst_ref=dst, send_sem=sems[0].at[a * 3 + r],
                                                recv_sem=sems[1].at[a * 3 + r], device_id=to, device_id_type=MESH)

        mine = [pltpu.make_async_copy(ins[a].at[layer] if a < n else ins[a], outs[a].at[k], local.at[a])
                for a in range(n_out)]
        for cp in mine:
            cp.start()
        first, passed, other_k = [], [], []
        for a in range(n_out):
            is_small = a >= n
            R = ins[a].shape[-2]
            rows = None if is_small else _half(c, R)
            for r, (px, py) in enumerate(others):
                if is_small:
                    cp = pltpu.make_async_remote_copy(src_ref=ins[a], dst_ref=outs[a].at[k], send_sem=send.at[a * 3 + r],
                                                      recv_sem=recv.at[a * 3 + r], device_id=(px, py, c), device_id_type=MESH)
                else:
                    cp = copy(a, r, (px, py, c), None, rows, (send, recv))
                cp.start()
                first.append(cp)
        for a in range(n):
            R = ins[a].shape[-2]
            for r, (px, py) in enumerate(others):
                pk = 2 * px + py
                copy(a, r, (px, py, c), pk, _half(c, R), (send, recv)).wait_recv()
                fw = copy(a, r, sib, pk, _half(c, R), (fsend, frecv))
                fw.start()
                passed.append(fw)
        if small is not None:
            for r, (px, py) in enumerate(others):
                pltpu.make_async_remote_copy(src_ref=ins[n], dst_ref=outs[n].at[2 * px + py], send_sem=send.at[n * 3 + r],
                                             recv_sem=recv.at[n * 3 + r], device_id=(px, py, c),
                                             device_id_type=MESH).wait_recv()
        for a in range(n):
            R = ins[a].shape[-2]
            for r, (px, py) in enumerate(others):
                copy(a, r, sib, 2 * px + py, _half(1 - c, R), (fsend, frecv)).wait_recv()
        for cp in first + passed:
            cp.wait_send()
        for cp in mine:
            cp.wait()

    operands = list(shards) + ([small] if small is not None else [])
    out_shape = [jax.ShapeDtypeStruct((N_CHIPS,) + s.shape[1:], s.dtype) for s in shards]
    if small is not None:
        out_shape.append(jax.ShapeDtypeStruct((N_CHIPS,) + small.shape, small.dtype))
    return pl.pallas_call(
        body, name=name, in_specs=[ANY] * n_out, out_specs=[ANY] * n_out, out_shape=out_shape,
        scratch_shapes=[pltpu.SemaphoreType.DMA((3 * n_out,))] * 4 + [pltpu.SemaphoreType.DMA((n_out,))],
    )(*operands)


def _exchange_halves(grads, name):
    n = len(grads)

    def body(*refs):
        ins, outs, send, recv = refs[:n], refs[n:2 * n], refs[2 * n], refs[2 * n + 1]
        x, y, c, _, _ = _place()
        copies = []
        for a in range(n):
            R = ins[a].shape[1]
            cp = pltpu.make_async_remote_copy(src_ref=ins[a].at[:, _half(1 - c, R), :], dst_ref=outs[a],
                                              send_sem=send.at[a], recv_sem=recv.at[a], device_id=(x, y, 1 - c),
                                              device_id_type=MESH)
            cp.start()
            copies.append(cp)
        for cp in copies:
            cp.wait()

    return pl.pallas_call(
        body, name=name, in_specs=[ANY] * n, out_specs=[ANY] * n,
        out_shape=[jax.ShapeDtypeStruct((g.shape[0], g.shape[1] // 2, g.shape[2]), g.dtype) for g in grads],
        scratch_shapes=[pltpu.SemaphoreType.DMA((n,))] * 2,
    )(*grads)


def _exchange_chips(pf, pb, name):
    n = len(pf)

    def body(*refs):
        pfs, pbs = refs[:n], refs[n:2 * n]
        owns, gots = refs[2 * n:3 * n], refs[3 * n:4 * n]
        send, recv, local = refs[4 * n:]
        x, y, c, k, others = _place()
        mine = [pltpu.make_async_copy(pfs[a].at[k], owns[a], local.at[a]) for a in range(n)]
        for cp in mine:
            cp.start()
        copies = []
        for a in range(n):
            for r, (px, py) in enumerate(others):
                cp = pltpu.make_async_remote_copy(src_ref=pbs[a].at[2 * px + py], dst_ref=gots[a].at[r],
                                                  send_sem=send.at[a * 3 + r], recv_sem=recv.at[a * 3 + r],
                                                  device_id=(px, py, c), device_id_type=MESH)
                cp.start()
                copies.append(cp)
        for cp in copies:
            cp.wait()
        for cp in mine:
            cp.wait()

    return pl.pallas_call(
        body, name=name, in_specs=[ANY] * (2 * n), out_specs=[ANY] * (2 * n),
        out_shape=[jax.ShapeDtypeStruct(p.shape[1:], F32) for p in pf]
        + [jax.ShapeDtypeStruct((N_CHIPS - 1,) + p.shape[1:], XFER) for p in pb],
        scratch_shapes=[pltpu.SemaphoreType.DMA((3 * n,))] * 2 + [pltpu.SemaphoreType.DMA((n,))],
    )(*pf, *pb)


def _join_halves(halves, fulls, layer, name):
    n = len(halves)

    def body(*refs):
        hs, outs = refs[:n], refs[2 * n:3 * n]
        send, recv, local = refs[3 * n:]
        x, y, c, _, _ = _place()
        copies, mine = [], []
        for a in range(n):
            R = outs[a].shape[1]
            dst = outs[a].at[layer, _half(c, R), :]
            mine.append(pltpu.make_async_copy(hs[a], dst, local.at[a]))
            mine[-1].start()
            cp = pltpu.make_async_remote_copy(src_ref=hs[a], dst_ref=dst, send_sem=send.at[a], recv_sem=recv.at[a],
                                              device_id=(x, y, 1 - c), device_id_type=MESH)
            cp.start()
            copies.append(cp)
        for a in range(n):
            R = outs[a].shape[1]
            pltpu.make_async_remote_copy(src_ref=hs[a], dst_ref=outs[a].at[layer, _half(1 - c, R), :], send_sem=send.at[a],
                                         recv_sem=recv.at[a], device_id=(x, y, 1 - c), device_id_type=MESH).wait_recv()
        for cp in copies:
            cp.wait_send()
        for cp in mine:
            cp.wait()

    return pl.pallas_call(
        body, name=name, in_specs=[ANY] * (2 * n), out_specs=[ANY] * n,
        out_shape=[jax.ShapeDtypeStruct(f.shape, f.dtype) for f in fulls],
        input_output_aliases={n + a: a for a in range(n)},
        scratch_shapes=[pltpu.SemaphoreType.DMA((n,))] * 3,
    )(*halves, *fulls)


def _all_reduce_small(packed, name):
    _, rows, lanes = packed.shape

    def body(in_ref, out_ref, stage, send1, recv1, send2, recv2):
        x, y, c, _, _ = _place()
        me = 4 * x + 2 * y + c
        ids = [(p // 4, (p // 2) % 2, p % 2) for p in range(N_DEV)]

        def scatter(p):
            return pltpu.make_async_remote_copy(src_ref=in_ref.at[p], dst_ref=stage.at[me], send_sem=send1.at[p],
                                                recv_sem=recv1.at[me], device_id=ids[p], device_id_type=MESH)

        def spread(p, to):
            return pltpu.make_async_remote_copy(src_ref=out_ref.at[p], dst_ref=out_ref.at[p], send_sem=send2.at[to],
                                                recv_sem=recv2.at[p], device_id=ids[to], device_id_type=MESH)

        for p in range(N_DEV):
            @pl.when(me != p)
            def _(p=p):
                scatter(p).start()

        total = None
        for p in range(N_DEV):
            @pl.when(me != p)
            def _(p=p):
                pltpu.make_async_remote_copy(src_ref=in_ref.at[0], dst_ref=stage.at[p], send_sem=send1.at[p],
                                             recv_sem=recv1.at[p], device_id=ids[p], device_id_type=MESH).wait_recv()

            @pl.when(me == p)
            def _(p=p):
                stage[p] = in_ref[me]

        for p in range(N_DEV):
            total = stage[p] if total is None else total + stage[p]
        out_ref[me] = total
        for p in range(N_DEV):
            @pl.when(me != p)
            def _(p=p):
                scatter(p).wait_send()
                spread(me, p).start()
        for p in range(N_DEV):
            @pl.when(me != p)
            def _(p=p):
                spread(p, p).wait_recv()
        for p in range(N_DEV):
            @pl.when(me != p)
            def _(p=p):
                spread(me, p).wait_send()

    vmem = pl.BlockSpec(memory_space=pltpu.VMEM)
    return pl.pallas_call(
        body, name=name, in_specs=[vmem], out_specs=vmem,
        out_shape=jax.ShapeDtypeStruct(packed.shape, F32),
        scratch_shapes=[pltpu.VMEM(packed.shape, F32)] + [pltpu.SemaphoreType.DMA((N_DEV,))] * 4,
        compiler_params=pltpu.CompilerParams(vmem_limit_bytes=VMEM_LIMIT),
    )(packed)


BIG = ("ffn1_w_up", "ffn1_w_down", "w_in", "w_pool_up", "w_lru_up", "w_out", "ffn2_w_up", "ffn2_w_down")
SMALL = ("norm_ffn1", "norm_mix", "pool_w", "pool_b", "pool_scale", "conv_w", "conv_b", "lru_w_a", "lru_b_a", "lru_w_x",
         "lru_b_x", "lru_lambda", "norm_ffn2", "final_norm")
WEIGHTS = ("norm_ffn1", "ffn1_w_up", "ffn1_w_down", "norm_mix", "w_in", "pool_w", "pool_b", "pool_scale", "w_pool_up",
           "conv_w", "conv_b", "lru_w_a", "lru_b_a", "lru_w_x", "lru_b_x", "lru_lambda", "w_lru_up", "w_out", "norm_ffn2",
           "ffn2_w_up", "ffn2_w_down", "final_norm")


def _pack(arrays, rows_multiple):
    flat = jnp.concatenate([a.reshape(-1) for a in arrays])
    rows = -(-flat.shape[0] // LANES)
    rows = -(-rows // rows_multiple) * rows_multiple
    return jnp.pad(flat, (0, rows * LANES - flat.shape[0])).reshape(rows, LANES)


def _unpack(packed, like):
    flat, out, at = packed.reshape(-1), [], 0
    for a in like:
        out.append(flat[at:at + a.size].reshape(a.shape))
        at += a.size
    return out


def kernel(x, norm_ffn1, ffn1_w_up, ffn1_w_down, norm_mix, w_in, pool_w, pool_b, pool_scale, w_pool_up, conv_w, conv_b, lru_w_a, lru_b_a, lru_w_x, lru_b_x, lru_lambda, w_lru_up, w_out, norm_ffn2, ffn2_w_up, ffn2_w_down, final_norm, loss_target, m_norm_ffn1, m_ffn1_w_up, m_ffn1_w_down, m_norm_mix, m_w_in, m_pool_w, m_pool_b, m_pool_scale, m_w_pool_up, m_conv_w, m_conv_b, m_lru_w_a, m_lru_b_a, m_lru_w_x, m_lru_b_x, m_lru_lambda, m_w_lru_up, m_w_out, m_norm_ffn2, m_ffn2_w_up, m_ffn2_w_down, m_final_norm, v_norm_ffn1, v_ffn1_w_up, v_ffn1_w_down, v_norm_mix, v_w_in, v_pool_w, v_pool_b, v_pool_scale, v_w_pool_up, v_conv_w, v_conv_b, v_lru_w_a, v_lru_b_a, v_lru_w_x, v_lru_b_x, v_lru_lambda, v_w_lru_up, v_w_out, v_norm_ffn2, v_ffn2_w_up, v_ffn2_w_down, v_final_norm):
    given = dict(locals())
    W = {n: given[n] for n in WEIGHTS}
    M = {n: given["m_" + n] for n in WEIGHTS}
    V = {n: given["v_" + n] for n in WEIGHTS}
    L = norm_ffn1.shape[0]
    T, D = x.shape[1], x.shape[2]
    G, H = pool_w.shape[1], lru_w_a.shape[1]
    pw, lw = G * LANES, H * LANES
    chip = 2 * lax.axis_index("x") + lax.axis_index("y")
    core = lax.axis_index("c").astype(jnp.int32).reshape(1)

    shards = [W[n].astype(MM) for n in BIG]
    conv_shard = conv_w.reshape(L * CONV_WIDTH, conv_w.shape[2])

    xs = x.reshape(T, D)
    saved = []
    conv_full = None
    for l in range(L):
        got = _gather_weights(shards, l, conv_shard if l == 0 else None, f"gather_weights_{l}")
        if l == 0:
            conv_full = got[-1].reshape(N_CHIPS, L, CONV_WIDTH, -1).transpose(1, 2, 0, 3).reshape(L, CONV_WIDTH, lw)
        w_up1, w_dn1, w_i, w_pu, w_lu, w_o, w_up2, w_dn2 = got[:8]
        w_dn1, w_dn2 = w_dn1.reshape(-1, D), w_dn2.reshape(-1, D)
        w_lu, w_o = w_lu.reshape(-1, D), w_o.reshape(-1, D)
        small = dict(pool_w=pool_w[l], pool_b=pool_b[l].reshape(G, 1, LANES), pool_scale=pool_scale[l].reshape(1, pw),
                     conv_w=conv_full[l], conv_b=conv_b[l].reshape(1, lw), wa=lru_w_a[l],
                     ba=lru_b_a[l].reshape(H, 1, LANES), wx=lru_w_x[l], bx=lru_b_x[l].reshape(H, 1, LANES),
                     lam=lru_lambda[l].reshape(1, lw))
        g1, g2, g3 = norm_ffn1[l].reshape(1, D), norm_mix[l].reshape(1, D), norm_ffn2[l].reshape(1, D)

        x0 = xs
        u1 = _norm_matmul_fwd(x0, g1, w_up1, f"ffn1_up_{l}")
        x1 = _swiglu_down_fwd(u1, x0, w_dn1, f"ffn1_down_{l}")
        proj = _norm_matmul_fwd(x1, g2, w_i, f"mix_in_{l}")
        pm = _pool_fwd(proj, small["pool_w"], small["pool_b"], small["pool_scale"], f"pool_{l}")
        zl = _lru_fwd(proj, small["conv_w"], small["conv_b"], small["wa"], small["ba"], small["wx"], small["bx"],
                      small["lam"], pw, f"lru_{l}")
        x2 = _mix_out_fwd(x1, proj, pm, zl, w_pu, w_lu, w_o, f"mix_out_{l}")
        u2 = _norm_matmul_fwd(x2, g3, w_up2, f"ffn2_up_{l}")
        xs = _swiglu_down_fwd(u2, x2, w_dn2, f"ffn2_down_{l}")
        saved.append(dict(x0=x0, u1=u1, x1=x1, proj=proj, pm=pm, zl=zl, x2=x2, u2=u2, small=small, g=(g1, g2, g3),
                          w=(w_up1, w_dn1, w_i, w_pu, w_lu, w_o, w_up2, w_dn2)))

    loss_part, dx, d_final = _loss_head(xs, final_norm.reshape(1, D), loss_target.reshape(T, D), "loss_head")
    loss = lax.psum(loss_part[0, 0], ("x", "y", "c"))

    full = [jnp.zeros(W[n].shape, F32) for n in BIG]
    small_grads = [None] * L
    du_spec_ffn = lambda tm, tn: pl.BlockSpec((None, tm, tn), lambda k, i: (k // 2, i, k % 2))
    du_spec_mix = lambda tm, tn: pl.BlockSpec((tm, tn), lambda k, i: (i, k))
    for l in reversed(range(L)):
        s = saved[l]
        w_up1, w_dn1, w_i, w_pu, w_lu, w_o, w_up2, w_dn2 = s["w"]
        g1, g2, g3 = s["g"]
        sm = s["small"]
        du2, d_dn2 = _swiglu_down_bwd(s["u2"], dx, w_dn2, f"ffn2_down_bwd_{l}")
        dx, d_up2, dg3 = _norm_matmul_bwd(s["x2"], g3, du2, du_spec_ffn, w_up2, dx, f"ffn2_up_bwd_{l}")
        dgl, dpm, dzl, d_o, d_pu, d_lu = _mix_out_bwd(dx, s["proj"], s["pm"], s["zl"], w_pu, w_lu, w_o, f"mix_out_bwd_{l}")
        du_pool, d_pool_w, d_pool_b, d_pool_scale = _pool_bwd(s["proj"], dpm, sm["pool_w"], sm["pool_b"], sm["pool_scale"],
                                                            f"pool_bwd_{l}")
        dul, dug, d_cw, d_cb, d_wa, d_ba, d_wx, d_bx, d_lam = _lru_bwd(
            s["proj"], dzl, sm["conv_w"], sm["conv_b"], sm["wa"], sm["ba"], sm["wx"], sm["bx"], sm["lam"], pw, f"lru_bwd_{l}")
        dproj = jnp.concatenate([du_pool, dul, dug, dgl], axis=1)
        dx, d_in, dg2 = _norm_matmul_bwd(s["x1"], g2, dproj, du_spec_mix, w_i, dx, f"mix_in_bwd_{l}")
        du1, d_dn1 = _swiglu_down_bwd(s["u1"], dx, w_dn1, f"ffn1_down_bwd_{l}")
        dx, d_up1, dg1 = _norm_matmul_bwd(s["x0"], g1, du1, du_spec_ffn, w_up1, dx, f"ffn1_up_bwd_{l}")
        small_grads[l] = dict(norm_ffn1=dg1, norm_mix=dg2, pool_w=d_pool_w, pool_b=d_pool_b, pool_scale=d_pool_scale,
                              conv_w=d_cw, conv_b=d_cb, lru_w_a=d_wa, lru_b_a=d_ba, lru_w_x=d_wx, lru_b_x=d_bx,
                              lru_lambda=d_lam, norm_ffn2=dg3)

        grads = [d_up1, d_dn1.reshape(N_CHIPS, -1, D), d_in, d_pu, d_lu.reshape(N_CHIPS, -1, D),
                 d_o.reshape(N_CHIPS, -1, D), d_up2, d_dn2.reshape(N_CHIPS, -1, D)]
        recv = _exchange_halves(grads, f"exchange_halves_{l}")
        sums = [_pair_sum(g, r, core, f"pair_sum_{n}_{l}") for g, r, n in zip(grads, recv, BIG)]
        res = _exchange_chips([p[0] for p in sums], [p[1] for p in sums], f"exchange_chips_{l}")
        halves = [_chip_sum(o, q, f"chip_sum_{n}_{l}") for o, q, n in zip(res[:len(BIG)], res[len(BIG):], BIG)]
        full = _join_halves(halves, full, l, f"join_halves_{l}")

    small_full = {n: jnp.stack([small_grads[l][n].reshape(W[n].shape[1:] if n != "conv_w" else (CONV_WIDTH, lw))
                                for l in range(L)]) for n in SMALL if n != "final_norm"}
    small_full["final_norm"] = d_final.reshape(D)
    small_list = [small_full[n] for n in SMALL]
    packed = _pack(small_list, 8 * N_DEV)
    summed = _all_reduce_small(packed.reshape(N_DEV, -1, LANES), "all_reduce_small")
    small_sum = dict(zip(SMALL, _unpack(summed, small_list)))
    cs = conv_w.shape[2]
    small_sum["conv_w"] = lax.dynamic_slice_in_dim(small_sum["conv_w"], chip * cs, cs, axis=2)

    out = {}
    for n, g in zip(BIG, full):
        out[n] = _adamw(W[n], g, M[n], V[n], f"adamw_{n}")
    packs = [_pack([d[n] for n in SMALL], 16) for d in (W, small_sum, M, V)]
    res = _adamw(*[p[None] for p in packs], "adamw_small")
    like = [W[n] for n in SMALL]
    unpacked = [_unpack(r[0], like) for r in res]
    for j, n in enumerate(SMALL):
        out[n] = tuple(u[j] for u in unpacked)

    return (loss, dx.reshape(x.shape), *[out[n][0] for n in WEIGHTS], *[out[n][1] for n in WEIGHTS],
            *[out[n][2] for n in WEIGHTS], *[out[n][3] for n in WEIGHTS])
```

```python
import functools

import jax
import jax.numpy as jnp
from jax import lax
from jax.experimental import pallas as pl
from jax.experimental.pallas import tpu as pltpu

F32 = jnp.float32
MM = jnp.bfloat16
XFER = jnp.bfloat16

EPS = 1e-6
LRU_C = 8.0
POOL_WINDOWS = (2, 4, 8, 16)
CONV_WIDTH = 4
ADAM_LR, ADAM_B1, ADAM_B2, ADAM_EPS, ADAM_WD, ADAM_STEP = 0.001, 0.9, 0.999, 1e-08, 0.01, 10

N_CHIPS = 4
N_DEV = 8
LANES = 128
SHIFT_PAD = 8
TOKEN_TILE = 256
VMEM_LIMIT = 52 * 1024 * 1024
MESH = pl.DeviceIdType.MESH
ANY = pl.BlockSpec(memory_space=pl.ANY)


def _params(sem=None):
    return pltpu.CompilerParams(dimension_semantics=sem, vmem_limit_bytes=VMEM_LIMIT)


def _dot(a, b):
    return jnp.dot(a.astype(MM), b.astype(MM), preferred_element_type=F32)


def _dot_nt(a, b):
    return lax.dot_general(a.astype(MM), b.astype(MM), (((1,), (1,)), ((), ())), preferred_element_type=F32)


def _dot_tn(a, b):
    return lax.dot_general(a.astype(MM), b.astype(MM), (((0,), (0,)), ((), ())), preferred_element_type=F32)


def _rms(x, g):
    r = lax.rsqrt(jnp.mean(x * x, axis=-1, keepdims=True) + EPS)
    xh = x * r
    return r, xh, xh * g


def _rms_bwd(dh, xh, r, g):
    dxh = dh * g
    return r * (dxh - xh * jnp.mean(dxh * xh, axis=-1, keepdims=True))


def _accumulate(ref, val, first):
    @pl.when(first)
    def _():
        ref[...] = val

    @pl.when(jnp.logical_not(first))
    def _():
        ref[...] += val


def _row_block(rows, cols, itemsize=4, budget=1 << 20):
    best = None
    for rb in range(16, rows + 1, 16):
        if rows % rb == 0 and rb * cols * itemsize <= budget:
            best = rb
    return best if best is not None else rows


def _norm_matmul_fwd(x, gain, w, name):
    T, D = x.shape
    K, _, tn = w.shape
    tm = min(TOKEN_TILE, T)

    def body(x_ref, g_ref, w_ref, u_ref):
        _, _, h = _rms(x_ref[...], g_ref[...])
        u_ref[...] = _dot(h, w_ref[...])

    return pl.pallas_call(
        body, name=name, grid=(K, T // tm),
        in_specs=[pl.BlockSpec((tm, D), lambda k, i: (i, 0)), pl.BlockSpec((1, D), lambda k, i: (0, 0)),
                  pl.BlockSpec((None, D, tn), lambda k, i: (k, 0, 0))],
        out_specs=pl.BlockSpec((tm, tn), lambda k, i: (i, k)),
        out_shape=jax.ShapeDtypeStruct((T, K * tn), F32),
        compiler_params=_params(("arbitrary", "arbitrary")),
    )(x, gain, w)


def _norm_matmul_bwd(x, gain, du, du_spec, w, dres, name):
    T, D = x.shape
    K, _, tn = w.shape
    tm = min(TOKEN_TILE, T)
    ni = T // tm

    def body(x_ref, g_ref, du_ref, w_ref, dres_ref, dx_ref, dw_ref, dg_ref, dh_acc, dg_acc):
        k, i = pl.program_id(0), pl.program_id(1)
        g = g_ref[...]
        r, xh, h = _rms(x_ref[...], g)
        du_t = du_ref[...].astype(MM)
        rows = pl.ds(pl.multiple_of(i * tm, tm), tm)
        part = _dot_nt(du_t, w_ref[...])

        @pl.when(k == 0)
        def _():
            dh_acc[rows, :] = part

        @pl.when(k > 0)
        def _():
            dh_acc[rows, :] += part

        _accumulate(dw_ref, _dot_tn(h, du_t), i == 0)

        @pl.when(k == K - 1)
        def _():
            dh = dh_acc[rows, :]
            _accumulate(dg_acc, jnp.sum(dh * xh, axis=0, keepdims=True), i == 0)
            dx_ref[...] = dres_ref[...] + _rms_bwd(dh, xh, r, g)

            @pl.when(i == ni - 1)
            def _():
                dg_ref[...] = dg_acc[...]

    def last(k, i):
        return (jnp.where(k == K - 1, i, 0), 0)

    return pl.pallas_call(
        body, name=name, grid=(K, ni),
        in_specs=[pl.BlockSpec((tm, D), lambda k, i: (i, 0)), pl.BlockSpec((1, D), lambda k, i: (0, 0)),
                  du_spec(tm, tn), pl.BlockSpec((None, D, tn), lambda k, i: (k, 0, 0)),
                  pl.BlockSpec((tm, D), last)],
        out_specs=[pl.BlockSpec((tm, D), last), pl.BlockSpec((None, D, tn), lambda k, i: (k, 0, 0)),
                   pl.BlockSpec((1, D), lambda k, i: (0, 0))],
        out_shape=[jax.ShapeDtypeStruct((T, D), F32), jax.ShapeDtypeStruct((K, D, tn), F32),
                   jax.ShapeDtypeStruct((1, D), F32)],
        scratch_shapes=[pltpu.VMEM((T, D), F32), pltpu.VMEM((1, D), F32)],
        compiler_params=_params(("arbitrary", "arbitrary")),
    )(x, gain, du, w, dres)


def _swiglu_down_fwd(u, x, wd, name):
    T, D = x.shape
    Fh = wd.shape[0]
    tm = min(TOKEN_TILE, T)

    def body(a_ref, b_ref, x_ref, wd_ref, o_ref):
        a = a_ref[...]
        s = a * jax.nn.sigmoid(a) * b_ref[...]
        o_ref[...] = x_ref[...] + 0.5 * _dot(s, wd_ref[...])

    return pl.pallas_call(
        body, name=name, grid=(T // tm,),
        in_specs=[pl.BlockSpec((tm, Fh), lambda i: (i, 0)), pl.BlockSpec((tm, Fh), lambda i: (i, 1)),
                  pl.BlockSpec((tm, D), lambda i: (i, 0)), pl.BlockSpec((Fh, D), lambda i: (0, 0))],
        out_specs=pl.BlockSpec((tm, D), lambda i: (i, 0)),
        out_shape=jax.ShapeDtypeStruct((T, D), F32),
        compiler_params=_params(("arbitrary",)),
    )(u, u, x, wd)


def _swiglu_down_bwd(u, dxn, wd, name):
    T, D = dxn.shape
    Fh = wd.shape[0]
    tm = min(TOKEN_TILE, T)
    nj = 2 if Fh % (2 * LANES) == 0 else 1
    tf = Fh // nj

    def body(a_ref, b_ref, dxn_ref, wd_ref, du_ref, dwd_ref):
        i = pl.program_id(1)
        a, b = a_ref[...], b_ref[...]
        dyh = (0.5 * dxn_ref[...]).astype(MM)
        ds = _dot_nt(dyh, wd_ref[...])
        sig = jax.nn.sigmoid(a)
        sa = a * sig
        _accumulate(dwd_ref, _dot_tn(sa * b, dyh), i == 0)
        du_ref[0] = (ds * b * (sig * (1.0 + a * (1.0 - sig)))).astype(MM)
        du_ref[1] = (ds * sa).astype(MM)

    return pl.pallas_call(
        body, name=name, grid=(nj, T // tm),
        in_specs=[pl.BlockSpec((tm, tf), lambda j, i: (i, j)), pl.BlockSpec((tm, tf), lambda j, i: (i, j + nj)),
                  pl.BlockSpec((tm, D), lambda j, i: (i, 0)), pl.BlockSpec((tf, D), lambda j, i: (j, 0))],
        out_specs=[pl.BlockSpec((2, tm, tf), lambda j, i: (0, i, j)), pl.BlockSpec((tf, D), lambda j, i: (j, 0))],
        out_shape=[jax.ShapeDtypeStruct((2, T, Fh), MM), jax.ShapeDtypeStruct((Fh, D), F32)],
        compiler_params=_params(("arbitrary", "arbitrary")),
    )(u, u, dxn, wd)


def _mix_branches(pm, zl, gp_logit, gr_logit, wpu_ref, wlu_ref):
    y_pool = jnp.concatenate([_dot(pm, wpu_ref[k]) for k in range(N_CHIPS)], axis=1)
    y_lru = _dot(zl, wlu_ref[...])
    return y_pool, y_lru, jax.nn.sigmoid(gp_logit), jax.nn.sigmoid(gr_logit)


def _mix_out_specs(tm, D, pw, lw, cs):
    gate0 = (pw + 2 * lw) // D
    return [pl.BlockSpec((tm, D), lambda i: (i, gate0)), pl.BlockSpec((tm, D), lambda i: (i, gate0 + 1)),
            pl.BlockSpec((tm, pw), lambda i: (i, 0)), pl.BlockSpec((tm, lw), lambda i: (i, 0)),
            pl.BlockSpec((N_CHIPS, pw, cs), lambda i: (0, 0, 0)), pl.BlockSpec((lw, D), lambda i: (0, 0)),
            pl.BlockSpec((D, D), lambda i: (0, 0))]


def _mix_out_fwd(x, proj, pm, zl, wpu, wlu, wo, name):
    T, D = x.shape
    pw, lw, cs = pm.shape[1], zl.shape[1], wpu.shape[2]
    assert (pw + 2 * lw) % D == 0
    tm = min(TOKEN_TILE, T)

    def body(x_ref, gp_ref, gr_ref, pm_ref, zl_ref, wpu_ref, wlu_ref, wo_ref, o_ref):
        y_pool, y_lru, gp, gr = _mix_branches(pm_ref[...], zl_ref[...], gp_ref[...], gr_ref[...], wpu_ref, wlu_ref)
        o_ref[...] = x_ref[...] + _dot(gp * y_pool + gr * y_lru, wo_ref[...])

    return pl.pallas_call(
        body, name=name, grid=(T // tm,),
        in_specs=[pl.BlockSpec((tm, D), lambda i: (i, 0))] + _mix_out_specs(tm, D, pw, lw, cs),
        out_specs=pl.BlockSpec((tm, D), lambda i: (i, 0)),
        out_shape=jax.ShapeDtypeStruct((T, D), F32),
        compiler_params=_params(("arbitrary",)),
    )(x, proj, proj, pm, zl, wpu, wlu, wo)


def _mix_out_bwd(dxn, proj, pm, zl, wpu, wlu, wo, name):
    T, D = dxn.shape
    pw, lw, cs = pm.shape[1], zl.shape[1], wpu.shape[2]
    tm = min(TOKEN_TILE, T)
    ni = T // tm

    def body(dxn_ref, gp_ref, gr_ref, pm_ref, zl_ref, wpu_ref, wlu_ref, wo_ref,
             dgl_ref, dpm_ref, dzl_ref, dwo_hbm, dwpu_hbm, dwlu_hbm, acc_o, acc_pu, acc_lu, sem):
        i = pl.program_id(0)
        pm, zl = pm_ref[...], zl_ref[...]
        y_pool, y_lru, gp, gr = _mix_branches(pm, zl, gp_ref[...], gr_ref[...], wpu_ref, wlu_ref)
        dxn_t = dxn_ref[...].astype(MM)
        dmerged = _dot_nt(dxn_t, wo_ref[...])
        _accumulate(acc_o, _dot_tn(gp * y_pool + gr * y_lru, dxn_t), i == 0)
        dgl_ref[:, :D] = (dmerged * y_pool * (gp * (1.0 - gp))).astype(MM)
        dgl_ref[:, D:] = (dmerged * y_lru * (gr * (1.0 - gr))).astype(MM)
        dyp = (dmerged * gp).astype(MM)
        dyl = (dmerged * gr).astype(MM)
        dpm = None
        for k in range(N_CHIPS):
            dyp_k = dyp[:, k * cs:(k + 1) * cs]
            part = _dot_nt(dyp_k, wpu_ref[k])
            dpm = part if dpm is None else dpm + part
            _accumulate(acc_pu.at[k], _dot_tn(pm, dyp_k), i == 0)
        dpm_ref[...] = dpm
        dzl_ref[...] = _dot_nt(dyl, wlu_ref[...])
        _accumulate(acc_lu, _dot_tn(zl, dyl), i == 0)

        @pl.when(i == ni - 1)
        def _():
            copies = [pltpu.make_async_copy(acc_o, dwo_hbm, sem.at[0]), pltpu.make_async_copy(acc_pu, dwpu_hbm, sem.at[1]),
                      pltpu.make_async_copy(acc_lu, dwlu_hbm, sem.at[2])]
            for cp in copies:
                cp.start()
            for cp in copies:
                cp.wait()

    return pl.pallas_call(
        body, name=name, grid=(ni,),
        in_specs=[pl.BlockSpec((tm, D), lambda i: (i, 0))] + _mix_out_specs(tm, D, pw, lw, cs),
        out_specs=[pl.BlockSpec((tm, 2 * D), lambda i: (i, 0)), pl.BlockSpec((tm, pw), lambda i: (i, 0)),
                   pl.BlockSpec((tm, lw), lambda i: (i, 0)), ANY, ANY, ANY],
        out_shape=[jax.ShapeDtypeStruct((T, 2 * D), MM), jax.ShapeDtypeStruct((T, pw), F32),
                   jax.ShapeDtypeStruct((T, lw), F32), jax.ShapeDtypeStruct((D, D), F32),
                   jax.ShapeDtypeStruct((N_CHIPS, pw, cs), F32), jax.ShapeDtypeStruct((lw, D), F32)],
        scratch_shapes=[pltpu.VMEM((D, D), F32), pltpu.VMEM((N_CHIPS, pw, cs), F32), pltpu.VMEM((lw, D), F32),
                        pltpu.SemaphoreType.DMA((3,))],
        compiler_params=_params(("arbitrary",)),
    )(dxn, proj, proj, pm, zl, wpu, wlu, wo)


def _shifted(buf, val, shift, S):
    buf[pl.ds(SHIFT_PAD, S), :] = val
    return buf[pl.ds(SHIFT_PAD - shift, S), :]


def _zero_pads(buf, S):
    z = jnp.zeros((SHIFT_PAD, buf.shape[1]), F32)
    buf[pl.ds(0, SHIFT_PAD), :] = z
    buf[pl.ds(SHIFT_PAD + S, SHIFT_PAD), :] = z


def _window_sum(buf, val, window, S, lead=False):
    acc, width = val, 1
    while width < window:
        acc = acc + _shifted(buf, acc, -width if lead else width, S)
        width *= 2
    return acc


def _pool_count(S, window):
    t = lax.broadcasted_iota(jnp.int32, (S, LANES), 0)
    return jnp.minimum(t + 1, window).astype(F32)


def _pool_fwd_group(u, g, window, pw_ref, pb_ref, buf, S):
    pooled = _window_sum(buf, u, window, S) / _pool_count(S, window) - u
    return pooled, _dot(pooled, pw_ref[g]) + pb_ref[g]


def _pool_fwd(proj, pool_w, pool_b, pool_scale, name):
    S = proj.shape[0]
    G = pool_w.shape[0]
    pw = G * LANES

    def body(u_ref, pw_ref, pb_ref, ps_ref, pm_ref, buf):
        _zero_pads(buf, S)
        for g in range(G):
            cols = slice(g * LANES, (g + 1) * LANES)
            _, mixed = _pool_fwd_group(u_ref[:, cols], g, POOL_WINDOWS[g], pw_ref, pb_ref, buf, S)
            pm_ref[:, cols] = (mixed * ps_ref[:, cols]).astype(MM)

    return pl.pallas_call(
        body, name=name, grid=(1,),
        in_specs=[pl.BlockSpec((S, pw), lambda i: (0, 0)), pl.BlockSpec((G, LANES, LANES), lambda i: (0, 0, 0)),
                  pl.BlockSpec((G, 1, LANES), lambda i: (0, 0, 0)), pl.BlockSpec((1, pw), lambda i: (0, 0))],
        out_specs=pl.BlockSpec((S, pw), lambda i: (0, 0)),
        out_shape=jax.ShapeDtypeStruct((S, pw), MM),
        scratch_shapes=[pltpu.VMEM((S + 2 * SHIFT_PAD, LANES), F32)],
        compiler_params=_params(("arbitrary",)),
    )(proj, pool_w, pool_b, pool_scale)


def _pool_bwd(proj, dpm, pool_w, pool_b, pool_scale, name):
    S = proj.shape[0]
    G = pool_w.shape[0]
    pw = G * LANES

    def body(u_ref, dpm_ref, pw_ref, pb_ref, ps_ref, du_ref, dpw_ref, dpb_ref, dps_ref, buf):
        _zero_pads(buf, S)
        for g in range(G):
            cols = slice(g * LANES, (g + 1) * LANES)
            window = POOL_WINDOWS[g]
            pooled, mixed = _pool_fwd_group(u_ref[:, cols], g, window, pw_ref, pb_ref, buf, S)
            d_out = dpm_ref[:, cols]
            dmixed = d_out * ps_ref[:, cols]
            dps_ref[:, cols] = jnp.sum(d_out * mixed, axis=0, keepdims=True)
            dpb_ref[g] = jnp.sum(dmixed, axis=0, keepdims=True)
            dpw_ref[g] = _dot_tn(pooled, dmixed)
            dpooled = _dot_nt(dmixed, pw_ref[g])
            q = dpooled / _pool_count(S, window)
            du_ref[:, cols] = (_window_sum(buf, q, window, S, lead=True) - dpooled).astype(MM)

    return pl.pallas_call(
        body, name=name, grid=(1,),
        in_specs=[pl.BlockSpec((S, pw), lambda i: (0, 0)), pl.BlockSpec((S, pw), lambda i: (0, 0)),
                  pl.BlockSpec((G, LANES, LANES), lambda i: (0, 0, 0)),
                  pl.BlockSpec((G, 1, LANES), lambda i: (0, 0, 0)), pl.BlockSpec((1, pw), lambda i: (0, 0))],
        out_specs=[pl.BlockSpec((S, pw), lambda i: (0, 0)), pl.BlockSpec((G, LANES, LANES), lambda i: (0, 0, 0)),
                   pl.BlockSpec((G, 1, LANES), lambda i: (0, 0, 0)), pl.BlockSpec((1, pw), lambda i: (0, 0))],
        out_shape=[jax.ShapeDtypeStruct((S, pw), MM), jax.ShapeDtypeStruct((G, LANES, LANES), F32),
                   jax.ShapeDtypeStruct((G, 1, LANES), F32), jax.ShapeDtypeStruct((1, pw), F32)],
        scratch_shapes=[pltpu.VMEM((S + 2 * SHIFT_PAD, LANES), F32)],
        compiler_params=_params(("arbitrary",)),
    )(proj, dpm, pool_w, pool_b, pool_scale)


def _scan(a, b, bufs, S, reverse=False):
    pad = S // 2
    seq = pl.ds(pad, S)
    A, B = bufs[:2], bufs[2:]
    A[0][seq, :] = a
    B[0][seq, :] = b
    cur, d = 0, 1
    while d < S:
        sh = pl.ds(pad + d if reverse else pad - d, S)
        a_c = A[cur][seq, :]
        B[1 - cur][seq, :] = a_c * B[cur][sh, :] + B[cur][seq, :]
        if 2 * d < S:
            A[1 - cur][seq, :] = a_c * A[cur][sh, :]
        cur, d = 1 - cur, 2 * d
    return B[cur][seq, :]


def _init_scan_pads(bufs, S):
    pad = S // 2
    for n, buf in enumerate(bufs):
        fill = jnp.full((pad, LANES), 1.0 if n < 2 else 0.0, F32)
        buf[pl.ds(0, pad), :] = fill
        buf[pl.ds(pad + S, pad), :] = fill


def _gelu_and_grad(x):
    c = 0.7978845608028654
    x2 = x * x
    th = jnp.tanh(c * (x + 0.044715 * (x2 * x)))
    gelu = 0.5 * x * (1.0 + th)
    grad = 0.5 * (1.0 + th) + 0.5 * x * (1.0 - th * th) * (c * (1.0 + 3 * 0.044715 * x2))
    return gelu, grad


def _lru_head_fwd(ul, cw, cb, wa, ba, wx, bx, lam, sbuf, bufs, S):
    sbuf[pl.ds(SHIFT_PAD, S), :] = ul
    v = None
    for k in range(CONV_WIDTH):
        term = sbuf[pl.ds(SHIFT_PAD - (CONV_WIDTH - 1 - k), S), :] * cw[k:k + 1, :]
        v = term if v is None else v + term
    v = v + cb
    r = jax.nn.sigmoid(_dot(v, wa) + ba)
    ig = jax.nn.sigmoid(_dot(v, wx) + bx)
    sp = jax.nn.softplus(-lam)
    log_a = -LRU_C * r * sp
    a = jnp.exp(log_a)
    mult = jnp.sqrt(-jnp.tanh(log_a) * (1.0 + a * a))
    h = _scan(a, mult * (ig * v), bufs, S)
    return v, r, ig, sp, a, mult, h


def _lru_specs(S, H, lw, pw):
    b0 = pw // LANES
    return [pl.BlockSpec((S, LANES), lambda h: (0, b0 + h)), pl.BlockSpec((S, LANES), lambda h: (0, b0 + H + h)),
            pl.BlockSpec((CONV_WIDTH, LANES), lambda h: (0, h)), pl.BlockSpec((1, LANES), lambda h: (0, h)),
            pl.BlockSpec((None, LANES, LANES), lambda h: (h, 0, 0)), pl.BlockSpec((None, 1, LANES), lambda h: (h, 0, 0)),
            pl.BlockSpec((None, LANES, LANES), lambda h: (h, 0, 0)), pl.BlockSpec((None, 1, LANES), lambda h: (h, 0, 0)),
            pl.BlockSpec((1, LANES), lambda h: (0, h))]


def _lru_scratch(S):
    return [pltpu.VMEM((S + 2 * SHIFT_PAD, LANES), F32)] + [pltpu.VMEM((2 * S, LANES), F32)] * 4


def _lru_fwd(proj, conv_w, conv_b, wa, ba, wx, bx, lam, pw, name):
    S = proj.shape[0]
    H = wa.shape[0]
    lw = H * LANES

    def body(ul_ref, ug_ref, cw_ref, cb_ref, wa_ref, ba_ref, wx_ref, bx_ref, lam_ref, zl_ref, sbuf, *bufs):
        @pl.when(pl.program_id(0) == 0)
        def _():
            _zero_pads(sbuf, S)
            _init_scan_pads(bufs, S)

        h = _lru_head_fwd(ul_ref[...], cw_ref[...], cb_ref[...], wa_ref[...], ba_ref[...], wx_ref[...], bx_ref[...],
                          lam_ref[...], sbuf, bufs, S)[-1]
        zl_ref[...] = (h * jax.nn.gelu(ug_ref[...])).astype(MM)

    return pl.pallas_call(
        body, name=name, grid=(H,),
        in_specs=_lru_specs(S, H, lw, pw),
        out_specs=pl.BlockSpec((S, LANES), lambda h: (0, h)),
        out_shape=jax.ShapeDtypeStruct((S, lw), MM),
        scratch_shapes=_lru_scratch(S),
        compiler_params=_params(("arbitrary",)),
    )(proj, proj, conv_w, conv_b, wa, ba, wx, bx, lam)


def _lru_bwd(proj, dzl, conv_w, conv_b, wa, ba, wx, bx, lam, pw, name):
    S = proj.shape[0]
    H = wa.shape[0]
    lw = H * LANES

    def body(ul_ref, ug_ref, cw_ref, cb_ref, wa_ref, ba_ref, wx_ref, bx_ref, lam_ref, dzl_ref,
             dul_ref, dug_ref, dcw_ref, dcb_ref, dwa_ref, dba_ref, dwx_ref, dbx_ref, dlam_ref, sbuf, *bufs):
        @pl.when(pl.program_id(0) == 0)
        def _():
            _zero_pads(sbuf, S)
            _init_scan_pads(bufs, S)

        ul, cw, lam = ul_ref[...], cw_ref[...], lam_ref[...]
        wa, wx = wa_ref[...], wx_ref[...]
        v, r, ig, sp, a, mult, h = _lru_head_fwd(ul, cw, cb_ref[...], wa, ba_ref[...], wx, bx_ref[...], lam, sbuf, bufs, S)
        gelu, gelu_grad = _gelu_and_grad(ug_ref[...])
        dzl_t = dzl_ref[...]
        dug_ref[...] = (dzl_t * h * gelu_grad).astype(MM)
        a_next = _shifted(sbuf, a, -1, S)
        lam_t = _scan(a_next, dzl_t * gelu, bufs, S, reverse=True)
        da = lam_t * _shifted(sbuf, h, 1, S)
        d_iv = lam_t * mult
        d_log_a = da * a - (lam_t * (ig * v)) * (a * a) / mult
        dlam_ref[...] = jnp.sum(d_log_a * r, axis=0, keepdims=True) * (LRU_C * jax.nn.sigmoid(-lam))
        dra = (d_log_a * (-LRU_C * sp)) * (r * (1.0 - r))
        dia = (d_iv * v) * (ig * (1.0 - ig))
        dba_ref[...] = jnp.sum(dra, axis=0, keepdims=True)
        dbx_ref[...] = jnp.sum(dia, axis=0, keepdims=True)
        dwa_ref[...] = _dot_tn(v, dra)
        dwx_ref[...] = _dot_tn(v, dia)
        dv = d_iv * ig + _dot_nt(dra, wa) + _dot_nt(dia, wx)
        dcb_ref[...] = jnp.sum(dv, axis=0, keepdims=True)
        sbuf[pl.ds(SHIFT_PAD, S), :] = ul
        for k in range(CONV_WIDTH):
            dcw_ref[k:k + 1, :] = jnp.sum(dv * sbuf[pl.ds(SHIFT_PAD - (CONV_WIDTH - 1 - k), S), :], axis=0, keepdims=True)
        sbuf[pl.ds(SHIFT_PAD, S), :] = dv
        dul = None
        for k in range(CONV_WIDTH):
            term = sbuf[pl.ds(SHIFT_PAD + (CONV_WIDTH - 1 - k), S), :] * cw[k:k + 1, :]
            dul = term if dul is None else dul + term
        dul_ref[...] = dul.astype(MM)

    head_mat = pl.BlockSpec((None, LANES, LANES), lambda h: (h, 0, 0))
    head_vec = pl.BlockSpec((None, 1, LANES), lambda h: (h, 0, 0))
    col = pl.BlockSpec((S, LANES), lambda h: (0, h))
    row = pl.BlockSpec((1, LANES), lambda h: (0, h))
    return pl.pallas_call(
        body, name=name, grid=(H,),
        in_specs=_lru_specs(S, H, lw, pw) + [col],
        out_specs=[col, col, pl.BlockSpec((CONV_WIDTH, LANES), lambda h: (0, h)), row, head_mat, head_vec, head_mat,
                   head_vec, row],
        out_shape=[jax.ShapeDtypeStruct((S, lw), MM), jax.ShapeDtypeStruct((S, lw), MM),
                   jax.ShapeDtypeStruct((CONV_WIDTH, lw), F32), jax.ShapeDtypeStruct((1, lw), F32),
                   jax.ShapeDtypeStruct((H, LANES, LANES), F32), jax.ShapeDtypeStruct((H, 1, LANES), F32),
                   jax.ShapeDtypeStruct((H, LANES, LANES), F32), jax.ShapeDtypeStruct((H, 1, LANES), F32),
                   jax.ShapeDtypeStruct((1, lw), F32)],
        scratch_shapes=_lru_scratch(S),
        compiler_params=_params(("arbitrary",)),
    )(proj, proj, conv_w, conv_b, wa, ba, wx, bx, lam, dzl)


def _loss_head(x, gain, target, name):
    T, D = x.shape
    tm = min(TOKEN_TILE, T)
    ni = T // tm

    def body(x_ref, g_ref, t_ref, loss_ref, dx_ref, dg_ref, loss_acc, dg_acc):
        i = pl.program_id(0)
        g = g_ref[...]
        r, xh, y = _rms(x_ref[...], g)
        err = y - t_ref[...]
        part = 0.5 * jnp.sum(jnp.mean(err * err, axis=-1, keepdims=True), axis=0, keepdims=True)
        _accumulate(loss_acc, jnp.broadcast_to(part, (1, LANES)), i == 0)
        dy = err * (1.0 / D)
        _accumulate(dg_acc, jnp.sum(dy * xh, axis=0, keepdims=True), i == 0)
        dx_ref[...] = _rms_bwd(dy, xh, r, g)

        @pl.when(i == ni - 1)
        def _():
            loss_ref[...] = loss_acc[...]
            dg_ref[...] = dg_acc[...]

    return pl.pallas_call(
        body, name=name, grid=(ni,),
        in_specs=[pl.BlockSpec((tm, D), lambda i: (i, 0)), pl.BlockSpec((1, D), lambda i: (0, 0)),
                  pl.BlockSpec((tm, D), lambda i: (i, 0))],
        out_specs=[pl.BlockSpec((1, LANES), lambda i: (0, 0)), pl.BlockSpec((tm, D), lambda i: (i, 0)),
                   pl.BlockSpec((1, D), lambda i: (0, 0))],
        out_shape=[jax.ShapeDtypeStruct((1, LANES), F32), jax.ShapeDtypeStruct((T, D), F32),
                   jax.ShapeDtypeStruct((1, D), F32)],
        scratch_shapes=[pltpu.VMEM((1, LANES), F32), pltpu.VMEM((1, D), F32)],
        compiler_params=_params(("arbitrary",)),
    )(x, gain, target)


def _pair_sum(a, recv, c, name):
    n, R, C = a.shape
    hr = R // 2

    def body(c_ref, a_ref, r_ref, pf_ref, pb_ref):
        s = a_ref[...] + r_ref[...]
        pf_ref[...] = s
        pb_ref[...] = s.astype(XFER)

    piece = pl.BlockSpec((None, hr, C), lambda k, c_ref: (k, 0, 0))
    return pl.pallas_call(
        body, name=name,
        grid_spec=pltpu.PrefetchScalarGridSpec(
            num_scalar_prefetch=1, grid=(n,),
            in_specs=[pl.BlockSpec((None, hr, C), lambda k, c_ref: (k, c_ref[0], 0)), piece],
            out_specs=[piece, piece]),
        out_shape=[jax.ShapeDtypeStruct((n, hr, C), F32), jax.ShapeDtypeStruct((n, hr, C), XFER)],
        compiler_params=_params(("arbitrary",)),
    )(c, a, recv)


def _chip_sum(place, pf, others, full, layer, name):
    _, hr, C = pf.shape
    rb = _row_block(hr, C)
    nb = hr // rb

    def body(place_ref, o_ref, q_ref, full_ref, g_ref):
        g_ref[...] = ((o_ref[...] + q_ref[0].astype(F32)) + q_ref[1].astype(F32)) + q_ref[2].astype(F32)

    return pl.pallas_call(
        body, name=name,
        grid_spec=pltpu.PrefetchScalarGridSpec(
            num_scalar_prefetch=1, grid=(nb,),
            in_specs=[pl.BlockSpec((None, rb, C), lambda i, p: (p[0], i, 0)),
                      pl.BlockSpec((N_CHIPS - 1, rb, C), lambda i, p: (0, i, 0)), ANY],
            out_specs=pl.BlockSpec((None, rb, C), lambda i, p: (layer, p[1] * nb + i, 0))),
        out_shape=jax.ShapeDtypeStruct(full.shape, F32),
        input_output_aliases={3: 0},
        compiler_params=_params(("arbitrary",)),
    )(place, pf, others, full)


def _adamw(w, g, m, v, name):
    L, R, C = w.shape
    rb = _row_block(R, C)

    def body(w_ref, g_ref, m_ref, v_ref, go_ref, d_ref, mo_ref, vo_ref):
        g_t = g_ref[...]
        m_t = ADAM_B1 * m_ref[...] + (1.0 - ADAM_B1) * g_t
        v_t = ADAM_B2 * v_ref[...] + (1.0 - ADAM_B2) * (g_t * g_t)
        m_hat = m_t / (1.0 - ADAM_B1 ** ADAM_STEP)
        v_hat = v_t / (1.0 - ADAM_B2 ** ADAM_STEP)
        go_ref[...] = g_t
        d_ref[...] = -ADAM_LR * (m_hat / (jnp.sqrt(v_hat) + ADAM_EPS) + ADAM_WD * w_ref[...])
        mo_ref[...] = m_t
        vo_ref[...] = v_t

    blk = pl.BlockSpec((None, rb, C), lambda l, i: (l, i, 0))
    return pl.pallas_call(
        body, name=name, grid=(L, R // rb), in_specs=[blk] * 4, out_specs=[blk] * 4,
        out_shape=[jax.ShapeDtypeStruct((L, R, C), F32)] * 4,
        compiler_params=_params(("arbitrary", "arbitrary")),
    )(w, g, m, v)


def _place():
    x, y, c = lax.axis_index("x"), lax.axis_index("y"), lax.axis_index("c")
    others = [(1 - x, y), (x, 1 - y), (1 - x, 1 - y)]
    return x, y, c, 2 * x + y, others


def _half(c, rows):
    return pl.ds(pl.multiple_of(c * (rows // 2), 16), rows // 2)


def _gather_weights(shards, layer, small, name):
    n = len(shards)
    n_out = n + (small is not None)

    def body(*refs):
        ins, outs = refs[:n_out], refs[n_out:2 * n_out]
        send, recv, fsend, frecv, osend, orecv, local = refs[2 * n_out:]
        x, y, c, k, others = _place()
        sib = (x, y, 1 - c)

        def copy(a, r, to, src_slot, rows, sems):
            whole = rows is None
            src = ins[a].at[layer] if src_slot is None else outs[a].at[src_slot]
            dst = outs[a].at[k if src_slot is None else src_slot]
            if not whole:
                src, dst = src.at[rows], dst.at[rows]
            return pltpu.make_async_remote_copy(src_ref=src, dst_ref=dst, send_sem=sems[0].at[a * 3 + r],
                                                recv_sem=sems[1].at[a * 3 + r], device_id=to, device_id_type=MESH)

        mine = [pltpu.make_async_remote_copy(src_ref=ins[a].at[layer], dst_ref=outs[a].at[k], send_sem=osend.at[a],
                                             recv_sem=orecv.at[a], device_id=sib, device_id_type=MESH) for a in range(n)]
        if small is not None:
            mine.append(pltpu.make_async_copy(ins[n], outs[n].at[k], local))
        for cp in mine:
            cp.start()
        first, passed = [], []
        for a in range(n_out):
            is_small = a >= n
            R = ins[a].shape[-2]
            rows = None if is_small else _half(c, R)
            for r, (px, py) in enumerate(others):
                if is_small:
                    cp = pltpu.make_async_remote_copy(src_ref=ins[a], dst_ref=outs[a].at[k], send_sem=send.at[a * 3 + r],
                                                      recv_sem=recv.at[a * 3 + r], device_id=(px, py, c), device_id_type=MESH)
                else:
                    cp = copy(a, r, (px, py, c), None, rows, (send, recv))
                cp.start()
                first.append(cp)
        for a in range(n):
            R = ins[a].shape[-2]
            for r, (px, py) in enumerate(others):
                pk = 2 * px + py
                copy(a, r, (px, py, c), pk, _half(c, R), (send, recv)).wait_recv()
                fw = copy(a, r, sib, pk, _half(c, R), (fsend, frecv))
                fw.start()
                passed.append(fw)
        if small is not None:
            for r, (px, py) in enumerate(others):
                pltpu.make_async_remote_copy(src_ref=ins[n], dst_ref=outs[n].at[2 * px + py], send_sem=send.at[n * 3 + r],
                                             recv_sem=recv.at[n * 3 + r], device_id=(px, py, c),
                                             device_id_type=MESH).wait_recv()
        for a in range(n):
            R = ins[a].shape[-2]
            for r, (px, py) in enumerate(others):
                copy(a, r, sib, 2 * px + py, _half(1 - c, R), (fsend, frecv)).wait_recv()
        for cp in first + passed:
            cp.wait_send()
        for cp in mine:
            cp.wait()

    operands = list(shards) + ([small] if small is not None else [])
    out_shape = [jax.ShapeDtypeStruct((N_CHIPS,) + s.shape[1:], s.dtype) for s in shards]
    if small is not None:
        out_shape.append(jax.ShapeDtypeStruct((N_CHIPS,) + small.shape, small.dtype))
    return pl.pallas_call(
        body, name=name, in_specs=[ANY] * n_out, out_specs=[ANY] * n_out, out_shape=out_shape,
        scratch_shapes=[pltpu.SemaphoreType.DMA((3 * n_out,))] * 4 + [pltpu.SemaphoreType.DMA((n,))] * 2
        + [pltpu.SemaphoreType.DMA],
    )(*operands)


def _exchange_halves(grads, name):
    n = len(grads)

    def body(*refs):
        ins, outs, send, recv = refs[:n], refs[n:2 * n], refs[2 * n], refs[2 * n + 1]
        x, y, c, _, _ = _place()
        copies = []
        for a in range(n):
            R = ins[a].shape[1]
            cp = pltpu.make_async_remote_copy(src_ref=ins[a].at[:, _half(1 - c, R), :], dst_ref=outs[a],
                                              send_sem=send.at[a], recv_sem=recv.at[a], device_id=(x, y, 1 - c),
                                              device_id_type=MESH)
            cp.start()
            copies.append(cp)
        for cp in copies:
            cp.wait()

    return pl.pallas_call(
        body, name=name, in_specs=[ANY] * n, out_specs=[ANY] * n,
        out_shape=[jax.ShapeDtypeStruct((g.shape[0], g.shape[1] // 2, g.shape[2]), g.dtype) for g in grads],
        scratch_shapes=[pltpu.SemaphoreType.DMA((n,))] * 2,
    )(*grads)


def _exchange_chips(pb, name):
    n = len(pb)

    def body(*refs):
        pbs, gots = refs[:n], refs[n:2 * n]
        send, recv = refs[2 * n:]
        x, y, c, k, others = _place()
        copies = []
        for a in range(n):
            for r, (px, py) in enumerate(others):
                cp = pltpu.make_async_remote_copy(src_ref=pbs[a].at[2 * px + py], dst_ref=gots[a].at[r],
                                                  send_sem=send.at[a * 3 + r], recv_sem=recv.at[a * 3 + r],
                                                  device_id=(px, py, c), device_id_type=MESH)
                cp.start()
                copies.append(cp)
        for cp in copies:
            cp.wait()

    return pl.pallas_call(
        body, name=name, in_specs=[ANY] * n, out_specs=[ANY] * n,
        out_shape=[jax.ShapeDtypeStruct((N_CHIPS - 1,) + p.shape[1:], XFER) for p in pb],
        scratch_shapes=[pltpu.SemaphoreType.DMA((3 * n,))] * 2,
    )(*pb)


def _join_halves(fulls, layer, name):
    n = len(fulls)

    def body(*refs):
        outs = refs[n:2 * n]
        send, recv = refs[2 * n:]
        x, y, c, _, _ = _place()

        def copy(a, half):
            rows = outs[a].at[layer, _half(half, outs[a].shape[1]), :]
            return pltpu.make_async_remote_copy(src_ref=rows, dst_ref=rows, send_sem=send.at[a], recv_sem=recv.at[a],
                                                device_id=(x, y, 1 - c), device_id_type=MESH)

        for a in range(n):
            copy(a, c).start()
        for a in range(n):
            copy(a, 1 - c).wait_recv()
        for a in range(n):
            copy(a, c).wait_send()

    return pl.pallas_call(
        body, name=name, in_specs=[ANY] * n, out_specs=[ANY] * n,
        out_shape=[jax.ShapeDtypeStruct(f.shape, f.dtype) for f in fulls],
        input_output_aliases={a: a for a in range(n)},
        scratch_shapes=[pltpu.SemaphoreType.DMA((n,))] * 2,
    )(*fulls)


def _all_reduce_small(packed, name):
    _, rows, lanes = packed.shape

    def body(in_ref, out_ref, stage, send1, recv1, send2, recv2):
        x, y, c, _, _ = _place()
        me = 4 * x + 2 * y + c
        ids = [(p // 4, (p // 2) % 2, p % 2) for p in range(N_DEV)]

        def scatter(p):
            return pltpu.make_async_remote_copy(src_ref=in_ref.at[p], dst_ref=stage.at[me], send_sem=send1.at[p],
                                                recv_sem=recv1.at[me], device_id=ids[p], device_id_type=MESH)

        def spread(p, to):
            return pltpu.make_async_remote_copy(src_ref=out_ref.at[p], dst_ref=out_ref.at[p], send_sem=send2.at[to],
                                                recv_sem=recv2.at[p], device_id=ids[to], device_id_type=MESH)

        for p in range(N_DEV):
            @pl.when(me != p)
            def _(p=p):
                scatter(p).start()

        total = None
        for p in range(N_DEV):
            @pl.when(me != p)
            def _(p=p):
                pltpu.make_async_remote_copy(src_ref=in_ref.at[0], dst_ref=stage.at[p], send_sem=send1.at[p],
                                             recv_sem=recv1.at[p], device_id=ids[p], device_id_type=MESH).wait_recv()

            @pl.when(me == p)
            def _(p=p):
                stage[p] = in_ref[me]

        for p in range(N_DEV):
            total = stage[p] if total is None else total + stage[p]
        out_ref[me] = total
        for p in range(N_DEV):
            @pl.when(me != p)
            def _(p=p):
                scatter(p).wait_send()
                spread(me, p).start()
        for p in range(N_DEV):
            @pl.when(me != p)
            def _(p=p):
                spread(p, p).wait_recv()
        for p in range(N_DEV):
            @pl.when(me != p)
            def _(p=p):
                spread(me, p).wait_send()

    vmem = pl.BlockSpec(memory_space=pltpu.VMEM)
    return pl.pallas_call(
        body, name=name, in_specs=[vmem], out_specs=vmem,
        out_shape=jax.ShapeDtypeStruct(packed.shape, F32),
        scratch_shapes=[pltpu.VMEM(packed.shape, F32)] + [pltpu.SemaphoreType.DMA((N_DEV,))] * 4,
        compiler_params=pltpu.CompilerParams(vmem_limit_bytes=VMEM_LIMIT),
    )(packed)


BIG = ("ffn1_w_up", "ffn1_w_down", "w_in", "w_pool_up", "w_lru_up", "w_out", "ffn2_w_up", "ffn2_w_down")
SMALL = ("norm_ffn1", "norm_mix", "pool_w", "pool_b", "pool_scale", "conv_w", "conv_b", "lru_w_a", "lru_b_a", "lru_w_x",
         "lru_b_x", "lru_lambda", "norm_ffn2", "final_norm")
WEIGHTS = ("norm_ffn1", "ffn1_w_up", "ffn1_w_down", "norm_mix", "w_in", "pool_w", "pool_b", "pool_scale", "w_pool_up",
           "conv_w", "conv_b", "lru_w_a", "lru_b_a", "lru_w_x", "lru_b_x", "lru_lambda", "w_lru_up", "w_out", "norm_ffn2",
           "ffn2_w_up", "ffn2_w_down", "final_norm")


def _pack(arrays, rows_multiple):
    flat = jnp.concatenate([a.reshape(-1) for a in arrays])
    rows = -(-flat.shape[0] // LANES)
    rows = -(-rows // rows_multiple) * rows_multiple
    return jnp.pad(flat, (0, rows * LANES - flat.shape[0])).reshape(rows, LANES)


def _unpack(packed, like):
    flat, out, at = packed.reshape(-1), [], 0
    for a in like:
        out.append(flat[at:at + a.size].reshape(a.shape))
        at += a.size
    return out


def kernel(x, norm_ffn1, ffn1_w_up, ffn1_w_down, norm_mix, w_in, pool_w, pool_b, pool_scale, w_pool_up, conv_w, conv_b, lru_w_a, lru_b_a, lru_w_x, lru_b_x, lru_lambda, w_lru_up, w_out, norm_ffn2, ffn2_w_up, ffn2_w_down, final_norm, loss_target, m_norm_ffn1, m_ffn1_w_up, m_ffn1_w_down, m_norm_mix, m_w_in, m_pool_w, m_pool_b, m_pool_scale, m_w_pool_up, m_conv_w, m_conv_b, m_lru_w_a, m_lru_b_a, m_lru_w_x, m_lru_b_x, m_lru_lambda, m_w_lru_up, m_w_out, m_norm_ffn2, m_ffn2_w_up, m_ffn2_w_down, m_final_norm, v_norm_ffn1, v_ffn1_w_up, v_ffn1_w_down, v_norm_mix, v_w_in, v_pool_w, v_pool_b, v_pool_scale, v_w_pool_up, v_conv_w, v_conv_b, v_lru_w_a, v_lru_b_a, v_lru_w_x, v_lru_b_x, v_lru_lambda, v_w_lru_up, v_w_out, v_norm_ffn2, v_ffn2_w_up, v_ffn2_w_down, v_final_norm):
    given = dict(locals())
    W = {n: given[n] for n in WEIGHTS}
    M = {n: given["m_" + n] for n in WEIGHTS}
    V = {n: given["v_" + n] for n in WEIGHTS}
    L = norm_ffn1.shape[0]
    T, D = x.shape[1], x.shape[2]
    G, H = pool_w.shape[1], lru_w_a.shape[1]
    pw, lw = G * LANES, H * LANES
    chip = 2 * lax.axis_index("x") + lax.axis_index("y")
    core = lax.axis_index("c").astype(jnp.int32).reshape(1)
    place = jnp.stack([chip, lax.axis_index("c")]).astype(jnp.int32)

    shards = [W[n].astype(MM) for n in BIG]
    conv_shard = conv_w.reshape(L * CONV_WIDTH, conv_w.shape[2])

    xs = x.reshape(T, D)
    saved = []
    conv_full = None
    for l in range(L):
        got = _gather_weights(shards, l, conv_shard if l == 0 else None, f"gather_weights_{l}")
        if l == 0:
            conv_full = got[-1].reshape(N_CHIPS, L, CONV_WIDTH, -1).transpose(1, 2, 0, 3).reshape(L, CONV_WIDTH, lw)
        w_up1, w_dn1, w_i, w_pu, w_lu, w_o, w_up2, w_dn2 = got[:8]
        w_dn1, w_dn2 = w_dn1.reshape(-1, D), w_dn2.reshape(-1, D)
        w_lu, w_o = w_lu.reshape(-1, D), w_o.reshape(-1, D)
        small = dict(pool_w=pool_w[l], pool_b=pool_b[l].reshape(G, 1, LANES), pool_scale=pool_scale[l].reshape(1, pw),
                     conv_w=conv_full[l], conv_b=conv_b[l].reshape(1, lw), wa=lru_w_a[l],
                     ba=lru_b_a[l].reshape(H, 1, LANES), wx=lru_w_x[l], bx=lru_b_x[l].reshape(H, 1, LANES),
                     lam=lru_lambda[l].reshape(1, lw))
        g1, g2, g3 = norm_ffn1[l].reshape(1, D), norm_mix[l].reshape(1, D), norm_ffn2[l].reshape(1, D)

        x0 = xs
        u1 = _norm_matmul_fwd(x0, g1, w_up1, f"ffn1_up_{l}")
        x1 = _swiglu_down_fwd(u1, x0, w_dn1, f"ffn1_down_{l}")
        proj = _norm_matmul_fwd(x1, g2, w_i, f"mix_in_{l}")
        pm = _pool_fwd(proj, small["pool_w"], small["pool_b"], small["pool_scale"], f"pool_{l}")
        zl = _lru_fwd(proj, small["conv_w"], small["conv_b"], small["wa"], small["ba"], small["wx"], small["bx"],
                      small["lam"], pw, f"lru_{l}")
        x2 = _mix_out_fwd(x1, proj, pm, zl, w_pu, w_lu, w_o, f"mix_out_{l}")
        u2 = _norm_matmul_fwd(x2, g3, w_up2, f"ffn2_up_{l}")
        xs = _swiglu_down_fwd(u2, x2, w_dn2, f"ffn2_down_{l}")
        saved.append(dict(x0=x0, u1=u1, x1=x1, proj=proj, pm=pm, zl=zl, x2=x2, u2=u2, small=small, g=(g1, g2, g3),
                          w=(w_up1, w_dn1, w_i, w_pu, w_lu, w_o, w_up2, w_dn2)))

    loss_part, dx, d_final = _loss_head(xs, final_norm.reshape(1, D), loss_target.reshape(T, D), "loss_head")
    loss = lax.psum(loss_part[0, 0], ("x", "y", "c"))

    full = [jnp.zeros(W[n].shape, F32) for n in BIG]
    small_grads = [None] * L
    du_spec_ffn = lambda tm, tn: pl.BlockSpec((None, tm, tn), lambda k, i: (k // 2, i, k % 2))
    du_spec_mix = lambda tm, tn: pl.BlockSpec((tm, tn), lambda k, i: (i, k))
    for l in reversed(range(L)):
        s = saved[l]
        w_up1, w_dn1, w_i, w_pu, w_lu, w_o, w_up2, w_dn2 = s["w"]
        g1, g2, g3 = s["g"]
        sm = s["small"]
        du2, d_dn2 = _swiglu_down_bwd(s["u2"], dx, w_dn2, f"ffn2_down_bwd_{l}")
        dx, d_up2, dg3 = _norm_matmul_bwd(s["x2"], g3, du2, du_spec_ffn, w_up2, dx, f"ffn2_up_bwd_{l}")
        dgl, dpm, dzl, d_o, d_pu, d_lu = _mix_out_bwd(dx, s["proj"], s["pm"], s["zl"], w_pu, w_lu, w_o, f"mix_out_bwd_{l}")
        du_pool, d_pool_w, d_pool_b, d_pool_scale = _pool_bwd(s["proj"], dpm, sm["pool_w"], sm["pool_b"], sm["pool_scale"],
                                                            f"pool_bwd_{l}")
        dul, dug, d_cw, d_cb, d_wa, d_ba, d_wx, d_bx, d_lam = _lru_bwd(
            s["proj"], dzl, sm["conv_w"], sm["conv_b"], sm["wa"], sm["ba"], sm["wx"], sm["bx"], sm["lam"], pw, f"lru_bwd_{l}")
        dproj = jnp.concatenate([du_pool, dul, dug, dgl], axis=1)
        dx, d_in, dg2 = _norm_matmul_bwd(s["x1"], g2, dproj, du_spec_mix, w_i, dx, f"mix_in_bwd_{l}")
        du1, d_dn1 = _swiglu_down_bwd(s["u1"], dx, w_dn1, f"ffn1_down_bwd_{l}")
        dx, d_up1, dg1 = _norm_matmul_bwd(s["x0"], g1, du1, du_spec_ffn, w_up1, dx, f"ffn1_up_bwd_{l}")
        small_grads[l] = dict(norm_ffn1=dg1, norm_mix=dg2, pool_w=d_pool_w, pool_b=d_pool_b, pool_scale=d_pool_scale,
                              conv_w=d_cw, conv_b=d_cb, lru_w_a=d_wa, lru_b_a=d_ba, lru_w_x=d_wx, lru_b_x=d_bx,
                              lru_lambda=d_lam, norm_ffn2=dg3)

        grads = [d_up1, d_dn1.reshape(N_CHIPS, -1, D), d_in, d_pu, d_lu.reshape(N_CHIPS, -1, D),
                 d_o.reshape(N_CHIPS, -1, D), d_up2, d_dn2.reshape(N_CHIPS, -1, D)]
        recv = _exchange_halves(grads, f"exchange_halves_{l}")
        sums = [_pair_sum(g, r, core, f"pair_sum_{n}_{l}") for g, r, n in zip(grads, recv, BIG)]
        got = _exchange_chips([p[1] for p in sums], f"exchange_chips_{l}")
        full = [_chip_sum(place, p[0], q, f, l, f"chip_sum_{n}_{l}") for p, q, f, n in zip(sums, got, full, BIG)]
        full = _join_halves(full, l, f"join_halves_{l}")

    small_full = {n: jnp.stack([small_grads[l][n].reshape(W[n].shape[1:] if n != "conv_w" else (CONV_WIDTH, lw))
                                for l in range(L)]) for n in SMALL if n != "final_norm"}
    small_full["final_norm"] = d_final.reshape(D)
    small_list = [small_full[n] for n in SMALL]
    packed = _pack(small_list, 8 * N_DEV)
    summed = _all_reduce_small(packed.reshape(N_DEV, -1, LANES), "all_reduce_small")
    small_sum = dict(zip(SMALL, _unpack(summed, small_list)))
    cs = conv_w.shape[2]
    small_sum["conv_w"] = lax.dynamic_slice_in_dim(small_sum["conv_w"], chip * cs, cs, axis=2)

    out = {}
    for n, g in zip(BIG, full):
        out[n] = _adamw(W[n], g, M[n], V[n], f"adamw_{n}")
    packs = [_pack([d[n] for n in SMALL], 1024) for d in (W, small_sum, M, V)]
    res = _adamw(*[p[None] for p in packs], "adamw_small")
    like = [W[n] for n in SMALL]
    unpacked = [_unpack(r[0], like) for r in res]
    for j, n in enumerate(SMALL):
        out[n] = tuple(u[j] for u in unpacked)

    return (loss, dx.reshape(x.shape), *[out[n][0] for n in WEIGHTS], *[out[n][1] for n in WEIGHTS],
            *[out[n][2] for n in WEIGHTS], *[out[n][3] for n in WEIGHTS])
```

```python
import functools

import jax
import jax.numpy as jnp
from jax import lax
from jax.experimental import pallas as pl
from jax.experimental.pallas import tpu as pltpu

F32 = jnp.float32
MM = jnp.bfloat16
XFER = jnp.bfloat16

EPS = 1e-6
LRU_C = 8.0
POOL_WINDOWS = (2, 4, 8, 16)
CONV_WIDTH = 4
ADAM_LR, ADAM_B1, ADAM_B2, ADAM_EPS, ADAM_WD, ADAM_STEP = 0.001, 0.9, 0.999, 1e-08, 0.01, 10

N_CHIPS = 4
N_DEV = 8
LANES = 128
SHIFT_PAD = 8
TOKEN_TILE = 256
VMEM_LIMIT = 52 * 1024 * 1024
MESH = pl.DeviceIdType.MESH
ANY = pl.BlockSpec(memory_space=pl.ANY)


def _params(sem=None):
    return pltpu.CompilerParams(dimension_semantics=sem, vmem_limit_bytes=VMEM_LIMIT)


def _dot(a, b):
    return jnp.dot(a.astype(MM), b.astype(MM), preferred_element_type=F32)


def _dot_nt(a, b):
    return lax.dot_general(a.astype(MM), b.astype(MM), (((1,), (1,)), ((), ())), preferred_element_type=F32)


def _dot_tn(a, b):
    return lax.dot_general(a.astype(MM), b.astype(MM), (((0,), (0,)), ((), ())), preferred_element_type=F32)


def _rms(x, g):
    r = lax.rsqrt(jnp.mean(x * x, axis=-1, keepdims=True) + EPS)
    xh = x * r
    return r, xh, xh * g


def _rms_bwd(dh, xh, r, g):
    dxh = dh * g
    return r * (dxh - xh * jnp.mean(dxh * xh, axis=-1, keepdims=True))


def _accumulate(ref, val, first):
    @pl.when(first)
    def _():
        ref[...] = val

    @pl.when(jnp.logical_not(first))
    def _():
        ref[...] += val


def _row_block(rows, cols, itemsize=4, budget=1 << 20):
    best = None
    for rb in range(16, rows + 1, 16):
        if rows % rb == 0 and rb * cols * itemsize <= budget:
            best = rb
    return best if best is not None else rows


def _norm_matmul_fwd(x, gain, w, name):
    T, D = x.shape
    K, _, tn = w.shape
    tm = min(TOKEN_TILE, T)

    def body(x_ref, g_ref, w_ref, u_ref):
        _, _, h = _rms(x_ref[...], g_ref[...])
        u_ref[...] = _dot(h, w_ref[...])

    return pl.pallas_call(
        body, name=name, grid=(K, T // tm),
        in_specs=[pl.BlockSpec((tm, D), lambda k, i: (i, 0)), pl.BlockSpec((1, D), lambda k, i: (0, 0)),
                  pl.BlockSpec((None, D, tn), lambda k, i: (k, 0, 0))],
        out_specs=pl.BlockSpec((tm, tn), lambda k, i: (i, k)),
        out_shape=jax.ShapeDtypeStruct((T, K * tn), F32),
        compiler_params=_params(("arbitrary", "arbitrary")),
    )(x, gain, w)


def _norm_matmul_bwd(x, gain, du, du_spec, w, dres, name):
    T, D = x.shape
    K, _, tn = w.shape
    tm = min(TOKEN_TILE, T)
    ni = T // tm

    def body(x_ref, g_ref, du_ref, w_ref, dres_ref, dx_ref, dw_ref, dg_ref, dh_acc, dg_acc):
        k, i = pl.program_id(0), pl.program_id(1)
        g = g_ref[...]
        r, xh, h = _rms(x_ref[...], g)
        du_t = du_ref[...].astype(MM)
        rows = pl.ds(pl.multiple_of(i * tm, tm), tm)
        part = _dot_nt(du_t, w_ref[...])

        @pl.when(k == 0)
        def _():
            dh_acc[rows, :] = part

        @pl.when(k > 0)
        def _():
            dh_acc[rows, :] += part

        _accumulate(dw_ref, _dot_tn(h, du_t), i == 0)

        @pl.when(k == K - 1)
        def _():
            dh = dh_acc[rows, :]
            _accumulate(dg_acc, jnp.sum(dh * xh, axis=0, keepdims=True), i == 0)
            dx_ref[...] = dres_ref[...] + _rms_bwd(dh, xh, r, g)

            @pl.when(i == ni - 1)
            def _():
                dg_ref[...] = dg_acc[...]

    def last(k, i):
        return (jnp.where(k == K - 1, i, 0), 0)

    return pl.pallas_call(
        body, name=name, grid=(K, ni),
        in_specs=[pl.BlockSpec((tm, D), lambda k, i: (i, 0)), pl.BlockSpec((1, D), lambda k, i: (0, 0)),
                  du_spec(tm, tn), pl.BlockSpec((None, D, tn), lambda k, i: (k, 0, 0)),
                  pl.BlockSpec((tm, D), last)],
        out_specs=[pl.BlockSpec((tm, D), last), pl.BlockSpec((None, D, tn), lambda k, i: (k, 0, 0)),
                   pl.BlockSpec((1, D), lambda k, i: (0, 0))],
        out_shape=[jax.ShapeDtypeStruct((T, D), F32), jax.ShapeDtypeStruct((K, D, tn), F32),
                   jax.ShapeDtypeStruct((1, D), F32)],
        scratch_shapes=[pltpu.VMEM((T, D), F32), pltpu.VMEM((1, D), F32)],
        compiler_params=_params(("arbitrary", "arbitrary")),
    )(x, gain, du, w, dres)


def _swiglu_down_fwd(u, x, wd, name):
    T, D = x.shape
    Fh = wd.shape[0]
    tm = min(TOKEN_TILE, T)

    def body(a_ref, b_ref, x_ref, wd_ref, o_ref):
        a = a_ref[...]
        s = a * jax.nn.sigmoid(a) * b_ref[...]
        o_ref[...] = x_ref[...] + 0.5 * _dot(s, wd_ref[...])

    return pl.pallas_call(
        body, name=name, grid=(T // tm,),
        in_specs=[pl.BlockSpec((tm, Fh), lambda i: (i, 0)), pl.BlockSpec((tm, Fh), lambda i: (i, 1)),
                  pl.BlockSpec((tm, D), lambda i: (i, 0)), pl.BlockSpec((Fh, D), lambda i: (0, 0))],
        out_specs=pl.BlockSpec((tm, D), lambda i: (i, 0)),
        out_shape=jax.ShapeDtypeStruct((T, D), F32),
        compiler_params=_params(("arbitrary",)),
    )(u, u, x, wd)


def _swiglu_down_bwd(u, dxn, wd, name):
    T, D = dxn.shape
    Fh = wd.shape[0]
    tm = min(TOKEN_TILE, T)
    nj = 2 if Fh % (2 * LANES) == 0 else 1
    tf = Fh // nj

    def body(a_ref, b_ref, dxn_ref, wd_ref, du_ref, dwd_ref):
        i = pl.program_id(1)
        a, b = a_ref[...], b_ref[...]
        dyh = (0.5 * dxn_ref[...]).astype(MM)
        ds = _dot_nt(dyh, wd_ref[...])
        sig = jax.nn.sigmoid(a)
        sa = a * sig
        _accumulate(dwd_ref, _dot_tn(sa * b, dyh), i == 0)
        du_ref[0] = (ds * b * (sig * (1.0 + a * (1.0 - sig)))).astype(MM)
        du_ref[1] = (ds * sa).astype(MM)

    return pl.pallas_call(
        body, name=name, grid=(nj, T // tm),
        in_specs=[pl.BlockSpec((tm, tf), lambda j, i: (i, j)), pl.BlockSpec((tm, tf), lambda j, i: (i, j + nj)),
                  pl.BlockSpec((tm, D), lambda j, i: (i, 0)), pl.BlockSpec((tf, D), lambda j, i: (j, 0))],
        out_specs=[pl.BlockSpec((2, tm, tf), lambda j, i: (0, i, j)), pl.BlockSpec((tf, D), lambda j, i: (j, 0))],
        out_shape=[jax.ShapeDtypeStruct((2, T, Fh), MM), jax.ShapeDtypeStruct((Fh, D), F32)],
        compiler_params=_params(("arbitrary", "arbitrary")),
    )(u, u, dxn, wd)


def _mix_branches(pm, zl, gp_logit, gr_logit, wpu_ref, wlu_ref):
    y_pool = jnp.concatenate([_dot(pm, wpu_ref[k]) for k in range(N_CHIPS)], axis=1)
    y_lru = _dot(zl, wlu_ref[...])
    return y_pool, y_lru, jax.nn.sigmoid(gp_logit), jax.nn.sigmoid(gr_logit)


def _mix_out_specs(tm, D, pw, lw, cs):
    gate0 = (pw + 2 * lw) // D
    return [pl.BlockSpec((tm, D), lambda i: (i, gate0)), pl.BlockSpec((tm, D), lambda i: (i, gate0 + 1)),
            pl.BlockSpec((tm, pw), lambda i: (i, 0)), pl.BlockSpec((tm, lw), lambda i: (i, 0)),
            pl.BlockSpec((N_CHIPS, pw, cs), lambda i: (0, 0, 0)), pl.BlockSpec((lw, D), lambda i: (0, 0)),
            pl.BlockSpec((D, D), lambda i: (0, 0))]


def _mix_out_fwd(x, proj, pm, zl, wpu, wlu, wo, name):
    T, D = x.shape
    pw, lw, cs = pm.shape[1], zl.shape[1], wpu.shape[2]
    assert (pw + 2 * lw) % D == 0
    tm = min(TOKEN_TILE, T)

    def body(x_ref, gp_ref, gr_ref, pm_ref, zl_ref, wpu_ref, wlu_ref, wo_ref, o_ref):
        y_pool, y_lru, gp, gr = _mix_branches(pm_ref[...], zl_ref[...], gp_ref[...], gr_ref[...], wpu_ref, wlu_ref)
        o_ref[...] = x_ref[...] + _dot(gp * y_pool + gr * y_lru, wo_ref[...])

    return pl.pallas_call(
        body, name=name, grid=(T // tm,),
        in_specs=[pl.BlockSpec((tm, D), lambda i: (i, 0))] + _mix_out_specs(tm, D, pw, lw, cs),
        out_specs=pl.BlockSpec((tm, D), lambda i: (i, 0)),
        out_shape=jax.ShapeDtypeStruct((T, D), F32),
        compiler_params=_params(("arbitrary",)),
    )(x, proj, proj, pm, zl, wpu, wlu, wo)


def _mix_out_bwd(dxn, proj, pm, zl, wpu, wlu, wo, name):
    T, D = dxn.shape
    pw, lw, cs = pm.shape[1], zl.shape[1], wpu.shape[2]
    tm = min(TOKEN_TILE, T)
    ni = T // tm

    def body(dxn_ref, gp_ref, gr_ref, pm_ref, zl_ref, wpu_ref, wlu_ref, wo_ref,
             dgl_ref, dpm_ref, dzl_ref, dwo_hbm, dwpu_hbm, dwlu_hbm, acc_o, acc_pu, acc_lu, sem):
        i = pl.program_id(0)
        pm, zl = pm_ref[...], zl_ref[...]
        y_pool, y_lru, gp, gr = _mix_branches(pm, zl, gp_ref[...], gr_ref[...], wpu_ref, wlu_ref)
        dxn_t = dxn_ref[...].astype(MM)
        dmerged = _dot_nt(dxn_t, wo_ref[...])
        _accumulate(acc_o, _dot_tn(gp * y_pool + gr * y_lru, dxn_t), i == 0)
        dgl_ref[:, :D] = (dmerged * y_pool * (gp * (1.0 - gp))).astype(MM)
        dgl_ref[:, D:] = (dmerged * y_lru * (gr * (1.0 - gr))).astype(MM)
        dyp = (dmerged * gp).astype(MM)
        dyl = (dmerged * gr).astype(MM)
        dpm = None
        for k in range(N_CHIPS):
            dyp_k = dyp[:, k * cs:(k + 1) * cs]
            part = _dot_nt(dyp_k, wpu_ref[k])
            dpm = part if dpm is None else dpm + part
            _accumulate(acc_pu.at[k], _dot_tn(pm, dyp_k), i == 0)
        dpm_ref[...] = dpm
        dzl_ref[...] = _dot_nt(dyl, wlu_ref[...])
        _accumulate(acc_lu, _dot_tn(zl, dyl), i == 0)

        @pl.when(i == ni - 1)
        def _():
            copies = [pltpu.make_async_copy(acc_o, dwo_hbm, sem.at[0]), pltpu.make_async_copy(acc_pu, dwpu_hbm, sem.at[1]),
                      pltpu.make_async_copy(acc_lu, dwlu_hbm, sem.at[2])]
            for cp in copies:
                cp.start()
            for cp in copies:
                cp.wait()

    return pl.pallas_call(
        body, name=name, grid=(ni,),
        in_specs=[pl.BlockSpec((tm, D), lambda i: (i, 0))] + _mix_out_specs(tm, D, pw, lw, cs),
        out_specs=[pl.BlockSpec((tm, 2 * D), lambda i: (i, 0)), pl.BlockSpec((tm, pw), lambda i: (i, 0)),
                   pl.BlockSpec((tm, lw), lambda i: (i, 0)), ANY, ANY, ANY],
        out_shape=[jax.ShapeDtypeStruct((T, 2 * D), MM), jax.ShapeDtypeStruct((T, pw), F32),
                   jax.ShapeDtypeStruct((T, lw), F32), jax.ShapeDtypeStruct((D, D), F32),
                   jax.ShapeDtypeStruct((N_CHIPS, pw, cs), F32), jax.ShapeDtypeStruct((lw, D), F32)],
        scratch_shapes=[pltpu.VMEM((D, D), F32), pltpu.VMEM((N_CHIPS, pw, cs), F32), pltpu.VMEM((lw, D), F32),
                        pltpu.SemaphoreType.DMA((3,))],
        compiler_params=_params(("arbitrary",)),
    )(dxn, proj, proj, pm, zl, wpu, wlu, wo)


def _shifted(buf, val, shift, S):
    buf[pl.ds(SHIFT_PAD, S), :] = val
    return buf[pl.ds(SHIFT_PAD - shift, S), :]


def _zero_pads(buf, S):
    z = jnp.zeros((SHIFT_PAD, buf.shape[1]), F32)
    buf[pl.ds(0, SHIFT_PAD), :] = z
    buf[pl.ds(SHIFT_PAD + S, SHIFT_PAD), :] = z


def _window_sum(buf, val, window, S, lead=False):
    acc, width = val, 1
    while width < window:
        acc = acc + _shifted(buf, acc, -width if lead else width, S)
        width *= 2
    return acc


def _pool_count(S, window):
    t = lax.broadcasted_iota(jnp.int32, (S, LANES), 0)
    return jnp.minimum(t + 1, window).astype(F32)


def _pool_fwd_group(u, g, window, pw_ref, pb_ref, buf, S):
    pooled = _window_sum(buf, u, window, S) / _pool_count(S, window) - u
    return pooled, _dot(pooled, pw_ref[g]) + pb_ref[g]


def _pool_fwd(proj, pool_w, pool_b, pool_scale, name):
    S = proj.shape[0]
    G = pool_w.shape[0]
    pw = G * LANES

    def body(u_ref, pw_ref, pb_ref, ps_ref, pm_ref, buf):
        _zero_pads(buf, S)
        for g in range(G):
            cols = slice(g * LANES, (g + 1) * LANES)
            _, mixed = _pool_fwd_group(u_ref[:, cols], g, POOL_WINDOWS[g], pw_ref, pb_ref, buf, S)
            pm_ref[:, cols] = (mixed * ps_ref[:, cols]).astype(MM)

    return pl.pallas_call(
        body, name=name, grid=(1,),
        in_specs=[pl.BlockSpec((S, pw), lambda i: (0, 0)), pl.BlockSpec((G, LANES, LANES), lambda i: (0, 0, 0)),
                  pl.BlockSpec((G, 1, LANES), lambda i: (0, 0, 0)), pl.BlockSpec((1, pw), lambda i: (0, 0))],
        out_specs=pl.BlockSpec((S, pw), lambda i: (0, 0)),
        out_shape=jax.ShapeDtypeStruct((S, pw), MM),
        scratch_shapes=[pltpu.VMEM((S + 2 * SHIFT_PAD, LANES), F32)],
        compiler_params=_params(("arbitrary",)),
    )(proj, pool_w, pool_b, pool_scale)


def _pool_bwd(proj, dpm, pool_w, pool_b, pool_scale, name):
    S = proj.shape[0]
    G = pool_w.shape[0]
    pw = G * LANES

    def body(u_ref, dpm_ref, pw_ref, pb_ref, ps_ref, du_ref, dpw_ref, dpb_ref, dps_ref, buf):
        _zero_pads(buf, S)
        for g in range(G):
            cols = slice(g * LANES, (g + 1) * LANES)
            window = POOL_WINDOWS[g]
            pooled, mixed = _pool_fwd_group(u_ref[:, cols], g, window, pw_ref, pb_ref, buf, S)
            d_out = dpm_ref[:, cols]
            dmixed = d_out * ps_ref[:, cols]
            dps_ref[:, cols] = jnp.sum(d_out * mixed, axis=0, keepdims=True)
            dpb_ref[g] = jnp.sum(dmixed, axis=0, keepdims=True)
            dpw_ref[g] = _dot_tn(pooled, dmixed)
            dpooled = _dot_nt(dmixed, pw_ref[g])
            q = dpooled / _pool_count(S, window)
            du_ref[:, cols] = (_window_sum(buf, q, window, S, lead=True) - dpooled).astype(MM)

    return pl.pallas_call(
        body, name=name, grid=(1,),
        in_specs=[pl.BlockSpec((S, pw), lambda i: (0, 0)), pl.BlockSpec((S, pw), lambda i: (0, 0)),
                  pl.BlockSpec((G, LANES, LANES), lambda i: (0, 0, 0)),
                  pl.BlockSpec((G, 1, LANES), lambda i: (0, 0, 0)), pl.BlockSpec((1, pw), lambda i: (0, 0))],
        out_specs=[pl.BlockSpec((S, pw), lambda i: (0, 0)), pl.BlockSpec((G, LANES, LANES), lambda i: (0, 0, 0)),
                   pl.BlockSpec((G, 1, LANES), lambda i: (0, 0, 0)), pl.BlockSpec((1, pw), lambda i: (0, 0))],
        out_shape=[jax.ShapeDtypeStruct((S, pw), MM), jax.ShapeDtypeStruct((G, LANES, LANES), F32),
                   jax.ShapeDtypeStruct((G, 1, LANES), F32), jax.ShapeDtypeStruct((1, pw), F32)],
        scratch_shapes=[pltpu.VMEM((S + 2 * SHIFT_PAD, LANES), F32)],
        compiler_params=_params(("arbitrary",)),
    )(proj, dpm, pool_w, pool_b, pool_scale)


def _scan(a, b, bufs, S, reverse=False):
    pad = S // 2
    seq = pl.ds(pad, S)
    A, B = bufs[:2], bufs[2:]
    A[0][seq, :] = a
    B[0][seq, :] = b
    cur, d = 0, 1
    while d < S:
        sh = pl.ds(pad + d if reverse else pad - d, S)
        a_c = A[cur][seq, :]
        B[1 - cur][seq, :] = a_c * B[cur][sh, :] + B[cur][seq, :]
        if 2 * d < S:
            A[1 - cur][seq, :] = a_c * A[cur][sh, :]
        cur, d = 1 - cur, 2 * d
    return B[cur][seq, :]


def _init_scan_pads(bufs, S):
    pad = S // 2
    for n, buf in enumerate(bufs):
        fill = jnp.full((pad, LANES), 1.0 if n < 2 else 0.0, F32)
        buf[pl.ds(0, pad), :] = fill
        buf[pl.ds(pad + S, pad), :] = fill


def _gelu_and_grad(x):
    c = 0.7978845608028654
    x2 = x * x
    th = jnp.tanh(c * (x + 0.044715 * (x2 * x)))
    gelu = 0.5 * x * (1.0 + th)
    grad = 0.5 * (1.0 + th) + 0.5 * x * (1.0 - th * th) * (c * (1.0 + 3 * 0.044715 * x2))
    return gelu, grad


def _lru_head_fwd(ul, cw, cb, wa, ba, wx, bx, lam, sbuf, bufs, S):
    sbuf[pl.ds(SHIFT_PAD, S), :] = ul
    v = None
    for k in range(CONV_WIDTH):
        term = sbuf[pl.ds(SHIFT_PAD - (CONV_WIDTH - 1 - k), S), :] * cw[k:k + 1, :]
        v = term if v is None else v + term
    v = v + cb
    r = jax.nn.sigmoid(_dot(v, wa) + ba)
    ig = jax.nn.sigmoid(_dot(v, wx) + bx)
    sp = jax.nn.softplus(-lam)
    log_a = -LRU_C * r * sp
    a = jnp.exp(log_a)
    mult = jnp.sqrt(-jnp.tanh(log_a) * (1.0 + a * a))
    h = _scan(a, mult * (ig * v), bufs, S)
    return v, r, ig, sp, a, mult, h


def _lru_specs(S, H, lw, pw):
    b0 = pw // LANES
    return [pl.BlockSpec((S, LANES), lambda h: (0, b0 + h)), pl.BlockSpec((S, LANES), lambda h: (0, b0 + H + h)),
            pl.BlockSpec((CONV_WIDTH, LANES), lambda h: (0, h)), pl.BlockSpec((1, LANES), lambda h: (0, h)),
            pl.BlockSpec((None, LANES, LANES), lambda h: (h, 0, 0)), pl.BlockSpec((None, 1, LANES), lambda h: (h, 0, 0)),
            pl.BlockSpec((None, LANES, LANES), lambda h: (h, 0, 0)), pl.BlockSpec((None, 1, LANES), lambda h: (h, 0, 0)),
            pl.BlockSpec((1, LANES), lambda h: (0, h))]


def _lru_scratch(S):
    return [pltpu.VMEM((S + 2 * SHIFT_PAD, LANES), F32)] + [pltpu.VMEM((2 * S, LANES), F32)] * 4


def _lru_fwd(proj, conv_w, conv_b, wa, ba, wx, bx, lam, pw, name):
    S = proj.shape[0]
    H = wa.shape[0]
    lw = H * LANES

    def body(ul_ref, ug_ref, cw_ref, cb_ref, wa_ref, ba_ref, wx_ref, bx_ref, lam_ref, zl_ref, sbuf, *bufs):
        @pl.when(pl.program_id(0) == 0)
        def _():
            _zero_pads(sbuf, S)
            _init_scan_pads(bufs, S)

        h = _lru_head_fwd(ul_ref[...], cw_ref[...], cb_ref[...], wa_ref[...], ba_ref[...], wx_ref[...], bx_ref[...],
                          lam_ref[...], sbuf, bufs, S)[-1]
        zl_ref[...] = (h * jax.nn.gelu(ug_ref[...])).astype(MM)

    return pl.pallas_call(
        body, name=name, grid=(H,),
        in_specs=_lru_specs(S, H, lw, pw),
        out_specs=pl.BlockSpec((S, LANES), lambda h: (0, h)),
        out_shape=jax.ShapeDtypeStruct((S, lw), MM),
        scratch_shapes=_lru_scratch(S),
        compiler_params=_params(("arbitrary",)),
    )(proj, proj, conv_w, conv_b, wa, ba, wx, bx, lam)


def _lru_bwd(proj, dzl, conv_w, conv_b, wa, ba, wx, bx, lam, pw, name):
    S = proj.shape[0]
    H = wa.shape[0]
    lw = H * LANES

    def body(ul_ref, ug_ref, cw_ref, cb_ref, wa_ref, ba_ref, wx_ref, bx_ref, lam_ref, dzl_ref,
             dul_ref, dug_ref, dcw_ref, dcb_ref, dwa_ref, dba_ref, dwx_ref, dbx_ref, dlam_ref, sbuf, *bufs):
        @pl.when(pl.program_id(0) == 0)
        def _():
            _zero_pads(sbuf, S)
            _init_scan_pads(bufs, S)

        ul, cw, lam = ul_ref[...], cw_ref[...], lam_ref[...]
        wa, wx = wa_ref[...], wx_ref[...]
        v, r, ig, sp, a, mult, h = _lru_head_fwd(ul, cw, cb_ref[...], wa, ba_ref[...], wx, bx_ref[...], lam, sbuf, bufs, S)
        gelu, gelu_grad = _gelu_and_grad(ug_ref[...])
        dzl_t = dzl_ref[...]
        dug_ref[...] = (dzl_t * h * gelu_grad).astype(MM)
        a_next = _shifted(sbuf, a, -1, S)
        lam_t = _scan(a_next, dzl_t * gelu, bufs, S, reverse=True)
        da = lam_t * _shifted(sbuf, h, 1, S)
        d_iv = lam_t * mult
        d_log_a = da * a - (lam_t * (ig * v)) * (a * a) / mult
        dlam_ref[...] = jnp.sum(d_log_a * r, axis=0, keepdims=True) * (LRU_C * jax.nn.sigmoid(-lam))
        dra = (d_log_a * (-LRU_C * sp)) * (r * (1.0 - r))
        dia = (d_iv * v) * (ig * (1.0 - ig))
        dba_ref[...] = jnp.sum(dra, axis=0, keepdims=True)
        dbx_ref[...] = jnp.sum(dia, axis=0, keepdims=True)
        dwa_ref[...] = _dot_tn(v, dra)
        dwx_ref[...] = _dot_tn(v, dia)
        dv = d_iv * ig + _dot_nt(dra, wa) + _dot_nt(dia, wx)
        dcb_ref[...] = jnp.sum(dv, axis=0, keepdims=True)
        sbuf[pl.ds(SHIFT_PAD, S), :] = ul
        for k in range(CONV_WIDTH):
            dcw_ref[k:k + 1, :] = jnp.sum(dv * sbuf[pl.ds(SHIFT_PAD - (CONV_WIDTH - 1 - k), S), :], axis=0, keepdims=True)
        sbuf[pl.ds(SHIFT_PAD, S), :] = dv
        dul = None
        for k in range(CONV_WIDTH):
            term = sbuf[pl.ds(SHIFT_PAD + (CONV_WIDTH - 1 - k), S), :] * cw[k:k + 1, :]
            dul = term if dul is None else dul + term
        dul_ref[...] = dul.astype(MM)

    head_mat = pl.BlockSpec((None, LANES, LANES), lambda h: (h, 0, 0))
    head_vec = pl.BlockSpec((None, 1, LANES), lambda h: (h, 0, 0))
    col = pl.BlockSpec((S, LANES), lambda h: (0, h))
    row = pl.BlockSpec((1, LANES), lambda h: (0, h))
    return pl.pallas_call(
        body, name=name, grid=(H,),
        in_specs=_lru_specs(S, H, lw, pw) + [col],
        out_specs=[col, col, pl.BlockSpec((CONV_WIDTH, LANES), lambda h: (0, h)), row, head_mat, head_vec, head_mat,
                   head_vec, row],
        out_shape=[jax.ShapeDtypeStruct((S, lw), MM), jax.ShapeDtypeStruct((S, lw), MM),
                   jax.ShapeDtypeStruct((CONV_WIDTH, lw), F32), jax.ShapeDtypeStruct((1, lw), F32),
                   jax.ShapeDtypeStruct((H, LANES, LANES), F32), jax.ShapeDtypeStruct((H, 1, LANES), F32),
                   jax.ShapeDtypeStruct((H, LANES, LANES), F32), jax.ShapeDtypeStruct((H, 1, LANES), F32),
                   jax.ShapeDtypeStruct((1, lw), F32)],
        scratch_shapes=_lru_scratch(S),
        compiler_params=_params(("arbitrary",)),
    )(proj, proj, conv_w, conv_b, wa, ba, wx, bx, lam, dzl)


def _loss_head(x, gain, target, name):
    T, D = x.shape
    tm = min(TOKEN_TILE, T)
    ni = T // tm

    def body(x_ref, g_ref, t_ref, loss_ref, dx_ref, dg_ref, loss_acc, dg_acc):
        i = pl.program_id(0)
        g = g_ref[...]
        r, xh, y = _rms(x_ref[...], g)
        err = y - t_ref[...]
        part = 0.5 * jnp.sum(jnp.mean(err * err, axis=-1, keepdims=True), axis=0, keepdims=True)
        _accumulate(loss_acc, jnp.broadcast_to(part, (1, LANES)), i == 0)
        dy = err * (1.0 / D)
        _accumulate(dg_acc, jnp.sum(dy * xh, axis=0, keepdims=True), i == 0)
        dx_ref[...] = _rms_bwd(dy, xh, r, g)

        @pl.when(i == ni - 1)
        def _():
            loss_ref[...] = loss_acc[...]
            dg_ref[...] = dg_acc[...]

    return pl.pallas_call(
        body, name=name, grid=(ni,),
        in_specs=[pl.BlockSpec((tm, D), lambda i: (i, 0)), pl.BlockSpec((1, D), lambda i: (0, 0)),
                  pl.BlockSpec((tm, D), lambda i: (i, 0))],
        out_specs=[pl.BlockSpec((1, LANES), lambda i: (0, 0)), pl.BlockSpec((tm, D), lambda i: (i, 0)),
                   pl.BlockSpec((1, D), lambda i: (0, 0))],
        out_shape=[jax.ShapeDtypeStruct((1, LANES), F32), jax.ShapeDtypeStruct((T, D), F32),
                   jax.ShapeDtypeStruct((1, D), F32)],
        scratch_shapes=[pltpu.VMEM((1, LANES), F32), pltpu.VMEM((1, D), F32)],
        compiler_params=_params(("arbitrary",)),
    )(x, gain, target)


def _my_core():
    return lax.axis_index("c")


def _my_chip():
    return 2 * lax.axis_index("x") + lax.axis_index("y")


def _pair_sum(a, recv, name):
    n, R, C = a.shape
    hr = R // 2

    def body(a_ref, r_ref, pf_ref, pb_ref):
        s = a_ref[...] + r_ref[...]
        pf_ref[...] = s
        pb_ref[...] = s.astype(XFER)

    piece = pl.BlockSpec((None, hr, C), lambda k: (k, 0, 0))
    return pl.pallas_call(
        body, name=name, grid=(n,),
        in_specs=[pl.BlockSpec((None, hr, C), lambda k: (k, _my_core(), 0)), piece],
        out_specs=[piece, piece],
        out_shape=[jax.ShapeDtypeStruct((n, hr, C), F32), jax.ShapeDtypeStruct((n, hr, C), XFER)],
        compiler_params=_params(("arbitrary",)),
    )(a, recv)


def _chip_sum(pf, others, full, layer, name):
    _, hr, C = pf.shape
    rb = _row_block(hr, C)
    nb = hr // rb

    def body(o_ref, q_ref, full_ref, g_ref):
        g_ref[...] = ((o_ref[...] + q_ref[0].astype(F32)) + q_ref[1].astype(F32)) + q_ref[2].astype(F32)

    return pl.pallas_call(
        body, name=name, grid=(nb,),
        in_specs=[pl.BlockSpec((None, rb, C), lambda i: (_my_chip(), i, 0)),
                  pl.BlockSpec((N_CHIPS - 1, rb, C), lambda i: (0, i, 0)), ANY],
        out_specs=pl.BlockSpec((None, rb, C), lambda i: (layer, _my_core() * nb + i, 0)),
        out_shape=jax.ShapeDtypeStruct(full.shape, F32),
        input_output_aliases={2: 0},
        compiler_params=_params(("arbitrary",)),
    )(pf, others, full)


def _adamw(w, g, m, v, name):
    L, R, C = w.shape
    rb = _row_block(R, C)

    def body(w_ref, g_ref, m_ref, v_ref, go_ref, d_ref, mo_ref, vo_ref):
        g_t = g_ref[...]
        m_t = ADAM_B1 * m_ref[...] + (1.0 - ADAM_B1) * g_t
        v_t = ADAM_B2 * v_ref[...] + (1.0 - ADAM_B2) * (g_t * g_t)
        m_hat = m_t / (1.0 - ADAM_B1 ** ADAM_STEP)
        v_hat = v_t / (1.0 - ADAM_B2 ** ADAM_STEP)
        go_ref[...] = g_t
        d_ref[...] = -ADAM_LR * (m_hat / (jnp.sqrt(v_hat) + ADAM_EPS) + ADAM_WD * w_ref[...])
        mo_ref[...] = m_t
        vo_ref[...] = v_t

    blk = pl.BlockSpec((None, rb, C), lambda l, i: (l, i, 0))
    return pl.pallas_call(
        body, name=name, grid=(L, R // rb), in_specs=[blk] * 4, out_specs=[blk] * 4,
        out_shape=[jax.ShapeDtypeStruct((L, R, C), F32)] * 4,
        compiler_params=_params(("arbitrary", "arbitrary")),
    )(w, g, m, v)


def _place():
    x, y, c = lax.axis_index("x"), lax.axis_index("y"), lax.axis_index("c")
    others = [(1 - x, y), (x, 1 - y), (1 - x, 1 - y)]
    return x, y, c, 2 * x + y, others


def _half(c, rows):
    return pl.ds(pl.multiple_of(c * (rows // 2), 16), rows // 2)


HBM_SPEC = pl.BlockSpec(memory_space=pltpu.HBM)
SEM_SPEC = pl.BlockSpec(memory_space=pltpu.SEMAPHORE)
SPLIT = pltpu.CompilerParams(has_side_effects=pltpu.SideEffectType.DATAFLOW_SIDE_EFFECTING)
TOKEN = jax.ShapeDtypeStruct((8, LANES), F32)


def _in_hbm(a):
    return pltpu.with_memory_space_constraint(a, pltpu.HBM)


def _gather_copies(srcs, lands, sems, layer):
    send, recv, osend, orecv = sems
    x, y, c, k, others = _place()
    pairs = []
    for a in range(len(srcs)):
        rows = _half(c, srcs[a].shape[-2])
        for r, (px, py) in enumerate(others):
            def ici(slot, src, a=a, r=r, px=px, py=py, rows=rows):
                return pltpu.make_async_remote_copy(src_ref=src.at[rows], dst_ref=lands[a].at[slot].at[rows],
                                                    send_sem=send.at[a * 3 + r], recv_sem=recv.at[a * 3 + r],
                                                    device_id=(px, py, c), device_id_type=MESH)
            pairs.append((ici(k, srcs[a].at[layer]), ici(2 * px + py, srcs[a].at[layer])))
        own = pltpu.make_async_remote_copy(src_ref=srcs[a].at[layer], dst_ref=lands[a].at[k], send_sem=osend.at[a],
                                           recv_sem=orecv.at[a], device_id=(x, y, 1 - c), device_id_type=MESH)
        pairs.append((own, own))
    return pairs


def _gather_start(shards, layer, after, name):
    n = len(shards)
    lands = [lax.empty((N_CHIPS,) + s.shape[1:], s.dtype) for s in shards]

    def body(*refs):
        srcs, zone = refs[:n], refs[n:2 * n]
        sems = refs[2 * n + 1:2 * n + 5]
        token = refs[-1]
        for started, _ in _gather_copies(srcs, zone, sems, layer):
            started.start()
        token[...] = jnp.zeros_like(token)

    sem_shapes = [pltpu.SemaphoreType.DMA((3 * n,))] * 2 + [pltpu.SemaphoreType.DMA((n,))] * 2
    res = pl.pallas_call(
        body, name=name,
        in_specs=[HBM_SPEC] * (2 * n) + [ANY],
        out_specs=[SEM_SPEC] * 4 + [HBM_SPEC] * n + [pl.BlockSpec(memory_space=pltpu.VMEM)],
        out_shape=sem_shapes + [pltpu.HBM(z.shape, z.dtype) for z in lands] + [TOKEN],
        input_output_aliases={n + a: 4 + a for a in range(n)},
        compiler_params=SPLIT,
    )(*[_in_hbm(s) for s in shards], *[_in_hbm(z) for z in lands], after)
    return res[:4], res[4:4 + n], res[-1]


def _gather_wait(shards, layer, sems, lands, after, name):
    n = len(shards)

    def body(*refs):
        srcs, zone = refs[:n], refs[n:2 * n]
        for started, arriving in _gather_copies(srcs, zone, refs[2 * n:2 * n + 4], layer):
            started.wait_send()
            arriving.wait_recv()

    return pl.pallas_call(
        body, name=name,
        in_specs=[HBM_SPEC] * (2 * n) + [SEM_SPEC] * 4 + [ANY],
        out_specs=[HBM_SPEC] * n,
        out_shape=[pltpu.HBM(z.shape, z.dtype) for z in lands],
        input_output_aliases={n + a: a for a in range(n)},
        compiler_params=SPLIT,
    )(*[_in_hbm(s) for s in shards], *lands, *sems, after)


def _forward_halves(lands, name):
    n = len(lands)

    def body(*refs):
        outs = refs[n:2 * n]
        send, recv = refs[2 * n:]
        x, y, c, _, others = _place()

        def copy(a, r, half):
            px, py = others[r]
            rows = outs[a].at[2 * px + py].at[_half(half, outs[a].shape[-2])]
            return pltpu.make_async_remote_copy(src_ref=rows, dst_ref=rows, send_sem=send.at[a * 3 + r],
                                                recv_sem=recv.at[a * 3 + r], device_id=(x, y, 1 - c), device_id_type=MESH)

        every = [(a, r) for a in range(n) for r in range(3)]
        for a, r in every:
            copy(a, r, c).start()
        for a, r in every:
            copy(a, r, 1 - c).wait_recv()
        for a, r in every:
            copy(a, r, c).wait_send()

    return pl.pallas_call(
        body, name=name, in_specs=[ANY] * n, out_specs=[ANY] * n,
        out_shape=[jax.ShapeDtypeStruct(z.shape, z.dtype) for z in lands],
        input_output_aliases={a: a for a in range(n)},
        scratch_shapes=[pltpu.SemaphoreType.DMA((3 * n,))] * 2,
    )(*lands)


def _gather_small(small, name):
    def body(in_ref, out_ref, send, recv, local):
        x, y, c, k, others = _place()
        mine = pltpu.make_async_copy(in_ref, out_ref.at[k], local)
        mine.start()
        copies = []
        for r, (px, py) in enumerate(others):
            copies.append(pltpu.make_async_remote_copy(src_ref=in_ref, dst_ref=out_ref.at[k], send_sem=send.at[r],
                                                       recv_sem=recv.at[r], device_id=(px, py, c), device_id_type=MESH))
            copies[-1].start()
        for r, (px, py) in enumerate(others):
            pltpu.make_async_remote_copy(src_ref=in_ref, dst_ref=out_ref.at[2 * px + py], send_sem=send.at[r],
                                         recv_sem=recv.at[r], device_id=(px, py, c), device_id_type=MESH).wait_recv()
        for cp in copies:
            cp.wait_send()
        mine.wait()

    return pl.pallas_call(
        body, name=name, in_specs=[ANY], out_specs=ANY,
        out_shape=jax.ShapeDtypeStruct((N_CHIPS,) + small.shape, small.dtype),
        scratch_shapes=[pltpu.SemaphoreType.DMA((3,))] * 2 + [pltpu.SemaphoreType.DMA],
    )(small)


def _exchange_halves(grads, name):
    n = len(grads)

    def body(*refs):
        ins, outs, send, recv = refs[:n], refs[n:2 * n], refs[2 * n], refs[2 * n + 1]
        x, y, c, _, _ = _place()
        copies = []
        for a in range(n):
            R = ins[a].shape[1]
            cp = pltpu.make_async_remote_copy(src_ref=ins[a].at[:, _half(1 - c, R), :], dst_ref=outs[a],
                                              send_sem=send.at[a], recv_sem=recv.at[a], device_id=(x, y, 1 - c),
                                              device_id_type=MESH)
            cp.start()
            copies.append(cp)
        for cp in copies:
            cp.wait()

    return pl.pallas_call(
        body, name=name, in_specs=[ANY] * n, out_specs=[ANY] * n,
        out_shape=[jax.ShapeDtypeStruct((g.shape[0], g.shape[1] // 2, g.shape[2]), g.dtype) for g in grads],
        scratch_shapes=[pltpu.SemaphoreType.DMA((n,))] * 2,
    )(*grads)


def _chip_copies(pbs, gots, send, recv):
    x, y, c, k, others = _place()
    return [pltpu.make_async_remote_copy(src_ref=pbs[a].at[2 * px + py], dst_ref=gots[a].at[r],
                                         send_sem=send.at[a * 3 + r], recv_sem=recv.at[a * 3 + r],
                                         device_id=(px, py, c), device_id_type=MESH)
            for a in range(len(pbs)) for r, (px, py) in enumerate(others)]


def _exchange_chips_start(pb, name):
    n = len(pb)
    gots = [lax.empty((N_CHIPS - 1,) + p.shape[1:], XFER) for p in pb]

    def body(*refs):
        for cp in _chip_copies(refs[:n], refs[n:2 * n], refs[2 * n], refs[2 * n + 1]):
            cp.start()
        refs[-1][...] = jnp.zeros_like(refs[-1])

    res = pl.pallas_call(
        body, name=name,
        in_specs=[HBM_SPEC] * (2 * n),
        out_specs=[SEM_SPEC] * 2 + [HBM_SPEC] * (2 * n) + [pl.BlockSpec(memory_space=pltpu.VMEM)],
        out_shape=[pltpu.SemaphoreType.DMA((3 * n,))] * 2 + [pltpu.HBM(p.shape, p.dtype) for p in pb]
        + [pltpu.HBM(g.shape, g.dtype) for g in gots] + [TOKEN],
        input_output_aliases={a: 2 + a for a in range(2 * n)},
        compiler_params=SPLIT,
    )(*[_in_hbm(p) for p in pb], *[_in_hbm(g) for g in gots])
    return res[:2], res[2:2 + n], res[2 + n:2 + 2 * n], res[-1]


def _exchange_chips_wait(sems, pb, gots, after, name):
    n = len(pb)

    def body(*refs):
        for cp in _chip_copies(refs[:n], refs[n:2 * n], refs[2 * n], refs[2 * n + 1]):
            cp.wait_send()
            cp.wait_recv()

    res = pl.pallas_call(
        body, name=name,
        in_specs=[HBM_SPEC] * (2 * n) + [SEM_SPEC] * 2 + [ANY],
        out_specs=[HBM_SPEC] * (2 * n),
        out_shape=[pltpu.HBM(p.shape, p.dtype) for p in pb] + [pltpu.HBM(g.shape, g.dtype) for g in gots],
        input_output_aliases={a: a for a in range(2 * n)},
        compiler_params=SPLIT,
    )(*pb, *gots, *sems, after)
    return res[n:]


def _join_halves(fulls, layer, name):
    n = len(fulls)

    def body(*refs):
        outs = refs[n:2 * n]
        send, recv = refs[2 * n:]
        x, y, c, _, _ = _place()

        def copy(a, half):
            rows = outs[a].at[layer, _half(half, outs[a].shape[1]), :]
            return pltpu.make_async_remote_copy(src_ref=rows, dst_ref=rows, send_sem=send.at[a], recv_sem=recv.at[a],
                                                device_id=(x, y, 1 - c), device_id_type=MESH)

        for a in range(n):
            copy(a, c).start()
        for a in range(n):
            copy(a, 1 - c).wait_recv()
        for a in range(n):
            copy(a, c).wait_send()

    return pl.pallas_call(
        body, name=name, in_specs=[ANY] * n, out_specs=[ANY] * n,
        out_shape=[jax.ShapeDtypeStruct(f.shape, f.dtype) for f in fulls],
        input_output_aliases={a: a for a in range(n)},
        scratch_shapes=[pltpu.SemaphoreType.DMA((n,))] * 2,
    )(*fulls)


def _all_reduce_small(packed, name):
    _, rows, lanes = packed.shape

    def body(in_ref, out_ref, stage, send1, recv1, send2, recv2):
        x, y, c, _, _ = _place()
        me = 4 * x + 2 * y + c
        ids = [(p // 4, (p // 2) % 2, p % 2) for p in range(N_DEV)]

        def scatter(p):
            return pltpu.make_async_remote_copy(src_ref=in_ref.at[p], dst_ref=stage.at[me], send_sem=send1.at[p],
                                                recv_sem=recv1.at[me], device_id=ids[p], device_id_type=MESH)

        def spread(p, to):
            return pltpu.make_async_remote_copy(src_ref=out_ref.at[p], dst_ref=out_ref.at[p], send_sem=send2.at[to],
                                                recv_sem=recv2.at[p], device_id=ids[to], device_id_type=MESH)

        for p in range(N_DEV):
            @pl.when(me != p)
            def _(p=p):
                scatter(p).start()

        total = None
        for p in range(N_DEV):
            @pl.when(me != p)
            def _(p=p):
                pltpu.make_async_remote_copy(src_ref=in_ref.at[0], dst_ref=stage.at[p], send_sem=send1.at[p],
                                             recv_sem=recv1.at[p], device_id=ids[p], device_id_type=MESH).wait_recv()

            @pl.when(me == p)
            def _(p=p):
                stage[p] = in_ref[me]

        for p in range(N_DEV):
            total = stage[p] if total is None else total + stage[p]
        out_ref[me] = total
        for p in range(N_DEV):
            @pl.when(me != p)
            def _(p=p):
                scatter(p).wait_send()
                spread(me, p).start()
        for p in range(N_DEV):
            @pl.when(me != p)
            def _(p=p):
                spread(p, p).wait_recv()
        for p in range(N_DEV):
            @pl.when(me != p)
            def _(p=p):
                spread(me, p).wait_send()

    vmem = pl.BlockSpec(memory_space=pltpu.VMEM)
    return pl.pallas_call(
        body, name=name, in_specs=[vmem], out_specs=vmem,
        out_shape=jax.ShapeDtypeStruct(packed.shape, F32),
        scratch_shapes=[pltpu.VMEM(packed.shape, F32)] + [pltpu.SemaphoreType.DMA((N_DEV,))] * 4,
        compiler_params=pltpu.CompilerParams(vmem_limit_bytes=VMEM_LIMIT),
    )(packed)


BIG = ("ffn1_w_up", "ffn1_w_down", "w_in", "w_pool_up", "w_lru_up", "w_out", "ffn2_w_up", "ffn2_w_down")
SMALL = ("norm_ffn1", "norm_mix", "pool_w", "pool_b", "pool_scale", "conv_w", "conv_b", "lru_w_a", "lru_b_a", "lru_w_x",
         "lru_b_x", "lru_lambda", "norm_ffn2", "final_norm")
WEIGHTS = ("norm_ffn1", "ffn1_w_up", "ffn1_w_down", "norm_mix", "w_in", "pool_w", "pool_b", "pool_scale", "w_pool_up",
           "conv_w", "conv_b", "lru_w_a", "lru_b_a", "lru_w_x", "lru_b_x", "lru_lambda", "w_lru_up", "w_out", "norm_ffn2",
           "ffn2_w_up", "ffn2_w_down", "final_norm")


def _pack(arrays, rows_multiple):
    flat = jnp.concatenate([a.reshape(-1) for a in arrays])
    rows = -(-flat.shape[0] // LANES)
    rows = -(-rows // rows_multiple) * rows_multiple
    return jnp.pad(flat, (0, rows * LANES - flat.shape[0])).reshape(rows, LANES)


def _unpack(packed, like):
    flat, out, at = packed.reshape(-1), [], 0
    for a in like:
        out.append(flat[at:at + a.size].reshape(a.shape))
        at += a.size
    return out


def kernel(x, norm_ffn1, ffn1_w_up, ffn1_w_down, norm_mix, w_in, pool_w, pool_b, pool_scale, w_pool_up, conv_w, conv_b, lru_w_a, lru_b_a, lru_w_x, lru_b_x, lru_lambda, w_lru_up, w_out, norm_ffn2, ffn2_w_up, ffn2_w_down, final_norm, loss_target, m_norm_ffn1, m_ffn1_w_up, m_ffn1_w_down, m_norm_mix, m_w_in, m_pool_w, m_pool_b, m_pool_scale, m_w_pool_up, m_conv_w, m_conv_b, m_lru_w_a, m_lru_b_a, m_lru_w_x, m_lru_b_x, m_lru_lambda, m_w_lru_up, m_w_out, m_norm_ffn2, m_ffn2_w_up, m_ffn2_w_down, m_final_norm, v_norm_ffn1, v_ffn1_w_up, v_ffn1_w_down, v_norm_mix, v_w_in, v_pool_w, v_pool_b, v_pool_scale, v_w_pool_up, v_conv_w, v_conv_b, v_lru_w_a, v_lru_b_a, v_lru_w_x, v_lru_b_x, v_lru_lambda, v_w_lru_up, v_w_out, v_norm_ffn2, v_ffn2_w_up, v_ffn2_w_down, v_final_norm):
    given = dict(locals())
    W = {n: given[n] for n in WEIGHTS}
    M = {n: given["m_" + n] for n in WEIGHTS}
    V = {n: given["v_" + n] for n in WEIGHTS}
    L = norm_ffn1.shape[0]
    T, D = x.shape[1], x.shape[2]
    G, H = pool_w.shape[1], lru_w_a.shape[1]
    pw, lw = G * LANES, H * LANES
    chip = 2 * lax.axis_index("x") + lax.axis_index("y")

    def tied(gain, token):
        return gain if token is None else gain + token[:1, :1]

    shards = [W[n].astype(MM) for n in BIG]
    conv_full = _gather_small(conv_w.reshape(L * CONV_WIDTH, conv_w.shape[2]), "gather_conv_w")
    conv_full = conv_full.reshape(N_CHIPS, L, CONV_WIDTH, -1).transpose(1, 2, 0, 3).reshape(L, CONV_WIDTH, lw)

    xs = x.reshape(T, D)
    saved = []
    started = {0: _gather_start(shards, 0, xs, "gather_start_0")}
    if L > 1:
        started[1] = _gather_start(shards, 1, started[0][2], "gather_start_1")
    for l in range(L):
        sems, lands, _ = started[l]
        after = xs if l > 0 else started[min(1, L - 1)][2]
        got = _forward_halves(_gather_wait(shards, l, sems, lands, after, f"gather_wait_{l}"), f"forward_halves_{l}")
        if l + 2 < L:
            started[l + 2] = _gather_start(shards, l + 2, got[0], f"gather_start_{l + 2}")
        w_up1, w_dn1, w_i, w_pu, w_lu, w_o, w_up2, w_dn2 = got
        w_dn1, w_dn2 = w_dn1.reshape(-1, D), w_dn2.reshape(-1, D)
        w_lu, w_o = w_lu.reshape(-1, D), w_o.reshape(-1, D)
        small = dict(pool_w=pool_w[l], pool_b=pool_b[l].reshape(G, 1, LANES), pool_scale=pool_scale[l].reshape(1, pw),
                     conv_w=conv_full[l], conv_b=conv_b[l].reshape(1, lw), wa=lru_w_a[l],
                     ba=lru_b_a[l].reshape(H, 1, LANES), wx=lru_w_x[l], bx=lru_b_x[l].reshape(H, 1, LANES),
                     lam=lru_lambda[l].reshape(1, lw))
        g1, g2, g3 = norm_ffn1[l].reshape(1, D), norm_mix[l].reshape(1, D), norm_ffn2[l].reshape(1, D)

        x0 = xs
        u1 = _norm_matmul_fwd(x0, tied(g1, started[l + 2][2] if l + 2 < L else None), w_up1, f"ffn1_up_{l}")
        x1 = _swiglu_down_fwd(u1, x0, w_dn1, f"ffn1_down_{l}")
        proj = _norm_matmul_fwd(x1, g2, w_i, f"mix_in_{l}")
        pm = _pool_fwd(proj, small["pool_w"], small["pool_b"], small["pool_scale"], f"pool_{l}")
        zl = _lru_fwd(proj, small["conv_w"], small["conv_b"], small["wa"], small["ba"], small["wx"], small["bx"],
                      small["lam"], pw, f"lru_{l}")
        x2 = _mix_out_fwd(x1, proj, pm, zl, w_pu, w_lu, w_o, f"mix_out_{l}")
        u2 = _norm_matmul_fwd(x2, g3, w_up2, f"ffn2_up_{l}")
        xs = _swiglu_down_fwd(u2, x2, w_dn2, f"ffn2_down_{l}")
        saved.append(dict(x0=x0, u1=u1, x1=x1, proj=proj, pm=pm, zl=zl, x2=x2, u2=u2, small=small, g=(g1, g2, g3),
                          w=(w_up1, w_dn1, w_i, w_pu, w_lu, w_o, w_up2, w_dn2)))

    loss_part, dx, d_final = _loss_head(xs, final_norm.reshape(1, D), loss_target.reshape(T, D), "loss_head")
    loss = lax.psum(loss_part[0, 0], ("x", "y", "c"))

    full = [jnp.zeros(W[n].shape, F32) for n in BIG]
    small_grads = [None] * L
    du_spec_ffn = lambda tm, tn: pl.BlockSpec((None, tm, tn), lambda k, i: (k // 2, i, k % 2))
    du_spec_mix = lambda tm, tn: pl.BlockSpec((tm, tn), lambda k, i: (i, k))

    def finish_exchange(l, flying, after, full):
        sems, pb, gots, _, pf = flying
        got = _exchange_chips_wait(sems, pb, gots, after, f"exchange_chips_wait_{l}")
        full = [_chip_sum(p, q, f, l, f"chip_sum_{n}_{l}") for p, q, f, n in zip(pf, got, full, BIG)]
        return _join_halves(full, l, f"join_halves_{l}")

    flying = None
    for l in reversed(range(L)):
        s = saved[l]
        w_up1, w_dn1, w_i, w_pu, w_lu, w_o, w_up2, w_dn2 = s["w"]
        g1, g2, g3 = s["g"]
        sm = s["small"]
        du2, d_dn2 = _swiglu_down_bwd(s["u2"], dx, w_dn2, f"ffn2_down_bwd_{l}")
        dx, d_up2, dg3 = _norm_matmul_bwd(s["x2"], tied(g3, flying[3] if flying else None), du2, du_spec_ffn, w_up2, dx,
                                          f"ffn2_up_bwd_{l}")
        dgl, dpm, dzl, d_o, d_pu, d_lu = _mix_out_bwd(dx, s["proj"], s["pm"], s["zl"], w_pu, w_lu, w_o, f"mix_out_bwd_{l}")
        du_pool, d_pool_w, d_pool_b, d_pool_scale = _pool_bwd(s["proj"], dpm, sm["pool_w"], sm["pool_b"], sm["pool_scale"],
                                                            f"pool_bwd_{l}")
        dul, dug, d_cw, d_cb, d_wa, d_ba, d_wx, d_bx, d_lam = _lru_bwd(
            s["proj"], dzl, sm["conv_w"], sm["conv_b"], sm["wa"], sm["ba"], sm["wx"], sm["bx"], sm["lam"], pw, f"lru_bwd_{l}")
        dproj = jnp.concatenate([du_pool, dul, dug, dgl], axis=1)
        dx, d_in, dg2 = _norm_matmul_bwd(s["x1"], g2, dproj, du_spec_mix, w_i, dx, f"mix_in_bwd_{l}")
        du1, d_dn1 = _swiglu_down_bwd(s["u1"], dx, w_dn1, f"ffn1_down_bwd_{l}")
        dx, d_up1, dg1 = _norm_matmul_bwd(s["x0"], g1, du1, du_spec_ffn, w_up1, dx, f"ffn1_up_bwd_{l}")
        small_grads[l] = dict(norm_ffn1=dg1, norm_mix=dg2, pool_w=d_pool_w, pool_b=d_pool_b, pool_scale=d_pool_scale,
                              conv_w=d_cw, conv_b=d_cb, lru_w_a=d_wa, lru_b_a=d_ba, lru_w_x=d_wx, lru_b_x=d_bx,
                              lru_lambda=d_lam, norm_ffn2=dg3)

        grads = [d_up1, d_dn1.reshape(N_CHIPS, -1, D), d_in, d_pu, d_lu.reshape(N_CHIPS, -1, D),
                 d_o.reshape(N_CHIPS, -1, D), d_up2, d_dn2.reshape(N_CHIPS, -1, D)]
        recv = _exchange_halves(grads, f"exchange_halves_{l}")
        sums = [_pair_sum(g, r, f"pair_sum_{n}_{l}") for g, r, n in zip(grads, recv, BIG)]
        now = _exchange_chips_start([p[1] for p in sums], f"exchange_chips_start_{l}") + ([p[0] for p in sums],)
        if flying is not None:
            full = finish_exchange(l + 1, flying, now[3], full)
        flying = now

    small_full = {n: jnp.stack([small_grads[l][n].reshape(W[n].shape[1:] if n != "conv_w" else (CONV_WIDTH, lw))
                                for l in range(L)]) for n in SMALL if n != "final_norm"}
    small_full["final_norm"] = d_final.reshape(D)
    small_list = [small_full[n] for n in SMALL]
    packed = _pack(small_list, 8 * N_DEV) + flying[3][:1, :1]
    summed = _all_reduce_small(packed.reshape(N_DEV, -1, LANES), "all_reduce_small")
    small_sum = dict(zip(SMALL, _unpack(summed, small_list)))
    cs = conv_w.shape[2]
    small_sum["conv_w"] = lax.dynamic_slice_in_dim(small_sum["conv_w"], chip * cs, cs, axis=2)

    out = {}
    packs = [_pack([d[n] for n in SMALL], 1024) for d in (W, small_sum, M, V)]
    res = _adamw(*[p[None] for p in packs], "adamw_small")
    like = [W[n] for n in SMALL]
    unpacked = [_unpack(r[0], like) for r in res]
    for j, n in enumerate(SMALL):
        out[n] = tuple(u[j] for u in unpacked)
    full = finish_exchange(0, flying, res[1], full)
    for n, g in zip(BIG, full):
        out[n] = _adamw(W[n], g, M[n], V[n], f"adamw_{n}")

    return (loss, dx.reshape(x.shape), *[out[n][0] for n in WEIGHTS], *[out[n][1] for n in WEIGHTS],
            *[out[n][2] for n in WEIGHTS], *[out[n][3] for n in WEIGHTS])
```

```python
import functools

import jax
import jax.numpy as jnp
from jax import lax
from jax.experimental import pallas as pl
from jax.experimental.pallas import tpu as pltpu

F32 = jnp.float32
MM = jnp.bfloat16
XFER = jnp.bfloat16

EPS = 1e-6
LRU_C = 8.0
POOL_WINDOWS = (2, 4, 8, 16)
CONV_WIDTH = 4
ADAM_LR, ADAM_B1, ADAM_B2, ADAM_EPS, ADAM_WD, ADAM_STEP = 0.001, 0.9, 0.999, 1e-08, 0.01, 10

N_CHIPS = 4
LANES = 128
SHIFT_PAD = 8
TOKEN_TILE = 256
VMEM_LIMIT = 52 * 1024 * 1024
MESH = pl.DeviceIdType.MESH
ANY = pl.BlockSpec(memory_space=pl.ANY)


def _params(sem=None):
    return pltpu.CompilerParams(dimension_semantics=sem, vmem_limit_bytes=VMEM_LIMIT)


def _dot(a, b):
    return jnp.dot(a.astype(MM), b.astype(MM), preferred_element_type=F32)


def _dot_nt(a, b):
    return lax.dot_general(a.astype(MM), b.astype(MM), (((1,), (1,)), ((), ())), preferred_element_type=F32)


def _dot_tn(a, b):
    return lax.dot_general(a.astype(MM), b.astype(MM), (((0,), (0,)), ((), ())), preferred_element_type=F32)


def _rms(x, g):
    r = lax.rsqrt(jnp.mean(x * x, axis=-1, keepdims=True) + EPS)
    xh = x * r
    return r, xh, xh * g


def _rms_bwd(dh, xh, r, g):
    dxh = dh * g
    return r * (dxh - xh * jnp.mean(dxh * xh, axis=-1, keepdims=True))


def _accumulate(ref, val, first):
    @pl.when(first)
    def _():
        ref[...] = val

    @pl.when(jnp.logical_not(first))
    def _():
        ref[...] += val


def _row_block(rows, cols, itemsize=4, budget=1 << 20):
    best = None
    for rb in range(16, rows + 1, 16):
        if rows % rb == 0 and rb * cols * itemsize <= budget:
            best = rb
    return best if best is not None else rows


def _norm_matmul_fwd(x, gain, w, name):
    T, D = x.shape
    K, _, tn = w.shape
    tm = min(TOKEN_TILE, T)

    def body(x_ref, g_ref, w_ref, u_ref):
        _, _, h = _rms(x_ref[...], g_ref[...])
        u_ref[...] = _dot(h, w_ref[...])

    return pl.pallas_call(
        body, name=name, grid=(K, T // tm),
        in_specs=[pl.BlockSpec((tm, D), lambda k, i: (i, 0)), pl.BlockSpec((1, D), lambda k, i: (0, 0)),
                  pl.BlockSpec((None, D, tn), lambda k, i: (k, 0, 0))],
        out_specs=pl.BlockSpec((tm, tn), lambda k, i: (i, k)),
        out_shape=jax.ShapeDtypeStruct((T, K * tn), F32),
        compiler_params=_params(("arbitrary", "arbitrary")),
    )(x, gain, w)


def _norm_matmul_bwd(x, gain, du, du_spec, w, dres, name):
    T, D = x.shape
    K, _, tn = w.shape
    tm = min(TOKEN_TILE, T)
    ni = T // tm

    def body(x_ref, g_ref, du_ref, w_ref, dres_ref, dx_ref, dw_ref, dg_ref, dh_acc, dg_acc):
        k, i = pl.program_id(0), pl.program_id(1)
        g = g_ref[...]
        r, xh, h = _rms(x_ref[...], g)
        du_t = du_ref[...].astype(MM)
        rows = pl.ds(pl.multiple_of(i * tm, tm), tm)
        part = _dot_nt(du_t, w_ref[...])

        @pl.when(k == 0)
        def _():
            dh_acc[rows, :] = part

        @pl.when(k > 0)
        def _():
            dh_acc[rows, :] += part

        _accumulate(dw_ref, _dot_tn(h, du_t), i == 0)

        @pl.when(k == K - 1)
        def _():
            dh = dh_acc[rows, :]
            _accumulate(dg_acc, jnp.sum(dh * xh, axis=0, keepdims=True), i == 0)
            dx_ref[...] = dres_ref[...] + _rms_bwd(dh, xh, r, g)

            @pl.when(i == ni - 1)
            def _():
                dg_ref[...] = dg_acc[...]

    def last(k, i):
        return (jnp.where(k == K - 1, i, 0), 0)

    return pl.pallas_call(
        body, name=name, grid=(K, ni),
        in_specs=[pl.BlockSpec((tm, D), lambda k, i: (i, 0)), pl.BlockSpec((1, D), lambda k, i: (0, 0)),
                  du_spec(tm, tn), pl.BlockSpec((None, D, tn), lambda k, i: (k, 0, 0)),
                  pl.BlockSpec((tm, D), last)],
        out_specs=[pl.BlockSpec((tm, D), last), pl.BlockSpec((None, D, tn), lambda k, i: (k, 0, 0)),
                   pl.BlockSpec((1, D), lambda k, i: (0, 0))],
        out_shape=[jax.ShapeDtypeStruct((T, D), F32), jax.ShapeDtypeStruct((K, D, tn), F32),
                   jax.ShapeDtypeStruct((1, D), F32)],
        scratch_shapes=[pltpu.VMEM((T, D), F32), pltpu.VMEM((1, D), F32)],
        compiler_params=_params(("arbitrary", "arbitrary")),
    )(x, gain, du, w, dres)


def _swiglu_down_fwd(u, x, wd, name):
    T, D = x.shape
    Fh = wd.shape[0]
    tm = min(TOKEN_TILE, T)

    def body(a_ref, b_ref, x_ref, wd_ref, o_ref):
        a = a_ref[...]
        s = a * jax.nn.sigmoid(a) * b_ref[...]
        o_ref[...] = x_ref[...] + 0.5 * _dot(s, wd_ref[...])

    return pl.pallas_call(
        body, name=name, grid=(T // tm,),
        in_specs=[pl.BlockSpec((tm, Fh), lambda i: (i, 0)), pl.BlockSpec((tm, Fh), lambda i: (i, 1)),
                  pl.BlockSpec((tm, D), lambda i: (i, 0)), pl.BlockSpec((Fh, D), lambda i: (0, 0))],
        out_specs=pl.BlockSpec((tm, D), lambda i: (i, 0)),
        out_shape=jax.ShapeDtypeStruct((T, D), F32),
        compiler_params=_params(("arbitrary",)),
    )(u, u, x, wd)


def _swiglu_down_bwd(u, dxn, wd, name):
    T, D = dxn.shape
    Fh = wd.shape[0]
    tm = min(TOKEN_TILE, T)
    nj = 2 if Fh % (2 * LANES) == 0 else 1
    tf = Fh // nj

    def body(a_ref, b_ref, dxn_ref, wd_ref, du_ref, dwd_ref):
        i = pl.program_id(1)
        a, b = a_ref[...], b_ref[...]
        dyh = (0.5 * dxn_ref[...]).astype(MM)
        ds = _dot_nt(dyh, wd_ref[...])
        sig = jax.nn.sigmoid(a)
        sa = a * sig
        _accumulate(dwd_ref, _dot_tn(sa * b, dyh), i == 0)
        du_ref[0] = (ds * b * (sig * (1.0 + a * (1.0 - sig)))).astype(MM)
        du_ref[1] = (ds * sa).astype(MM)

    return pl.pallas_call(
        body, name=name, grid=(nj, T // tm),
        in_specs=[pl.BlockSpec((tm, tf), lambda j, i: (i, j)), pl.BlockSpec((tm, tf), lambda j, i: (i, j + nj)),
                  pl.BlockSpec((tm, D), lambda j, i: (i, 0)), pl.BlockSpec((tf, D), lambda j, i: (j, 0))],
        out_specs=[pl.BlockSpec((2, tm, tf), lambda j, i: (0, i, j)), pl.BlockSpec((tf, D), lambda j, i: (j, 0))],
        out_shape=[jax.ShapeDtypeStruct((2, T, Fh), MM), jax.ShapeDtypeStruct((Fh, D), F32)],
        compiler_params=_params(("arbitrary", "arbitrary")),
    )(u, u, dxn, wd)


def _mix_branches(pm, zl, gp_logit, gr_logit, wpu_ref, wlu_ref):
    y_pool = jnp.concatenate([_dot(pm, wpu_ref[k]) for k in range(N_CHIPS)], axis=1)
    y_lru = _dot(zl, wlu_ref[...])
    return y_pool, y_lru, jax.nn.sigmoid(gp_logit), jax.nn.sigmoid(gr_logit)


def _mix_out_specs(tm, D, pw, lw, cs):
    gate0 = (pw + 2 * lw) // D
    return [pl.BlockSpec((tm, D), lambda i: (i, gate0)), pl.BlockSpec((tm, D), lambda i: (i, gate0 + 1)),
            pl.BlockSpec((tm, pw), lambda i: (i, 0)), pl.BlockSpec((tm, lw), lambda i: (i, 0)),
            pl.BlockSpec((N_CHIPS, pw, cs), lambda i: (0, 0, 0)), pl.BlockSpec((lw, D), lambda i: (0, 0)),
            pl.BlockSpec((D, D), lambda i: (0, 0))]


def _mix_out_fwd(x, proj, pm, zl, wpu, wlu, wo, name):
    T, D = x.shape
    pw, lw, cs = pm.shape[1], zl.shape[1], wpu.shape[2]
    assert (pw + 2 * lw) % D == 0
    tm = min(TOKEN_TILE, T)

    def body(x_ref, gp_ref, gr_ref, pm_ref, zl_ref, wpu_ref, wlu_ref, wo_ref, o_ref):
        y_pool, y_lru, gp, gr = _mix_branches(pm_ref[...], zl_ref[...], gp_ref[...], gr_ref[...], wpu_ref, wlu_ref)
        o_ref[...] = x_ref[...] + _dot(gp * y_pool + gr * y_lru, wo_ref[...])

    return pl.pallas_call(
        body, name=name, grid=(T // tm,),
        in_specs=[pl.BlockSpec((tm, D), lambda i: (i, 0))] + _mix_out_specs(tm, D, pw, lw, cs),
        out_specs=pl.BlockSpec((tm, D), lambda i: (i, 0)),
        out_shape=jax.ShapeDtypeStruct((T, D), F32),
        compiler_params=_params(("arbitrary",)),
    )(x, proj, proj, pm, zl, wpu, wlu, wo)


def _mix_out_bwd(dxn, proj, pm, zl, wpu, wlu, wo, name):
    T, D = dxn.shape
    pw, lw, cs = pm.shape[1], zl.shape[1], wpu.shape[2]
    tm = min(TOKEN_TILE, T)
    ni = T // tm

    def body(dxn_ref, gp_ref, gr_ref, pm_ref, zl_ref, wpu_ref, wlu_ref, wo_ref,
             dgl_ref, dpm_ref, dzl_ref, dwo_hbm, dwpu_hbm, dwlu_hbm, acc_o, acc_pu, acc_lu, sem):
        i = pl.program_id(0)
        pm, zl = pm_ref[...], zl_ref[...]
        y_pool, y_lru, gp, gr = _mix_branches(pm, zl, gp_ref[...], gr_ref[...], wpu_ref, wlu_ref)
        dxn_t = dxn_ref[...].astype(MM)
        dmerged = _dot_nt(dxn_t, wo_ref[...])
        _accumulate(acc_o, _dot_tn(gp * y_pool + gr * y_lru, dxn_t), i == 0)
        dgl_ref[:, :D] = (dmerged * y_pool * (gp * (1.0 - gp))).astype(MM)
        dgl_ref[:, D:] = (dmerged * y_lru * (gr * (1.0 - gr))).astype(MM)
        dyp = (dmerged * gp).astype(MM)
        dyl = (dmerged * gr).astype(MM)
        dpm = None
        for k in range(N_CHIPS):
            dyp_k = dyp[:, k * cs:(k + 1) * cs]
            part = _dot_nt(dyp_k, wpu_ref[k])
            dpm = part if dpm is None else dpm + part
            _accumulate(acc_pu.at[k], _dot_tn(pm, dyp_k), i == 0)
        dpm_ref[...] = dpm
        dzl_ref[...] = _dot_nt(dyl, wlu_ref[...])
        _accumulate(acc_lu, _dot_tn(zl, dyl), i == 0)

        @pl.when(i == ni - 1)
        def _():
            copies = [pltpu.make_async_copy(acc_o, dwo_hbm, sem.at[0]), pltpu.make_async_copy(acc_pu, dwpu_hbm, sem.at[1]),
                      pltpu.make_async_copy(acc_lu, dwlu_hbm, sem.at[2])]
            for cp in copies:
                cp.start()
            for cp in copies:
                cp.wait()

    return pl.pallas_call(
        body, name=name, grid=(ni,),
        in_specs=[pl.BlockSpec((tm, D), lambda i: (i, 0))] + _mix_out_specs(tm, D, pw, lw, cs),
        out_specs=[pl.BlockSpec((tm, 2 * D), lambda i: (i, 0)), pl.BlockSpec((tm, pw), lambda i: (i, 0)),
                   pl.BlockSpec((tm, lw), lambda i: (i, 0)), ANY, ANY, ANY],
        out_shape=[jax.ShapeDtypeStruct((T, 2 * D), MM), jax.ShapeDtypeStruct((T, pw), F32),
                   jax.ShapeDtypeStruct((T, lw), F32), jax.ShapeDtypeStruct((D, D), F32),
                   jax.ShapeDtypeStruct((N_CHIPS, pw, cs), F32), jax.ShapeDtypeStruct((lw, D), F32)],
        scratch_shapes=[pltpu.VMEM((D, D), F32), pltpu.VMEM((N_CHIPS, pw, cs), F32), pltpu.VMEM((lw, D), F32),
                        pltpu.SemaphoreType.DMA((3,))],
        compiler_params=_params(("arbitrary",)),
    )(dxn, proj, proj, pm, zl, wpu, wlu, wo)


def _shifted(buf, val, shift, S):
    buf[pl.ds(SHIFT_PAD, S), :] = val
    return buf[pl.ds(SHIFT_PAD - shift, S), :]


def _zero_pads(buf, S):
    z = jnp.zeros((SHIFT_PAD, buf.shape[1]), F32)
    buf[pl.ds(0, SHIFT_PAD), :] = z
    buf[pl.ds(SHIFT_PAD + S, SHIFT_PAD), :] = z


def _window_sum(buf, val, window, S, lead=False):
    acc, width = val, 1
    while width < window:
        acc = acc + _shifted(buf, acc, -width if lead else width, S)
        width *= 2
    return acc


def _pool_count(S, window):
    t = lax.broadcasted_iota(jnp.int32, (S, LANES), 0)
    return jnp.minimum(t + 1, window).astype(F32)


def _pool_fwd_group(u, g, window, pw_ref, pb_ref, buf, S):
    pooled = _window_sum(buf, u, window, S) / _pool_count(S, window) - u
    return pooled, _dot(pooled, pw_ref[g]) + pb_ref[g]


def _pool_fwd(proj, pool_w, pool_b, pool_scale, name):
    S = proj.shape[0]
    G = pool_w.shape[0]
    pw = G * LANES

    def body(u_ref, pw_ref, pb_ref, ps_ref, pm_ref, buf):
        _zero_pads(buf, S)
        for g in range(G):
            cols = slice(g * LANES, (g + 1) * LANES)
            _, mixed = _pool_fwd_group(u_ref[:, cols], g, POOL_WINDOWS[g], pw_ref, pb_ref, buf, S)
            pm_ref[:, cols] = (mixed * ps_ref[:, cols]).astype(MM)

    return pl.pallas_call(
        body, name=name, grid=(1,),
        in_specs=[pl.BlockSpec((S, pw), lambda i: (0, 0)), pl.BlockSpec((G, LANES, LANES), lambda i: (0, 0, 0)),
                  pl.BlockSpec((G, 1, LANES), lambda i: (0, 0, 0)), pl.BlockSpec((1, pw), lambda i: (0, 0))],
        out_specs=pl.BlockSpec((S, pw), lambda i: (0, 0)),
        out_shape=jax.ShapeDtypeStruct((S, pw), MM),
        scratch_shapes=[pltpu.VMEM((S + 2 * SHIFT_PAD, LANES), F32)],
        compiler_params=_params(("arbitrary",)),
    )(proj, pool_w, pool_b, pool_scale)


def _pool_bwd(proj, dpm, pool_w, pool_b, pool_scale, name):
    S = proj.shape[0]
    G = pool_w.shape[0]
    pw = G * LANES

    def body(u_ref, dpm_ref, pw_ref, pb_ref, ps_ref, du_ref, dpw_ref, dpb_ref, dps_ref, buf):
        _zero_pads(buf, S)
        for g in range(G):
            cols = slice(g * LANES, (g + 1) * LANES)
            window = POOL_WINDOWS[g]
            pooled, mixed = _pool_fwd_group(u_ref[:, cols], g, window, pw_ref, pb_ref, buf, S)
            d_out = dpm_ref[:, cols]
            dmixed = d_out * ps_ref[:, cols]
            dps_ref[:, cols] = jnp.sum(d_out * mixed, axis=0, keepdims=True)
            dpb_ref[g] = jnp.sum(dmixed, axis=0, keepdims=True)
            dpw_ref[g] = _dot_tn(pooled, dmixed)
            dpooled = _dot_nt(dmixed, pw_ref[g])
            q = dpooled / _pool_count(S, window)
            du_ref[:, cols] = (_window_sum(buf, q, window, S, lead=True) - dpooled).astype(MM)

    return pl.pallas_call(
        body, name=name, grid=(1,),
        in_specs=[pl.BlockSpec((S, pw), lambda i: (0, 0)), pl.BlockSpec((S, pw), lambda i: (0, 0)),
                  pl.BlockSpec((G, LANES, LANES), lambda i: (0, 0, 0)),
                  pl.BlockSpec((G, 1, LANES), lambda i: (0, 0, 0)), pl.BlockSpec((1, pw), lambda i: (0, 0))],
        out_specs=[pl.BlockSpec((S, pw), lambda i: (0, 0)), pl.BlockSpec((G, LANES, LANES), lambda i: (0, 0, 0)),
                   pl.BlockSpec((G, 1, LANES), lambda i: (0, 0, 0)), pl.BlockSpec((1, pw), lambda i: (0, 0))],
        out_shape=[jax.ShapeDtypeStruct((S, pw), MM), jax.ShapeDtypeStruct((G, LANES, LANES), F32),
                   jax.ShapeDtypeStruct((G, 1, LANES), F32), jax.ShapeDtypeStruct((1, pw), F32)],
        scratch_shapes=[pltpu.VMEM((S + 2 * SHIFT_PAD, LANES), F32)],
        compiler_params=_params(("arbitrary",)),
    )(proj, dpm, pool_w, pool_b, pool_scale)


def _scan(a, b, bufs, S, reverse=False):
    pad = S // 2
    seq = pl.ds(pad, S)
    A, B = bufs[:2], bufs[2:]
    A[0][seq, :] = a
    B[0][seq, :] = b
    cur, d = 0, 1
    while d < S:
        sh = pl.ds(pad + d if reverse else pad - d, S)
        a_c = A[cur][seq, :]
        B[1 - cur][seq, :] = a_c * B[cur][sh, :] + B[cur][seq, :]
        if 2 * d < S:
            A[1 - cur][seq, :] = a_c * A[cur][sh, :]
        cur, d = 1 - cur, 2 * d
    return B[cur][seq, :]


def _init_scan_pads(bufs, S):
    pad = S // 2
    for n, buf in enumerate(bufs):
        fill = jnp.full((pad, LANES), 1.0 if n < 2 else 0.0, F32)
        buf[pl.ds(0, pad), :] = fill
        buf[pl.ds(pad + S, pad), :] = fill


def _gelu_and_grad(x):
    c = 0.7978845608028654
    x2 = x * x
    th = jnp.tanh(c * (x + 0.044715 * (x2 * x)))
    gelu = 0.5 * x * (1.0 + th)
    grad = 0.5 * (1.0 + th) + 0.5 * x * (1.0 - th * th) * (c * (1.0 + 3 * 0.044715 * x2))
    return gelu, grad


def _lru_head_fwd(ul, cw, cb, wa, ba, wx, bx, lam, sbuf, bufs, S):
    sbuf[pl.ds(SHIFT_PAD, S), :] = ul
    v = None
    for k in range(CONV_WIDTH):
        term = sbuf[pl.ds(SHIFT_PAD - (CONV_WIDTH - 1 - k), S), :] * cw[k:k + 1, :]
        v = term if v is None else v + term
    v = v + cb
    r = jax.nn.sigmoid(_dot(v, wa) + ba)
    ig = jax.nn.sigmoid(_dot(v, wx) + bx)
    sp = jax.nn.softplus(-lam)
    log_a = -LRU_C * r * sp
    a = jnp.exp(log_a)
    mult = jnp.sqrt(-jnp.tanh(log_a) * (1.0 + a * a))
    h = _scan(a, mult * (ig * v), bufs, S)
    return v, r, ig, sp, a, mult, h


def _lru_specs(S, H, lw, pw):
    b0 = pw // LANES
    return [pl.BlockSpec((S, LANES), lambda h: (0, b0 + h)), pl.BlockSpec((S, LANES), lambda h: (0, b0 + H + h)),
            pl.BlockSpec((CONV_WIDTH, LANES), lambda h: (0, h)), pl.BlockSpec((1, LANES), lambda h: (0, h)),
            pl.BlockSpec((None, LANES, LANES), lambda h: (h, 0, 0)), pl.BlockSpec((None, 1, LANES), lambda h: (h, 0, 0)),
            pl.BlockSpec((None, LANES, LANES), lambda h: (h, 0, 0)), pl.BlockSpec((None, 1, LANES), lambda h: (h, 0, 0)),
            pl.BlockSpec((1, LANES), lambda h: (0, h))]


def _lru_scratch(S):
    return [pltpu.VMEM((S + 2 * SHIFT_PAD, LANES), F32)] + [pltpu.VMEM((2 * S, LANES), F32)] * 4


def _lru_fwd(proj, conv_w, conv_b, wa, ba, wx, bx, lam, pw, name):
    S = proj.shape[0]
    H = wa.shape[0]
    lw = H * LANES

    def body(ul_ref, ug_ref, cw_ref, cb_ref, wa_ref, ba_ref, wx_ref, bx_ref, lam_ref, zl_ref, sbuf, *bufs):
        @pl.when(pl.program_id(0) == 0)
        def _():
            _zero_pads(sbuf, S)
            _init_scan_pads(bufs, S)

        h = _lru_head_fwd(ul_ref[...], cw_ref[...], cb_ref[...], wa_ref[...], ba_ref[...], wx_ref[...], bx_ref[...],
                          lam_ref[...], sbuf, bufs, S)[-1]
        zl_ref[...] = (h * jax.nn.gelu(ug_ref[...])).astype(MM)

    return pl.pallas_call(
        body, name=name, grid=(H,),
        in_specs=_lru_specs(S, H, lw, pw),
        out_specs=pl.BlockSpec((S, LANES), lambda h: (0, h)),
        out_shape=jax.ShapeDtypeStruct((S, lw), MM),
        scratch_shapes=_lru_scratch(S),
        compiler_params=_params(("arbitrary",)),
    )(proj, proj, conv_w, conv_b, wa, ba, wx, bx, lam)


def _lru_bwd(proj, dzl, conv_w, conv_b, wa, ba, wx, bx, lam, pw, name):
    S = proj.shape[0]
    H = wa.shape[0]
    lw = H * LANES

    def body(ul_ref, ug_ref, cw_ref, cb_ref, wa_ref, ba_ref, wx_ref, bx_ref, lam_ref, dzl_ref,
             dul_ref, dug_ref, dcw_ref, dcb_ref, dwa_ref, dba_ref, dwx_ref, dbx_ref, dlam_ref, sbuf, *bufs):
        @pl.when(pl.program_id(0) == 0)
        def _():
            _zero_pads(sbuf, S)
            _init_scan_pads(bufs, S)

        ul, cw, lam = ul_ref[...], cw_ref[...], lam_ref[...]
        wa, wx = wa_ref[...], wx_ref[...]
        v, r, ig, sp, a, mult, h = _lru_head_fwd(ul, cw, cb_ref[...], wa, ba_ref[...], wx, bx_ref[...], lam, sbuf, bufs, S)
        gelu, gelu_grad = _gelu_and_grad(ug_ref[...])
        dzl_t = dzl_ref[...]
        dug_ref[...] = (dzl_t * h * gelu_grad).astype(MM)
        a_next = _shifted(sbuf, a, -1, S)
        lam_t = _scan(a_next, dzl_t * gelu, bufs, S, reverse=True)
        da = lam_t * _shifted(sbuf, h, 1, S)
        d_iv = lam_t * mult
        d_log_a = da * a - (lam_t * (ig * v)) * (a * a) / mult
        dlam_ref[...] = jnp.sum(d_log_a * r, axis=0, keepdims=True) * (LRU_C * jax.nn.sigmoid(-lam))
        dra = (d_log_a * (-LRU_C * sp)) * (r * (1.0 - r))
        dia = (d_iv * v) * (ig * (1.0 - ig))
        dba_ref[...] = jnp.sum(dra, axis=0, keepdims=True)
        dbx_ref[...] = jnp.sum(dia, axis=0, keepdims=True)
        dwa_ref[...] = _dot_tn(v, dra)
        dwx_ref[...] = _dot_tn(v, dia)
        dv = d_iv * ig + _dot_nt(dra, wa) + _dot_nt(dia, wx)
        dcb_ref[...] = jnp.sum(dv, axis=0, keepdims=True)
        sbuf[pl.ds(SHIFT_PAD, S), :] = ul
        for k in range(CONV_WIDTH):
            dcw_ref[k:k + 1, :] = jnp.sum(dv * sbuf[pl.ds(SHIFT_PAD - (CONV_WIDTH - 1 - k), S), :], axis=0, keepdims=True)
        sbuf[pl.ds(SHIFT_PAD, S), :] = dv
        dul = None
        for k in range(CONV_WIDTH):
            term = sbuf[pl.ds(SHIFT_PAD + (CONV_WIDTH - 1 - k), S), :] * cw[k:k + 1, :]
            dul = term if dul is None else dul + term
        dul_ref[...] = dul.astype(MM)

    head_mat = pl.BlockSpec((None, LANES, LANES), lambda h: (h, 0, 0))
    head_vec = pl.BlockSpec((None, 1, LANES), lambda h: (h, 0, 0))
    col = pl.BlockSpec((S, LANES), lambda h: (0, h))
    row = pl.BlockSpec((1, LANES), lambda h: (0, h))
    return pl.pallas_call(
        body, name=name, grid=(H,),
        in_specs=_lru_specs(S, H, lw, pw) + [col],
        out_specs=[col, col, pl.BlockSpec((CONV_WIDTH, LANES), lambda h: (0, h)), row, head_mat, head_vec, head_mat,
                   head_vec, row],
        out_shape=[jax.ShapeDtypeStruct((S, lw), MM), jax.ShapeDtypeStruct((S, lw), MM),
                   jax.ShapeDtypeStruct((CONV_WIDTH, lw), F32), jax.ShapeDtypeStruct((1, lw), F32),
                   jax.ShapeDtypeStruct((H, LANES, LANES), F32), jax.ShapeDtypeStruct((H, 1, LANES), F32),
                   jax.ShapeDtypeStruct((H, LANES, LANES), F32), jax.ShapeDtypeStruct((H, 1, LANES), F32),
                   jax.ShapeDtypeStruct((1, lw), F32)],
        scratch_shapes=_lru_scratch(S),
        compiler_params=_params(("arbitrary",)),
    )(proj, proj, conv_w, conv_b, wa, ba, wx, bx, lam, dzl)


def _loss_head(x, gain, target, name):
    T, D = x.shape
    tm = min(TOKEN_TILE, T)
    ni = T // tm

    def body(x_ref, g_ref, t_ref, loss_ref, dx_ref, dg_ref, loss_acc, dg_acc):
        i = pl.program_id(0)
        g = g_ref[...]
        r, xh, y = _rms(x_ref[...], g)
        err = y - t_ref[...]
        part = 0.5 * jnp.sum(jnp.mean(err * err, axis=-1, keepdims=True), axis=0, keepdims=True)
        _accumulate(loss_acc, jnp.broadcast_to(part, (1, LANES)), i == 0)
        dy = err * (1.0 / D)
        _accumulate(dg_acc, jnp.sum(dy * xh, axis=0, keepdims=True), i == 0)
        dx_ref[...] = _rms_bwd(dy, xh, r, g)

        @pl.when(i == ni - 1)
        def _():
            loss_ref[...] = loss_acc[...]
            dg_ref[...] = dg_acc[...]

    return pl.pallas_call(
        body, name=name, grid=(ni,),
        in_specs=[pl.BlockSpec((tm, D), lambda i: (i, 0)), pl.BlockSpec((1, D), lambda i: (0, 0)),
                  pl.BlockSpec((tm, D), lambda i: (i, 0))],
        out_specs=[pl.BlockSpec((1, LANES), lambda i: (0, 0)), pl.BlockSpec((tm, D), lambda i: (i, 0)),
                   pl.BlockSpec((1, D), lambda i: (0, 0))],
        out_shape=[jax.ShapeDtypeStruct((1, LANES), F32), jax.ShapeDtypeStruct((T, D), F32),
                   jax.ShapeDtypeStruct((1, D), F32)],
        scratch_shapes=[pltpu.VMEM((1, LANES), F32), pltpu.VMEM((1, D), F32)],
        compiler_params=_params(("arbitrary",)),
    )(x, gain, target)


def _my_core():
    return lax.axis_index("c")


def _my_chip():
    return 2 * lax.axis_index("x") + lax.axis_index("y")


def _pair_sum(a, recv, xfer, name):
    n, R, C = a.shape
    hr = R // 2

    def body(a_ref, r_ref, pf_ref, pb_ref):
        s = a_ref[...] + r_ref[...]
        pf_ref[...] = s
        pb_ref[...] = s.astype(xfer)

    piece = pl.BlockSpec((None, hr, C), lambda k: (k, 0, 0))
    return pl.pallas_call(
        body, name=name, grid=(n,),
        in_specs=[pl.BlockSpec((None, hr, C), lambda k: (k, _my_core(), 0)), piece],
        out_specs=[piece, piece],
        out_shape=[jax.ShapeDtypeStruct((n, hr, C), F32), jax.ShapeDtypeStruct((n, hr, C), xfer)],
        compiler_params=_params(("arbitrary",)),
    )(a, recv)


def _chip_sum(pf, others, name):
    _, hr, C = pf.shape
    rb = _row_block(hr, C)
    nb = hr // rb

    def body(o_ref, q_ref, g_ref):
        g_ref[...] = ((o_ref[...] + q_ref[0].astype(F32)) + q_ref[1].astype(F32)) + q_ref[2].astype(F32)

    return pl.pallas_call(
        body, name=name, grid=(nb,),
        in_specs=[pl.BlockSpec((None, rb, C), lambda i: (_my_chip(), i, 0)),
                  pl.BlockSpec((N_CHIPS - 1, rb, C), lambda i: (0, i, 0))],
        out_specs=pl.BlockSpec((rb, C), lambda i: (_my_core() * nb + i, 0)),
        out_shape=jax.ShapeDtypeStruct((2 * hr, C), F32),
        compiler_params=_params(("arbitrary",)),
    )(pf, others)


def _adamw_step(w_ref, g_ref, m_ref, v_ref, go_ref, d_ref, mo_ref, vo_ref):
    g_t = g_ref[...]
    m_t = ADAM_B1 * m_ref[...] + (1.0 - ADAM_B1) * g_t
    v_t = ADAM_B2 * v_ref[...] + (1.0 - ADAM_B2) * (g_t * g_t)
    m_hat = m_t / (1.0 - ADAM_B1 ** ADAM_STEP)
    v_hat = v_t / (1.0 - ADAM_B2 ** ADAM_STEP)
    go_ref[...] = g_t
    d_ref[...] = -ADAM_LR * (m_hat / (jnp.sqrt(v_hat) + ADAM_EPS) + ADAM_WD * w_ref[...])
    mo_ref[...] = m_t
    vo_ref[...] = v_t


def _adamw(w, g, m, v, name):
    R, C = w.shape
    rb = _row_block(R, C)
    blk = pl.BlockSpec((rb, C), lambda i: (i, 0))
    return pl.pallas_call(
        functools.partial(_adamw_step), name=name, grid=(R // rb,), in_specs=[blk] * 4, out_specs=[blk] * 4,
        out_shape=[jax.ShapeDtypeStruct((R, C), F32)] * 4,
        compiler_params=_params(("arbitrary",)),
    )(w, g, m, v)


def _adamw_layer(w, g, m, v, outs, layer, name):
    L, R, C = w.shape
    rb = _row_block(R, C)

    def body(w_ref, g_ref, m_ref, v_ref, *rest):
        _adamw_step(w_ref, g_ref, m_ref, v_ref, *rest[4:])

    stacked = pl.BlockSpec((None, rb, C), lambda i: (layer, i, 0))
    return pl.pallas_call(
        body, name=name, grid=(R // rb,),
        in_specs=[stacked, pl.BlockSpec((rb, C), lambda i: (i, 0)), stacked, stacked] + [ANY] * 4,
        out_specs=[stacked] * 4,
        out_shape=[jax.ShapeDtypeStruct((L, R, C), F32)] * 4,
        input_output_aliases={4 + j: j for j in range(4)},
        compiler_params=_params(("arbitrary",)),
    )(w, g, m, v, *outs)


def _place():
    x, y, c = lax.axis_index("x"), lax.axis_index("y"), lax.axis_index("c")
    others = [(1 - x, y), (x, 1 - y), (1 - x, 1 - y)]
    return x, y, c, 2 * x + y, others


def _half(c, rows):
    return pl.ds(pl.multiple_of(c * (rows // 2), 16), rows // 2)


HBM_SPEC = pl.BlockSpec(memory_space=pltpu.HBM)
SEM_SPEC = pl.BlockSpec(memory_space=pltpu.SEMAPHORE)
SPLIT = pltpu.CompilerParams(has_side_effects=pltpu.SideEffectType.DATAFLOW_SIDE_EFFECTING)
TOKEN = jax.ShapeDtypeStruct((8, LANES), F32)


def _in_hbm(a):
    return pltpu.with_memory_space_constraint(a, pltpu.HBM)


def _split_start(build, count, srcs, lands, after, name):
    ns, nl = len(srcs), len(lands)

    def body(*refs):
        for started, _ in build(refs[:ns], refs[ns:ns + nl], refs[ns + nl + 1], refs[ns + nl + 2]):
            started.start()
        refs[-1][...] = jnp.zeros_like(refs[-1])

    res = pl.pallas_call(
        body, name=name,
        in_specs=[HBM_SPEC] * (ns + nl) + [ANY],
        out_specs=[SEM_SPEC] * 2 + [HBM_SPEC] * nl + [pl.BlockSpec(memory_space=pltpu.VMEM)],
        out_shape=[pltpu.SemaphoreType.DMA((count,))] * 2 + [pltpu.HBM(z.shape, z.dtype) for z in lands] + [TOKEN],
        input_output_aliases={ns + a: 2 + a for a in range(nl)},
        compiler_params=SPLIT,
    )(*[_in_hbm(s) for s in srcs], *[_in_hbm(z) for z in lands], after)
    return res[:2], res[2:2 + nl], res[-1]


def _split_wait(build, srcs, lands, sems, after, name):
    ns, nl = len(srcs), len(lands)

    def body(*refs):
        for started, arriving in build(refs[:ns], refs[ns:ns + nl], refs[ns + nl], refs[ns + nl + 1]):
            started.wait_send()
            arriving.wait_recv()

    return pl.pallas_call(
        body, name=name,
        in_specs=[HBM_SPEC] * (ns + nl) + [SEM_SPEC] * 2 + [ANY],
        out_specs=[HBM_SPEC] * nl,
        out_shape=[pltpu.HBM(z.shape, z.dtype) for z in lands],
        input_output_aliases={ns + a: a for a in range(nl)},
        compiler_params=SPLIT,
    )(*[_in_hbm(s) for s in srcs], *lands, *sems, after)


def _gather_copies(srcs, lands, send, recv, layer):
    x, y, c, k, others = _place()
    pairs = []
    for a in range(len(srcs)):
        rows = _half(c, srcs[a].shape[-2])
        for r, (px, py) in enumerate(others):
            def ici(slot, a=a, r=r, px=px, py=py, rows=rows):
                return pltpu.make_async_remote_copy(src_ref=srcs[a].at[layer].at[rows], dst_ref=lands[a].at[slot].at[rows],
                                                    send_sem=send.at[a * 4 + r], recv_sem=recv.at[a * 4 + r],
                                                    device_id=(px, py, c), device_id_type=MESH)
            pairs.append((ici(k), ici(2 * px + py)))
        own = pltpu.make_async_remote_copy(src_ref=srcs[a].at[layer], dst_ref=lands[a].at[k], send_sem=send.at[a * 4 + 3],
                                           recv_sem=recv.at[a * 4 + 3], device_id=(x, y, 1 - c), device_id_type=MESH)
        pairs.append((own, own))
    return pairs


def _halves_copies(srcs, lands, send, recv):
    x, y, c, _, _ = _place()
    pairs = []
    for a in range(len(srcs)):
        cp = pltpu.make_async_remote_copy(src_ref=srcs[a].at[:, _half(1 - c, srcs[a].shape[1]), :], dst_ref=lands[a],
                                          send_sem=send.at[a], recv_sem=recv.at[a], device_id=(x, y, 1 - c),
                                          device_id_type=MESH)
        pairs.append((cp, cp))
    return pairs


def _chip_copies(srcs, lands, send, recv):
    x, y, c, k, others = _place()
    pairs = []
    for a in range(len(srcs)):
        for r, (px, py) in enumerate(others):
            cp = pltpu.make_async_remote_copy(src_ref=srcs[a].at[2 * px + py], dst_ref=lands[a].at[r],
                                              send_sem=send.at[a * 3 + r], recv_sem=recv.at[a * 3 + r],
                                              device_id=(px, py, c), device_id_type=MESH)
            pairs.append((cp, cp))
    return pairs


def _forward_halves(lands, name):
    n = len(lands)

    def body(*refs):
        outs = refs[n:2 * n]
        send, recv = refs[2 * n:]
        x, y, c, _, others = _place()

        def copy(a, r, half):
            px, py = others[r]
            rows = outs[a].at[2 * px + py].at[_half(half, outs[a].shape[-2])]
            return pltpu.make_async_remote_copy(src_ref=rows, dst_ref=rows, send_sem=send.at[a * 3 + r],
                                                recv_sem=recv.at[a * 3 + r], device_id=(x, y, 1 - c), device_id_type=MESH)

        every = [(a, r) for a in range(n) for r in range(3)]
        for a, r in every:
            copy(a, r, c).start()
        for a, r in every:
            copy(a, r, 1 - c).wait_recv()
        for a, r in every:
            copy(a, r, c).wait_send()

    return pl.pallas_call(
        body, name=name, in_specs=[ANY] * n, out_specs=[ANY] * n,
        out_shape=[jax.ShapeDtypeStruct(z.shape, z.dtype) for z in lands],
        input_output_aliases={a: a for a in range(n)},
        scratch_shapes=[pltpu.SemaphoreType.DMA((3 * n,))] * 2,
    )(*lands)


def _gather_small(small, name):
    def body(in_ref, out_ref, send, recv):
        x, y, c, k, others = _place()
        mine = pltpu.make_async_remote_copy(src_ref=in_ref, dst_ref=out_ref.at[k], send_sem=send.at[3], recv_sem=recv.at[3],
                                            device_id=(x, y, 1 - c), device_id_type=MESH)
        mine.start()
        copies = []
        for r, (px, py) in enumerate(others):
            copies.append(pltpu.make_async_remote_copy(src_ref=in_ref, dst_ref=out_ref.at[k], send_sem=send.at[r],
                                                       recv_sem=recv.at[r], device_id=(px, py, c), device_id_type=MESH))
            copies[-1].start()
        for r, (px, py) in enumerate(others):
            pltpu.make_async_remote_copy(src_ref=in_ref, dst_ref=out_ref.at[2 * px + py], send_sem=send.at[r],
                                         recv_sem=recv.at[r], device_id=(px, py, c), device_id_type=MESH).wait_recv()
        for cp in copies:
            cp.wait_send()
        mine.wait()

    return pl.pallas_call(
        body, name=name, in_specs=[ANY], out_specs=ANY,
        out_shape=jax.ShapeDtypeStruct((N_CHIPS,) + small.shape, small.dtype),
        scratch_shapes=[pltpu.SemaphoreType.DMA((4,))] * 2,
    )(small)


def _join_halves(fulls, name):
    n = len(fulls)

    def body(*refs):
        outs = refs[n:2 * n]
        send, recv = refs[2 * n:]
        x, y, c, _, _ = _place()

        def copy(a, half):
            rows = outs[a].at[_half(half, outs[a].shape[0]), :]
            return pltpu.make_async_remote_copy(src_ref=rows, dst_ref=rows, send_sem=send.at[a], recv_sem=recv.at[a],
                                                device_id=(x, y, 1 - c), device_id_type=MESH)

        for a in range(n):
            copy(a, c).start()
        for a in range(n):
            copy(a, 1 - c).wait_recv()
        for a in range(n):
            copy(a, c).wait_send()

    return pl.pallas_call(
        body, name=name, in_specs=[ANY] * n, out_specs=[ANY] * n,
        out_shape=[jax.ShapeDtypeStruct(f.shape, f.dtype) for f in fulls],
        input_output_aliases={a: a for a in range(n)},
        scratch_shapes=[pltpu.SemaphoreType.DMA((n,))] * 2,
    )(*fulls)


GROUPS = (("ffn1_w_up", "ffn1_w_down"), ("w_in", "w_pool_up", "w_lru_up", "w_out"), ("ffn2_w_up", "ffn2_w_down"))
BIG = GROUPS[0] + GROUPS[1] + GROUPS[2]
SMALL_LAYER = ("norm_ffn1", "norm_mix", "pool_w", "pool_b", "pool_scale", "conv_w", "conv_b", "lru_w_a", "lru_b_a",
               "lru_w_x", "lru_b_x", "lru_lambda", "norm_ffn2")
SMALL = SMALL_LAYER + ("final_norm",)
GATHER_AHEAD = 3
WEIGHTS = ("norm_ffn1", "ffn1_w_up", "ffn1_w_down", "norm_mix", "w_in", "pool_w", "pool_b", "pool_scale", "w_pool_up",
           "conv_w", "conv_b", "lru_w_a", "lru_b_a", "lru_w_x", "lru_b_x", "lru_lambda", "w_lru_up", "w_out", "norm_ffn2",
           "ffn2_w_up", "ffn2_w_down", "final_norm")


def _pack(arrays, rows_multiple):
    flat = jnp.concatenate([a.reshape(-1) for a in arrays])
    rows = -(-flat.shape[0] // LANES)
    rows = -(-rows // rows_multiple) * rows_multiple
    return jnp.pad(flat, (0, rows * LANES - flat.shape[0])).reshape(rows, LANES)


def _unpack(packed, like):
    flat, out, at = packed.reshape(-1), [], 0
    for a in like:
        out.append(flat[at:at + a.size].reshape(a.shape))
        at += a.size
    return out


def kernel(x, norm_ffn1, ffn1_w_up, ffn1_w_down, norm_mix, w_in, pool_w, pool_b, pool_scale, w_pool_up, conv_w, conv_b, lru_w_a, lru_b_a, lru_w_x, lru_b_x, lru_lambda, w_lru_up, w_out, norm_ffn2, ffn2_w_up, ffn2_w_down, final_norm, loss_target, m_norm_ffn1, m_ffn1_w_up, m_ffn1_w_down, m_norm_mix, m_w_in, m_pool_w, m_pool_b, m_pool_scale, m_w_pool_up, m_conv_w, m_conv_b, m_lru_w_a, m_lru_b_a, m_lru_w_x, m_lru_b_x, m_lru_lambda, m_w_lru_up, m_w_out, m_norm_ffn2, m_ffn2_w_up, m_ffn2_w_down, m_final_norm, v_norm_ffn1, v_ffn1_w_up, v_ffn1_w_down, v_norm_mix, v_w_in, v_pool_w, v_pool_b, v_pool_scale, v_w_pool_up, v_conv_w, v_conv_b, v_lru_w_a, v_lru_b_a, v_lru_w_x, v_lru_b_x, v_lru_lambda, v_w_lru_up, v_w_out, v_norm_ffn2, v_ffn2_w_up, v_ffn2_w_down, v_final_norm):
    given = dict(locals())
    W = {n: given[n] for n in WEIGHTS}
    M = {n: given["m_" + n] for n in WEIGHTS}
    V = {n: given["v_" + n] for n in WEIGHTS}
    L = norm_ffn1.shape[0]
    T, D = x.shape[1], x.shape[2]
    G, H = pool_w.shape[1], lru_w_a.shape[1]
    pw, lw = G * LANES, H * LANES
    chip = 2 * lax.axis_index("x") + lax.axis_index("y")

    def tied(gain, token):
        return gain if token is None else gain + token[:1, :1]

    shards = {n: W[n].astype(MM) for n in BIG}
    conv_full = _gather_small(conv_w.reshape(L * CONV_WIDTH, conv_w.shape[2]), "gather_conv_w")
    conv_full = conv_full.reshape(N_CHIPS, L, CONV_WIDTH, -1).transpose(1, 2, 0, 3).reshape(L, CONV_WIDTH, lw)

    order = [(l, g) for l in range(L) for g in range(len(GROUPS))]
    started = {}

    def start_gather(i, after):
        l, g = order[i]
        srcs = [shards[n] for n in GROUPS[g]]
        lands = [lax.empty((N_CHIPS,) + s.shape[1:], s.dtype) for s in srcs]
        build = functools.partial(_gather_copies, layer=l)
        started[i] = (build, srcs) + _split_start(build, 4 * len(srcs), srcs, lands, after, f"gather_start_{l}_{g}")
        return started[i][-1]

    def finish_gather(i, after):
        l, g = order[i]
        build, srcs, sems, lands, _ = started.pop(i)
        lands = _split_wait(build, srcs, lands, sems, after, f"gather_wait_{l}_{g}")
        got = _forward_halves(lands, f"forward_halves_{l}_{g}")
        token = start_gather(i + GATHER_AHEAD, got[0]) if i + GATHER_AHEAD < len(order) else None
        return got, token

    token = conv_full
    for i in range(min(GATHER_AHEAD, len(order))):
        token = start_gather(i, token)

    xs = x.reshape(T, D)
    saved = []
    for l in range(L):
        small = dict(pool_w=pool_w[l], pool_b=pool_b[l].reshape(G, 1, LANES), pool_scale=pool_scale[l].reshape(1, pw),
                     conv_w=conv_full[l], conv_b=conv_b[l].reshape(1, lw), wa=lru_w_a[l],
                     ba=lru_b_a[l].reshape(H, 1, LANES), wx=lru_w_x[l], bx=lru_b_x[l].reshape(H, 1, LANES),
                     lam=lru_lambda[l].reshape(1, lw))
        g1, g2, g3 = norm_ffn1[l].reshape(1, D), norm_mix[l].reshape(1, D), norm_ffn2[l].reshape(1, D)

        x0 = xs
        (w_up1, w_dn1), token = finish_gather(3 * l, token if l == 0 else x0)
        w_dn1 = w_dn1.reshape(-1, D)
        u1 = _norm_matmul_fwd(x0, tied(g1, token), w_up1, f"ffn1_up_{l}")
        x1 = _swiglu_down_fwd(u1, x0, w_dn1, f"ffn1_down_{l}")
        (w_i, w_pu, w_lu, w_o), token = finish_gather(3 * l + 1, x1)
        w_lu, w_o = w_lu.reshape(-1, D), w_o.reshape(-1, D)
        proj = _norm_matmul_fwd(x1, tied(g2, token), w_i, f"mix_in_{l}")
        pm = _pool_fwd(proj, small["pool_w"], small["pool_b"], small["pool_scale"], f"pool_{l}")
        zl = _lru_fwd(proj, small["conv_w"], small["conv_b"], small["wa"], small["ba"], small["wx"], small["bx"],
                      small["lam"], pw, f"lru_{l}")
        x2 = _mix_out_fwd(x1, proj, pm, zl, w_pu, w_lu, w_o, f"mix_out_{l}")
        (w_up2, w_dn2), token = finish_gather(3 * l + 2, x2)
        w_dn2 = w_dn2.reshape(-1, D)
        u2 = _norm_matmul_fwd(x2, tied(g3, token), w_up2, f"ffn2_up_{l}")
        xs = _swiglu_down_fwd(u2, x2, w_dn2, f"ffn2_down_{l}")
        saved.append(dict(x0=x0, u1=u1, x1=x1, proj=proj, pm=pm, zl=zl, x2=x2, u2=u2, small=small, g=(g1, g2, g3),
                          w=(w_up1, w_dn1, w_i, w_pu, w_lu, w_o, w_up2, w_dn2)))

    loss_part, dx, d_final = _loss_head(xs, final_norm.reshape(1, D), loss_target.reshape(T, D), "loss_head")
    loss = lax.psum(loss_part[0, 0], ("x", "y", "c"))

    du_spec_ffn = lambda tm, tn: pl.BlockSpec((None, tm, tn), lambda k, i: (k // 2, i, k % 2))
    du_spec_mix = lambda tm, tn: pl.BlockSpec((tm, tn), lambda k, i: (i, k))
    out = {n: [lax.empty(W[n].shape, F32) for _ in range(4)] for n in BIG}
    small_pieces = [None] * L
    stages = [None, None]

    def advance(new):
        token = None
        first = None
        if new is not None:
            l, names, grads = new
            tag = f"{names[0]}_{l}"
            lands = [lax.empty((N_CHIPS, a.shape[1] // 2, a.shape[2]), F32) for a in grads]
            sems, lands, token = _split_start(_halves_copies, len(grads), grads, lands, grads[0], f"halves_start_{tag}")
            first = (l, names, tag, grads, sems, lands)
        second = None
        if stages[0] is not None:
            l, names, tag, grads, sems, lands = stages[0]
            recv = _split_wait(_halves_copies, grads, lands, sems, grads[0] if token is None else token,
                               f"halves_wait_{tag}")
            sums = [_pair_sum(g, r, F32 if n == "small" else XFER, f"pair_sum_{n}_{l}")
                    for g, r, n in zip(grads, recv, names)]
            pbs = [p[1] for p in sums]
            lands = [lax.empty((N_CHIPS - 1,) + p.shape[1:], p.dtype) for p in pbs]
            sems, lands, token = _split_start(_chip_copies, 3 * len(pbs), pbs, lands, pbs[0], f"chips_start_{tag}")
            second = (l, names, tag, [p[0] for p in sums], pbs, sems, lands)
        if stages[1] is not None:
            l, names, tag, pfs, pbs, sems, lands = stages[1]
            got = _split_wait(_chip_copies, pbs, lands, sems, pbs[0] if token is None else token, f"chips_wait_{tag}")
            whole = _join_halves([_chip_sum(p, q, f"chip_sum_{n}_{l}") for p, q, n in zip(pfs, got, names)],
                                 f"join_halves_{tag}")
            for n, g in zip(names, whole):
                if n == "small":
                    small_pieces[l] = g
                else:
                    out[n] = _adamw_layer(W[n], g, M[n], V[n], out[n], l, f"adamw_{n}_{l}")
        stages[:] = [first, second]
        return token

    token = None
    for l in reversed(range(L)):
        s = saved[l]
        w_up1, w_dn1, w_i, w_pu, w_lu, w_o, w_up2, w_dn2 = s["w"]
        g1, g2, g3 = s["g"]
        sm = s["small"]
        du2, d_dn2 = _swiglu_down_bwd(s["u2"], dx, w_dn2, f"ffn2_down_bwd_{l}")
        dx, d_up2, dg3 = _norm_matmul_bwd(s["x2"], tied(g3, token), du2, du_spec_ffn, w_up2, dx, f"ffn2_up_bwd_{l}")
        token = advance((l, GROUPS[2], [d_up2, d_dn2.reshape(N_CHIPS, -1, D)]))
        dgl, dpm, dzl, d_o, d_pu, d_lu = _mix_out_bwd(dx, s["proj"], s["pm"], s["zl"], w_pu, w_lu, w_o, f"mix_out_bwd_{l}")
        du_pool, d_pool_w, d_pool_b, d_pool_scale = _pool_bwd(s["proj"], dpm, sm["pool_w"], sm["pool_b"], sm["pool_scale"],
                                                            f"pool_bwd_{l}")
        dul, dug, d_cw, d_cb, d_wa, d_ba, d_wx, d_bx, d_lam = _lru_bwd(
            s["proj"], dzl, sm["conv_w"], sm["conv_b"], sm["wa"], sm["ba"], sm["wx"], sm["bx"], sm["lam"], pw, f"lru_bwd_{l}")
        dproj = jnp.concatenate([du_pool, dul, dug, dgl], axis=1)
        dx, d_in, dg2 = _norm_matmul_bwd(s["x1"], tied(g2, token), dproj, du_spec_mix, w_i, dx, f"mix_in_bwd_{l}")
        token = advance((l, GROUPS[1], [d_in, d_pu, d_lu.reshape(N_CHIPS, -1, D), d_o.reshape(N_CHIPS, -1, D)]))
        du1, d_dn1 = _swiglu_down_bwd(s["u1"], dx, w_dn1, f"ffn1_down_bwd_{l}")
        dx, d_up1, dg1 = _norm_matmul_bwd(s["x0"], tied(g1, token), du1, du_spec_ffn, w_up1, dx, f"ffn1_up_bwd_{l}")
        small = [dg1, dg2, d_pool_w, d_pool_b, d_pool_scale, d_cw, d_cb, d_wa, d_ba, d_wx, d_bx, d_lam, dg3]
        small = _pack(small + ([d_final] if l == L - 1 else []), 2 * 16 * N_CHIPS).reshape(N_CHIPS, -1, LANES)
        token = advance((l, GROUPS[0] + ("small",), [d_up1, d_dn1.reshape(N_CHIPS, -1, D), small]))
    while stages[0] is not None or stages[1] is not None:
        advance(None)

    pieces = _gather_small(jnp.concatenate(small_pieces), "gather_small_grads")
    pieces = pieces.reshape(N_CHIPS, L, -1)
    like = {n: jax.ShapeDtypeStruct(W[n].shape[1:] if n != "conv_w" else (CONV_WIDTH, lw), F32) for n in SMALL_LAYER}
    per_layer = []
    for l in range(L):
        names = SMALL_LAYER + (("final_norm",) if l == L - 1 else ())
        shapes = [like[n] if n in like else jax.ShapeDtypeStruct((D,), F32) for n in names]
        per_layer.append(dict(zip(names, _unpack(pieces[:, l], shapes))))
    small_sum = {n: jnp.stack([per_layer[l][n] for l in range(L)]) for n in SMALL_LAYER}
    small_sum["final_norm"] = per_layer[L - 1]["final_norm"]
    cs = conv_w.shape[2]
    small_sum["conv_w"] = lax.dynamic_slice_in_dim(small_sum["conv_w"], chip * cs, cs, axis=2)
    packs = [_pack([d[n] for n in SMALL], 1024) for d in (W, small_sum, M, V)]
    res = _adamw(*packs, "adamw_small")
    like = [W[n] for n in SMALL]
    unpacked = [_unpack(r, like) for r in res]
    for j, n in enumerate(SMALL):
        out[n] = tuple(u[j] for u in unpacked)

    return (loss, dx.reshape(x.shape), *[out[n][0] for n in WEIGHTS], *[out[n][1] for n in WEIGHTS],
            *[out[n][2] for n in WEIGHTS], *[out[n][3] for n in WEIGHTS])
```

```python
import functools

import jax
import jax.numpy as jnp
from jax import lax
from jax.experimental import pallas as pl
from jax.experimental.pallas import tpu as pltpu

F32 = jnp.float32
MM = jnp.bfloat16
XFER = jnp.bfloat16

EPS = 1e-6
LRU_C = 8.0
POOL_WINDOWS = (2, 4, 8, 16)
CONV_WIDTH = 4
ADAM_LR, ADAM_B1, ADAM_B2, ADAM_EPS, ADAM_WD, ADAM_STEP = 0.001, 0.9, 0.999, 1e-08, 0.01, 10

N_CHIPS = 4
LANES = 128
SHIFT_PAD = 8
TOKEN_TILE = 512
VMEM_LIMIT = 60 * 1024 * 1024
MESH = pl.DeviceIdType.MESH
ANY = pl.BlockSpec(memory_space=pl.ANY)


def _params(sem=None):
    return pltpu.CompilerParams(dimension_semantics=sem, vmem_limit_bytes=VMEM_LIMIT)


def _dot(a, b):
    return jnp.dot(a.astype(MM), b.astype(MM), preferred_element_type=F32)


def _dot_nt(a, b):
    return lax.dot_general(a.astype(MM), b.astype(MM), (((1,), (1,)), ((), ())), preferred_element_type=F32)


def _dot_tn(a, b):
    return lax.dot_general(a.astype(MM), b.astype(MM), (((0,), (0,)), ((), ())), preferred_element_type=F32)


def _rms(x, g):
    r = lax.rsqrt(jnp.mean(x * x, axis=-1, keepdims=True) + EPS)
    xh = x * r
    return r, xh, xh * g


def _rms_bwd(dh, xh, r, g):
    dxh = dh * g
    return r * (dxh - xh * jnp.mean(dxh * xh, axis=-1, keepdims=True))


def _accumulate(ref, val, first):
    @pl.when(first)
    def _():
        ref[...] = val

    @pl.when(jnp.logical_not(first))
    def _():
        ref[...] += val


def _row_block(rows, cols, itemsize=4, budget=1 << 20):
    best = None
    for rb in range(16, rows + 1, 16):
        if rows % rb == 0 and rb * cols * itemsize <= budget:
            best = rb
    return best if best is not None else rows


def _norm_matmul_fwd(x, gain, w, name):
    T, D = x.shape
    K, _, tn = w.shape
    tm = min(TOKEN_TILE, T)

    def body(x_ref, g_ref, w_ref, u_ref):
        _, _, h = _rms(x_ref[...], g_ref[...])
        u_ref[...] = _dot(h, w_ref[...])

    return pl.pallas_call(
        body, name=name, grid=(K, T // tm),
        in_specs=[pl.BlockSpec((tm, D), lambda k, i: (i, 0)), pl.BlockSpec((1, D), lambda k, i: (0, 0)),
                  pl.BlockSpec((None, D, tn), lambda k, i: (k, 0, 0))],
        out_specs=pl.BlockSpec((tm, tn), lambda k, i: (i, k)),
        out_shape=jax.ShapeDtypeStruct((T, K * tn), F32),
        compiler_params=_params(("arbitrary", "arbitrary")),
    )(x, gain, w)


def _norm_matmul_bwd(x, gain, du, du_spec, w, dres, name):
    T, D = x.shape
    K, _, tn = w.shape
    tm = min(TOKEN_TILE, T)
    ni = T // tm

    def body(x_ref, g_ref, du_ref, w_ref, dres_ref, dx_ref, dw_ref, dg_ref, dh_acc, dg_acc):
        k, i = pl.program_id(0), pl.program_id(1)
        g = g_ref[...]
        r, xh, h = _rms(x_ref[...], g)
        du_t = du_ref[...].astype(MM)
        rows = pl.ds(pl.multiple_of(i * tm, tm), tm)
        part = _dot_nt(du_t, w_ref[...])

        @pl.when(k == 0)
        def _():
            dh_acc[rows, :] = part

        @pl.when(k > 0)
        def _():
            dh_acc[rows, :] += part

        _accumulate(dw_ref, _dot_tn(h, du_t), i == 0)

        @pl.when(k == K - 1)
        def _():
            dh = dh_acc[rows, :]
            _accumulate(dg_acc, jnp.sum(dh * xh, axis=0, keepdims=True), i == 0)
            dx_ref[...] = dres_ref[...] + _rms_bwd(dh, xh, r, g)

            @pl.when(i == ni - 1)
            def _():
                dg_ref[...] = dg_acc[...]

    def last(k, i):
        return (jnp.where(k == K - 1, i, 0), 0)

    return pl.pallas_call(
        body, name=name, grid=(K, ni),
        in_specs=[pl.BlockSpec((tm, D), lambda k, i: (i, 0)), pl.BlockSpec((1, D), lambda k, i: (0, 0)),
                  du_spec(tm, tn), pl.BlockSpec((None, D, tn), lambda k, i: (k, 0, 0)),
                  pl.BlockSpec((tm, D), last)],
        out_specs=[pl.BlockSpec((tm, D), last), pl.BlockSpec((None, D, tn), lambda k, i: (k, 0, 0)),
                   pl.BlockSpec((1, D), lambda k, i: (0, 0))],
        out_shape=[jax.ShapeDtypeStruct((T, D), F32), jax.ShapeDtypeStruct((K, D, tn), F32),
                   jax.ShapeDtypeStruct((1, D), F32)],
        scratch_shapes=[pltpu.VMEM((T, D), F32), pltpu.VMEM((1, D), F32)],
        compiler_params=_params(("arbitrary", "arbitrary")),
    )(x, gain, du, w, dres)


def _swiglu_down_fwd(u, x, wd, name):
    T, D = x.shape
    Fh = wd.shape[0]
    tm = min(TOKEN_TILE, T)

    def body(a_ref, b_ref, x_ref, wd_ref, o_ref):
        a = a_ref[...]
        s = a * jax.nn.sigmoid(a) * b_ref[...]
        o_ref[...] = x_ref[...] + 0.5 * _dot(s, wd_ref[...])

    return pl.pallas_call(
        body, name=name, grid=(T // tm,),
        in_specs=[pl.BlockSpec((tm, Fh), lambda i: (i, 0)), pl.BlockSpec((tm, Fh), lambda i: (i, 1)),
                  pl.BlockSpec((tm, D), lambda i: (i, 0)), pl.BlockSpec((Fh, D), lambda i: (0, 0))],
        out_specs=pl.BlockSpec((tm, D), lambda i: (i, 0)),
        out_shape=jax.ShapeDtypeStruct((T, D), F32),
        compiler_params=_params(("arbitrary",)),
    )(u, u, x, wd)


def _swiglu_down_bwd(u, dxn, wd, name):
    T, D = dxn.shape
    Fh = wd.shape[0]
    tm = min(TOKEN_TILE, T)
    nj = 2 if Fh % (2 * LANES) == 0 else 1
    tf = Fh // nj

    def body(a_ref, b_ref, dxn_ref, wd_ref, du_ref, dwd_ref):
        i = pl.program_id(1)
        a, b = a_ref[...], b_ref[...]
        dyh = (0.5 * dxn_ref[...]).astype(MM)
        ds = _dot_nt(dyh, wd_ref[...])
        sig = jax.nn.sigmoid(a)
        sa = a * sig
        _accumulate(dwd_ref, _dot_tn(sa * b, dyh), i == 0)
        du_ref[0] = (ds * b * (sig * (1.0 + a * (1.0 - sig)))).astype(MM)
        du_ref[1] = (ds * sa).astype(MM)

    return pl.pallas_call(
        body, name=name, grid=(nj, T // tm),
        in_specs=[pl.BlockSpec((tm, tf), lambda j, i: (i, j)), pl.BlockSpec((tm, tf), lambda j, i: (i, j + nj)),
                  pl.BlockSpec((tm, D), lambda j, i: (i, 0)), pl.BlockSpec((tf, D), lambda j, i: (j, 0))],
        out_specs=[pl.BlockSpec((2, tm, tf), lambda j, i: (0, i, j)), pl.BlockSpec((tf, D), lambda j, i: (j, 0))],
        out_shape=[jax.ShapeDtypeStruct((2, T, Fh), MM), jax.ShapeDtypeStruct((Fh, D), F32)],
        compiler_params=_params(("arbitrary", "arbitrary")),
    )(u, u, dxn, wd)


def _mix_branches(pm, zl, gp_logit, gr_logit, wpu_ref, wlu_ref):
    y_pool = jnp.concatenate([_dot(pm, wpu_ref[k]) for k in range(N_CHIPS)], axis=1)
    y_lru = _dot(zl, wlu_ref[...])
    return y_pool, y_lru, jax.nn.sigmoid(gp_logit), jax.nn.sigmoid(gr_logit)


def _mix_out_specs(tm, D, pw, lw, cs):
    gate0 = (pw + 2 * lw) // D
    return [pl.BlockSpec((tm, D), lambda i: (i, gate0)), pl.BlockSpec((tm, D), lambda i: (i, gate0 + 1)),
            pl.BlockSpec((tm, pw), lambda i: (i, 0)), pl.BlockSpec((tm, lw), lambda i: (i, 0)),
            pl.BlockSpec((N_CHIPS, pw, cs), lambda i: (0, 0, 0)), pl.BlockSpec((lw, D), lambda i: (0, 0)),
            pl.BlockSpec((D, D), lambda i: (0, 0))]


def _mix_out_fwd(x, proj, pm, zl, wpu, wlu, wo, name):
    T, D = x.shape
    pw, lw, cs = pm.shape[1], zl.shape[1], wpu.shape[2]
    assert (pw + 2 * lw) % D == 0
    tm = min(TOKEN_TILE, T)

    def body(x_ref, gp_ref, gr_ref, pm_ref, zl_ref, wpu_ref, wlu_ref, wo_ref, o_ref):
        y_pool, y_lru, gp, gr = _mix_branches(pm_ref[...], zl_ref[...], gp_ref[...], gr_ref[...], wpu_ref, wlu_ref)
        o_ref[...] = x_ref[...] + _dot(gp * y_pool + gr * y_lru, wo_ref[...])

    return pl.pallas_call(
        body, name=name, grid=(T // tm,),
        in_specs=[pl.BlockSpec((tm, D), lambda i: (i, 0))] + _mix_out_specs(tm, D, pw, lw, cs),
        out_specs=pl.BlockSpec((tm, D), lambda i: (i, 0)),
        out_shape=jax.ShapeDtypeStruct((T, D), F32),
        compiler_params=_params(("arbitrary",)),
    )(x, proj, proj, pm, zl, wpu, wlu, wo)


def _mix_out_bwd(dxn, proj, pm, zl, wpu, wlu, wo, name):
    T, D = dxn.shape
    pw, lw, cs = pm.shape[1], zl.shape[1], wpu.shape[2]
    tm = min(TOKEN_TILE // 2, T)
    ni = T // tm

    def body(dxn_ref, gp_ref, gr_ref, pm_ref, zl_ref, wpu_ref, wlu_ref, wo_ref,
             dgl_ref, dpm_ref, dzl_ref, dwo_hbm, dwpu_hbm, dwlu_hbm, acc_o, acc_pu, acc_lu, sem):
        i = pl.program_id(0)
        pm, zl = pm_ref[...], zl_ref[...]
        y_pool, y_lru, gp, gr = _mix_branches(pm, zl, gp_ref[...], gr_ref[...], wpu_ref, wlu_ref)
        dxn_t = dxn_ref[...].astype(MM)
        dmerged = _dot_nt(dxn_t, wo_ref[...])
        _accumulate(acc_o, _dot_tn(gp * y_pool + gr * y_lru, dxn_t), i == 0)
        dgl_ref[:, :D] = (dmerged * y_pool * (gp * (1.0 - gp))).astype(MM)
        dgl_ref[:, D:] = (dmerged * y_lru * (gr * (1.0 - gr))).astype(MM)
        dyp = (dmerged * gp).astype(MM)
        dyl = (dmerged * gr).astype(MM)
        dpm = None
        for k in range(N_CHIPS):
            dyp_k = dyp[:, k * cs:(k + 1) * cs]
            part = _dot_nt(dyp_k, wpu_ref[k])
            dpm = part if dpm is None else dpm + part
            _accumulate(acc_pu.at[k], _dot_tn(pm, dyp_k), i == 0)
        dpm_ref[...] = dpm
        dzl_ref[...] = _dot_nt(dyl, wlu_ref[...])
        _accumulate(acc_lu, _dot_tn(zl, dyl), i == 0)

        @pl.when(i == ni - 1)
        def _():
            copies = [pltpu.make_async_copy(acc_o, dwo_hbm, sem.at[0]), pltpu.make_async_copy(acc_pu, dwpu_hbm, sem.at[1]),
                      pltpu.make_async_copy(acc_lu, dwlu_hbm, sem.at[2])]
            for cp in copies:
                cp.start()
            for cp in copies:
                cp.wait()

    return pl.pallas_call(
        body, name=name, grid=(ni,),
        in_specs=[pl.BlockSpec((tm, D), lambda i: (i, 0))] + _mix_out_specs(tm, D, pw, lw, cs),
        out_specs=[pl.BlockSpec((tm, 2 * D), lambda i: (i, 0)), pl.BlockSpec((tm, pw), lambda i: (i, 0)),
                   pl.BlockSpec((tm, lw), lambda i: (i, 0)), ANY, ANY, ANY],
        out_shape=[jax.ShapeDtypeStruct((T, 2 * D), MM), jax.ShapeDtypeStruct((T, pw), F32),
                   jax.ShapeDtypeStruct((T, lw), F32), jax.ShapeDtypeStruct((D, D), F32),
                   jax.ShapeDtypeStruct((N_CHIPS, pw, cs), F32), jax.ShapeDtypeStruct((lw, D), F32)],
        scratch_shapes=[pltpu.VMEM((D, D), F32), pltpu.VMEM((N_CHIPS, pw, cs), F32), pltpu.VMEM((lw, D), F32),
                        pltpu.SemaphoreType.DMA((3,))],
        compiler_params=_params(("arbitrary",)),
    )(dxn, proj, proj, pm, zl, wpu, wlu, wo)


def _shifted(buf, val, shift, S):
    buf[pl.ds(SHIFT_PAD, S), :] = val
    return buf[pl.ds(SHIFT_PAD - shift, S), :]


def _zero_pads(buf, S):
    z = jnp.zeros((SHIFT_PAD, buf.shape[1]), F32)
    buf[pl.ds(0, SHIFT_PAD), :] = z
    buf[pl.ds(SHIFT_PAD + S, SHIFT_PAD), :] = z


def _window_sum(buf, val, window, S, lead=False):
    acc, width = val, 1
    while width < window:
        acc = acc + _shifted(buf, acc, -width if lead else width, S)
        width *= 2
    return acc


def _pool_count(S, window):
    t = lax.broadcasted_iota(jnp.int32, (S, LANES), 0)
    return jnp.minimum(t + 1, window).astype(F32)


def _pool_fwd_group(u, g, window, pw_ref, pb_ref, buf, S):
    pooled = _window_sum(buf, u, window, S) / _pool_count(S, window) - u
    return pooled, _dot(pooled, pw_ref[g]) + pb_ref[g]


def _pool_fwd(proj, pool_w, pool_b, pool_scale, name):
    S = proj.shape[0]
    G = pool_w.shape[0]
    pw = G * LANES

    def body(u_ref, pw_ref, pb_ref, ps_ref, pm_ref, buf):
        _zero_pads(buf, S)
        for g in range(G):
            cols = slice(g * LANES, (g + 1) * LANES)
            _, mixed = _pool_fwd_group(u_ref[:, cols], g, POOL_WINDOWS[g], pw_ref, pb_ref, buf, S)
            pm_ref[:, cols] = (mixed * ps_ref[:, cols]).astype(MM)

    return pl.pallas_call(
        body, name=name, grid=(1,),
        in_specs=[pl.BlockSpec((S, pw), lambda i: (0, 0)), pl.BlockSpec((G, LANES, LANES), lambda i: (0, 0, 0)),
                  pl.BlockSpec((G, 1, LANES), lambda i: (0, 0, 0)), pl.BlockSpec((1, pw), lambda i: (0, 0))],
        out_specs=pl.BlockSpec((S, pw), lambda i: (0, 0)),
        out_shape=jax.ShapeDtypeStruct((S, pw), MM),
        scratch_shapes=[pltpu.VMEM((S + 2 * SHIFT_PAD, LANES), F32)],
        compiler_params=_params(("arbitrary",)),
    )(proj, pool_w, pool_b, pool_scale)


def _pool_bwd(proj, dpm, pool_w, pool_b, pool_scale, name):
    S = proj.shape[0]
    G = pool_w.shape[0]
    pw = G * LANES

    def body(u_ref, dpm_ref, pw_ref, pb_ref, ps_ref, du_ref, dpw_ref, dpb_ref, dps_ref, buf):
        _zero_pads(buf, S)
        for g in range(G):
            cols = slice(g * LANES, (g + 1) * LANES)
            window = POOL_WINDOWS[g]
            pooled, mixed = _pool_fwd_group(u_ref[:, cols], g, window, pw_ref, pb_ref, buf, S)
            d_out = dpm_ref[:, cols]
            dmixed = d_out * ps_ref[:, cols]
            dps_ref[:, cols] = jnp.sum(d_out * mixed, axis=0, keepdims=True)
            dpb_ref[g] = jnp.sum(dmixed, axis=0, keepdims=True)
            dpw_ref[g] = _dot_tn(pooled, dmixed)
            dpooled = _dot_nt(dmixed, pw_ref[g])
            q = dpooled / _pool_count(S, window)
            du_ref[:, cols] = (_window_sum(buf, q, window, S, lead=True) - dpooled).astype(MM)

    return pl.pallas_call(
        body, name=name, grid=(1,),
        in_specs=[pl.BlockSpec((S, pw), lambda i: (0, 0)), pl.BlockSpec((S, pw), lambda i: (0, 0)),
                  pl.BlockSpec((G, LANES, LANES), lambda i: (0, 0, 0)),
                  pl.BlockSpec((G, 1, LANES), lambda i: (0, 0, 0)), pl.BlockSpec((1, pw), lambda i: (0, 0))],
        out_specs=[pl.BlockSpec((S, pw), lambda i: (0, 0)), pl.BlockSpec((G, LANES, LANES), lambda i: (0, 0, 0)),
                   pl.BlockSpec((G, 1, LANES), lambda i: (0, 0, 0)), pl.BlockSpec((1, pw), lambda i: (0, 0))],
        out_shape=[jax.ShapeDtypeStruct((S, pw), MM), jax.ShapeDtypeStruct((G, LANES, LANES), F32),
                   jax.ShapeDtypeStruct((G, 1, LANES), F32), jax.ShapeDtypeStruct((1, pw), F32)],
        scratch_shapes=[pltpu.VMEM((S + 2 * SHIFT_PAD, LANES), F32)],
        compiler_params=_params(("arbitrary",)),
    )(proj, dpm, pool_w, pool_b, pool_scale)


def _scan(a, b, bufs, S, reverse=False):
    pad = S // 2
    seq = pl.ds(pad, S)
    A, B = bufs[:2], bufs[2:]
    A[0][seq, :] = a
    B[0][seq, :] = b
    cur, d = 0, 1
    while d < S:
        sh = pl.ds(pad + d if reverse else pad - d, S)
        a_c = A[cur][seq, :]
        B[1 - cur][seq, :] = a_c * B[cur][sh, :] + B[cur][seq, :]
        if 2 * d < S:
            A[1 - cur][seq, :] = a_c * A[cur][sh, :]
        cur, d = 1 - cur, 2 * d
    return B[cur][seq, :]


def _init_scan_pads(bufs, S):
    pad = S // 2
    for n, buf in enumerate(bufs):
        fill = jnp.full((pad, LANES), 1.0 if n < 2 else 0.0, F32)
        buf[pl.ds(0, pad), :] = fill
        buf[pl.ds(pad + S, pad), :] = fill


def _gelu_and_grad(x):
    c = 0.7978845608028654
    x2 = x * x
    th = jnp.tanh(c * (x + 0.044715 * (x2 * x)))
    gelu = 0.5 * x * (1.0 + th)
    grad = 0.5 * (1.0 + th) + 0.5 * x * (1.0 - th * th) * (c * (1.0 + 3 * 0.044715 * x2))
    return gelu, grad


def _lru_head_fwd(ul, cw, cb, wa, ba, wx, bx, lam, sbuf, bufs, S):
    sbuf[pl.ds(SHIFT_PAD, S), :] = ul
    v = None
    for k in range(CONV_WIDTH):
        term = sbuf[pl.ds(SHIFT_PAD - (CONV_WIDTH - 1 - k), S), :] * cw[k:k + 1, :]
        v = term if v is None else v + term
    v = v + cb
    r = jax.nn.sigmoid(_dot(v, wa) + ba)
    ig = jax.nn.sigmoid(_dot(v, wx) + bx)
    sp = jax.nn.softplus(-lam)
    log_a = -LRU_C * r * sp
    a = jnp.exp(log_a)
    mult = jnp.sqrt(-jnp.tanh(log_a) * (1.0 + a * a))
    h = _scan(a, mult * (ig * v), bufs, S)
    return v, r, ig, sp, a, mult, h


def _lru_specs(S, H, lw, pw):
    b0 = pw // LANES
    return [pl.BlockSpec((S, LANES), lambda h: (0, b0 + h)), pl.BlockSpec((S, LANES), lambda h: (0, b0 + H + h)),
            pl.BlockSpec((CONV_WIDTH, LANES), lambda h: (0, h)), pl.BlockSpec((1, LANES), lambda h: (0, h)),
            pl.BlockSpec((None, LANES, LANES), lambda h: (h, 0, 0)), pl.BlockSpec((None, 1, LANES), lambda h: (h, 0, 0)),
            pl.BlockSpec((None, LANES, LANES), lambda h: (h, 0, 0)), pl.BlockSpec((None, 1, LANES), lambda h: (h, 0, 0)),
            pl.BlockSpec((1, LANES), lambda h: (0, h))]


def _lru_scratch(S):
    return [pltpu.VMEM((S + 2 * SHIFT_PAD, LANES), F32)] + [pltpu.VMEM((2 * S, LANES), F32)] * 4


def _lru_fwd(proj, conv_w, conv_b, wa, ba, wx, bx, lam, pw, name):
    S = proj.shape[0]
    H = wa.shape[0]
    lw = H * LANES

    def body(ul_ref, ug_ref, cw_ref, cb_ref, wa_ref, ba_ref, wx_ref, bx_ref, lam_ref, zl_ref, sbuf, *bufs):
        @pl.when(pl.program_id(0) == 0)
        def _():
            _zero_pads(sbuf, S)
            _init_scan_pads(bufs, S)

        h = _lru_head_fwd(ul_ref[...], cw_ref[...], cb_ref[...], wa_ref[...], ba_ref[...], wx_ref[...], bx_ref[...],
                          lam_ref[...], sbuf, bufs, S)[-1]
        zl_ref[...] = (h * jax.nn.gelu(ug_ref[...])).astype(MM)

    return pl.pallas_call(
        body, name=name, grid=(H,),
        in_specs=_lru_specs(S, H, lw, pw),
        out_specs=pl.BlockSpec((S, LANES), lambda h: (0, h)),
        out_shape=jax.ShapeDtypeStruct((S, lw), MM),
        scratch_shapes=_lru_scratch(S),
        compiler_params=_params(("arbitrary",)),
    )(proj, proj, conv_w, conv_b, wa, ba, wx, bx, lam)


def _lru_bwd(proj, dzl, conv_w, conv_b, wa, ba, wx, bx, lam, pw, name):
    S = proj.shape[0]
    H = wa.shape[0]
    lw = H * LANES

    def body(ul_ref, ug_ref, cw_ref, cb_ref, wa_ref, ba_ref, wx_ref, bx_ref, lam_ref, dzl_ref,
             dul_ref, dug_ref, dcw_ref, dcb_ref, dwa_ref, dba_ref, dwx_ref, dbx_ref, dlam_ref, sbuf, *bufs):
        @pl.when(pl.program_id(0) == 0)
        def _():
            _zero_pads(sbuf, S)
            _init_scan_pads(bufs, S)

        ul, cw, lam = ul_ref[...], cw_ref[...], lam_ref[...]
        wa, wx = wa_ref[...], wx_ref[...]
        v, r, ig, sp, a, mult, h = _lru_head_fwd(ul, cw, cb_ref[...], wa, ba_ref[...], wx, bx_ref[...], lam, sbuf, bufs, S)
        gelu, gelu_grad = _gelu_and_grad(ug_ref[...])
        dzl_t = dzl_ref[...]
        dug_ref[...] = (dzl_t * h * gelu_grad).astype(MM)
        a_next = _shifted(sbuf, a, -1, S)
        lam_t = _scan(a_next, dzl_t * gelu, bufs, S, reverse=True)
        da = lam_t * _shifted(sbuf, h, 1, S)
        d_iv = lam_t * mult
        d_log_a = da * a - (lam_t * (ig * v)) * (a * a) / mult
        dlam_ref[...] = jnp.sum(d_log_a * r, axis=0, keepdims=True) * (LRU_C * jax.nn.sigmoid(-lam))
        dra = (d_log_a * (-LRU_C * sp)) * (r * (1.0 - r))
        dia = (d_iv * v) * (ig * (1.0 - ig))
        dba_ref[...] = jnp.sum(dra, axis=0, keepdims=True)
        dbx_ref[...] = jnp.sum(dia, axis=0, keepdims=True)
        dwa_ref[...] = _dot_tn(v, dra)
        dwx_ref[...] = _dot_tn(v, dia)
        dv = d_iv * ig + _dot_nt(dra, wa) + _dot_nt(dia, wx)
        dcb_ref[...] = jnp.sum(dv, axis=0, keepdims=True)
        sbuf[pl.ds(SHIFT_PAD, S), :] = ul
        for k in range(CONV_WIDTH):
            dcw_ref[k:k + 1, :] = jnp.sum(dv * sbuf[pl.ds(SHIFT_PAD - (CONV_WIDTH - 1 - k), S), :], axis=0, keepdims=True)
        sbuf[pl.ds(SHIFT_PAD, S), :] = dv
        dul = None
        for k in range(CONV_WIDTH):
            term = sbuf[pl.ds(SHIFT_PAD + (CONV_WIDTH - 1 - k), S), :] * cw[k:k + 1, :]
            dul = term if dul is None else dul + term
        dul_ref[...] = dul.astype(MM)

    head_mat = pl.BlockSpec((None, LANES, LANES), lambda h: (h, 0, 0))
    head_vec = pl.BlockSpec((None, 1, LANES), lambda h: (h, 0, 0))
    col = pl.BlockSpec((S, LANES), lambda h: (0, h))
    row = pl.BlockSpec((1, LANES), lambda h: (0, h))
    return pl.pallas_call(
        body, name=name, grid=(H,),
        in_specs=_lru_specs(S, H, lw, pw) + [col],
        out_specs=[col, col, pl.BlockSpec((CONV_WIDTH, LANES), lambda h: (0, h)), row, head_mat, head_vec, head_mat,
                   head_vec, row],
        out_shape=[jax.ShapeDtypeStruct((S, lw), MM), jax.ShapeDtypeStruct((S, lw), MM),
                   jax.ShapeDtypeStruct((CONV_WIDTH, lw), F32), jax.ShapeDtypeStruct((1, lw), F32),
                   jax.ShapeDtypeStruct((H, LANES, LANES), F32), jax.ShapeDtypeStruct((H, 1, LANES), F32),
                   jax.ShapeDtypeStruct((H, LANES, LANES), F32), jax.ShapeDtypeStruct((H, 1, LANES), F32),
                   jax.ShapeDtypeStruct((1, lw), F32)],
        scratch_shapes=_lru_scratch(S),
        compiler_params=_params(("arbitrary",)),
    )(proj, proj, conv_w, conv_b, wa, ba, wx, bx, lam, dzl)


def _loss_head(x, gain, target, name):
    T, D = x.shape
    tm = min(TOKEN_TILE, T)
    ni = T // tm

    def body(x_ref, g_ref, t_ref, loss_ref, dx_ref, dg_ref, loss_acc, dg_acc):
        i = pl.program_id(0)
        g = g_ref[...]
        r, xh, y = _rms(x_ref[...], g)
        err = y - t_ref[...]
        part = 0.5 * jnp.sum(jnp.mean(err * err, axis=-1, keepdims=True), axis=0, keepdims=True)
        _accumulate(loss_acc, jnp.broadcast_to(part, (1, LANES)), i == 0)
        dy = err * (1.0 / D)
        _accumulate(dg_acc, jnp.sum(dy * xh, axis=0, keepdims=True), i == 0)
        dx_ref[...] = _rms_bwd(dy, xh, r, g)

        @pl.when(i == ni - 1)
        def _():
            loss_ref[...] = loss_acc[...]
            dg_ref[...] = dg_acc[...]

    return pl.pallas_call(
        body, name=name, grid=(ni,),
        in_specs=[pl.BlockSpec((tm, D), lambda i: (i, 0)), pl.BlockSpec((1, D), lambda i: (0, 0)),
                  pl.BlockSpec((tm, D), lambda i: (i, 0))],
        out_specs=[pl.BlockSpec((1, LANES), lambda i: (0, 0)), pl.BlockSpec((tm, D), lambda i: (i, 0)),
                   pl.BlockSpec((1, D), lambda i: (0, 0))],
        out_shape=[jax.ShapeDtypeStruct((1, LANES), F32), jax.ShapeDtypeStruct((T, D), F32),
                   jax.ShapeDtypeStruct((1, D), F32)],
        scratch_shapes=[pltpu.VMEM((1, LANES), F32), pltpu.VMEM((1, D), F32)],
        compiler_params=_params(("arbitrary",)),
    )(x, gain, target)


def _my_core():
    return lax.axis_index("c")


def _my_chip():
    return 2 * lax.axis_index("x") + lax.axis_index("y")


def _pair_sum(a, recv, xfer, name):
    n, R, C = a.shape
    hr = R // 2

    def body(a_ref, r_ref, pf_ref, pb_ref):
        s = a_ref[...] + r_ref[...]
        pf_ref[...] = s
        pb_ref[...] = s.astype(xfer)

    piece = pl.BlockSpec((None, hr, C), lambda k: (k, 0, 0))
    return pl.pallas_call(
        body, name=name, grid=(n,),
        in_specs=[pl.BlockSpec((None, hr, C), lambda k: (k, _my_core(), 0)), piece],
        out_specs=[piece, piece],
        out_shape=[jax.ShapeDtypeStruct((n, hr, C), F32), jax.ShapeDtypeStruct((n, hr, C), xfer)],
        compiler_params=_params(("arbitrary",)),
    )(a, recv)


def _chip_sum(pf, others, name):
    _, hr, C = pf.shape
    rb = _row_block(hr, C)
    nb = hr // rb

    def body(o_ref, q_ref, g_ref):
        g_ref[...] = ((o_ref[...] + q_ref[0].astype(F32)) + q_ref[1].astype(F32)) + q_ref[2].astype(F32)

    return pl.pallas_call(
        body, name=name, grid=(nb,),
        in_specs=[pl.BlockSpec((None, rb, C), lambda i: (_my_chip(), i, 0)),
                  pl.BlockSpec((N_CHIPS - 1, rb, C), lambda i: (0, i, 0))],
        out_specs=pl.BlockSpec((rb, C), lambda i: (_my_core() * nb + i, 0)),
        out_shape=jax.ShapeDtypeStruct((2 * hr, C), F32),
        compiler_params=_params(("arbitrary",)),
    )(pf, others)


def _adamw_step(w_ref, g_ref, m_ref, v_ref, go_ref, d_ref, mo_ref, vo_ref):
    g_t = g_ref[...]
    m_t = ADAM_B1 * m_ref[...] + (1.0 - ADAM_B1) * g_t
    v_t = ADAM_B2 * v_ref[...] + (1.0 - ADAM_B2) * (g_t * g_t)
    m_hat = m_t / (1.0 - ADAM_B1 ** ADAM_STEP)
    v_hat = v_t / (1.0 - ADAM_B2 ** ADAM_STEP)
    go_ref[...] = g_t
    d_ref[...] = -ADAM_LR * (m_hat / (jnp.sqrt(v_hat) + ADAM_EPS) + ADAM_WD * w_ref[...])
    mo_ref[...] = m_t
    vo_ref[...] = v_t


def _adamw(w, g, m, v, name):
    R, C = w.shape
    rb = _row_block(R, C)
    blk = pl.BlockSpec((rb, C), lambda i: (i, 0))
    return pl.pallas_call(
        functools.partial(_adamw_step), name=name, grid=(R // rb,), in_specs=[blk] * 4, out_specs=[blk] * 4,
        out_shape=[jax.ShapeDtypeStruct((R, C), F32)] * 4,
        compiler_params=_params(("arbitrary",)),
    )(w, g, m, v)


def _adamw_layer(w, g, m, v, outs, layer, name):
    L, R, C = w.shape
    rb = _row_block(R, C)

    def body(w_ref, g_ref, m_ref, v_ref, *rest):
        _adamw_step(w_ref, g_ref, m_ref, v_ref, *rest[4:])

    stacked = pl.BlockSpec((None, rb, C), lambda i: (layer, i, 0))
    return pl.pallas_call(
        body, name=name, grid=(R // rb,),
        in_specs=[stacked, pl.BlockSpec((rb, C), lambda i: (i, 0)), stacked, stacked] + [ANY] * 4,
        out_specs=[stacked] * 4,
        out_shape=[jax.ShapeDtypeStruct((L, R, C), F32)] * 4,
        input_output_aliases={4 + j: j for j in range(4)},
        compiler_params=_params(("arbitrary",)),
    )(w, g, m, v, *outs)


def _place():
    x, y, c = lax.axis_index("x"), lax.axis_index("y"), lax.axis_index("c")
    others = [(1 - x, y), (x, 1 - y), (1 - x, 1 - y)]
    return x, y, c, 2 * x + y, others


def _half(c, rows):
    return pl.ds(pl.multiple_of(c * (rows // 2), 16), rows // 2)


HBM_SPEC = pl.BlockSpec(memory_space=pltpu.HBM)
SEM_SPEC = pl.BlockSpec(memory_space=pltpu.SEMAPHORE)
SPLIT = pltpu.CompilerParams(has_side_effects=pltpu.SideEffectType.DATAFLOW_SIDE_EFFECTING)
TOKEN = jax.ShapeDtypeStruct((8, LANES), F32)


def _in_hbm(a):
    return pltpu.with_memory_space_constraint(a, pltpu.HBM)


def _split_start(build, count, srcs, lands, after, name):
    ns, nl = len(srcs), len(lands)

    def body(*refs):
        for started, _ in build(refs[:ns], refs[ns:ns + nl], refs[ns + nl + 1], refs[ns + nl + 2]):
            started.start()
        refs[-1][...] = jnp.zeros_like(refs[-1])

    res = pl.pallas_call(
        body, name=name,
        in_specs=[HBM_SPEC] * (ns + nl) + [ANY],
        out_specs=[SEM_SPEC] * 2 + [HBM_SPEC] * nl + [pl.BlockSpec(memory_space=pltpu.VMEM)],
        out_shape=[pltpu.SemaphoreType.DMA((count,))] * 2 + [pltpu.HBM(z.shape, z.dtype) for z in lands] + [TOKEN],
        input_output_aliases={ns + a: 2 + a for a in range(nl)},
        compiler_params=SPLIT,
    )(*[_in_hbm(s) for s in srcs], *[_in_hbm(z) for z in lands], after)
    return res[:2], res[2:2 + nl], res[-1]


def _split_wait(build, srcs, lands, sems, after, name):
    ns, nl = len(srcs), len(lands)

    def body(*refs):
        for started, arriving in build(refs[:ns], refs[ns:ns + nl], refs[ns + nl], refs[ns + nl + 1]):
            started.wait_send()
            arriving.wait_recv()

    return pl.pallas_call(
        body, name=name,
        in_specs=[HBM_SPEC] * (ns + nl) + [SEM_SPEC] * 2 + [ANY],
        out_specs=[HBM_SPEC] * nl,
        out_shape=[pltpu.HBM(z.shape, z.dtype) for z in lands],
        input_output_aliases={ns + a: a for a in range(nl)},
        compiler_params=SPLIT,
    )(*[_in_hbm(s) for s in srcs], *lands, *sems, after)


def _gather_copies(srcs, lands, send, recv, layer):
    x, y, c, k, others = _place()
    pairs = []
    for a in range(len(srcs)):
        rows = _half(c, srcs[a].shape[-2])
        for r, (px, py) in enumerate(others):
            def ici(slot, a=a, r=r, px=px, py=py, rows=rows):
                return pltpu.make_async_remote_copy(src_ref=srcs[a].at[layer].at[rows], dst_ref=lands[a].at[slot].at[rows],
                                                    send_sem=send.at[a * 4 + r], recv_sem=recv.at[a * 4 + r],
                                                    device_id=(px, py, c), device_id_type=MESH)
            pairs.append((ici(k), ici(2 * px + py)))
        own = pltpu.make_async_remote_copy(src_ref=srcs[a].at[layer], dst_ref=lands[a].at[k], send_sem=send.at[a * 4 + 3],
                                           recv_sem=recv.at[a * 4 + 3], device_id=(x, y, 1 - c), device_id_type=MESH)
        pairs.append((own, own))
    return pairs


def _halves_copies(srcs, lands, send, recv):
    x, y, c, _, _ = _place()
    pairs = []
    for a in range(len(srcs)):
        cp = pltpu.make_async_remote_copy(src_ref=srcs[a].at[:, _half(1 - c, srcs[a].shape[1]), :], dst_ref=lands[a],
                                          send_sem=send.at[a], recv_sem=recv.at[a], device_id=(x, y, 1 - c),
                                          device_id_type=MESH)
        pairs.append((cp, cp))
    return pairs


def _chip_copies(srcs, lands, send, recv):
    x, y, c, k, others = _place()
    pairs = []
    for a in range(len(srcs)):
        for r, (px, py) in enumerate(others):
            cp = pltpu.make_async_remote_copy(src_ref=srcs[a].at[2 * px + py], dst_ref=lands[a].at[r],
                                              send_sem=send.at[a * 3 + r], recv_sem=recv.at[a * 3 + r],
                                              device_id=(px, py, c), device_id_type=MESH)
            pairs.append((cp, cp))
    return pairs


def _forward_halves(lands, name):
    n = len(lands)

    def body(*refs):
        outs = refs[n:2 * n]
        send, recv = refs[2 * n:]
        x, y, c, _, others = _place()

        def copy(a, r, half):
            px, py = others[r]
            rows = outs[a].at[2 * px + py].at[_half(half, outs[a].shape[-2])]
            return pltpu.make_async_remote_copy(src_ref=rows, dst_ref=rows, send_sem=send.at[a * 3 + r],
                                                recv_sem=recv.at[a * 3 + r], device_id=(x, y, 1 - c), device_id_type=MESH)

        every = [(a, r) for a in range(n) for r in range(3)]
        for a, r in every:
            copy(a, r, c).start()
        for a, r in every:
            copy(a, r, 1 - c).wait_recv()
        for a, r in every:
            copy(a, r, c).wait_send()

    return pl.pallas_call(
        body, name=name, in_specs=[ANY] * n, out_specs=[ANY] * n,
        out_shape=[jax.ShapeDtypeStruct(z.shape, z.dtype) for z in lands],
        input_output_aliases={a: a for a in range(n)},
        scratch_shapes=[pltpu.SemaphoreType.DMA((3 * n,))] * 2,
    )(*lands)


def _gather_small(small, name):
    def body(in_ref, out_ref, send, recv):
        x, y, c, k, others = _place()
        mine = pltpu.make_async_remote_copy(src_ref=in_ref, dst_ref=out_ref.at[k], send_sem=send.at[3], recv_sem=recv.at[3],
                                            device_id=(x, y, 1 - c), device_id_type=MESH)
        mine.start()
        copies = []
        for r, (px, py) in enumerate(others):
            copies.append(pltpu.make_async_remote_copy(src_ref=in_ref, dst_ref=out_ref.at[k], send_sem=send.at[r],
                                                       recv_sem=recv.at[r], device_id=(px, py, c), device_id_type=MESH))
            copies[-1].start()
        for r, (px, py) in enumerate(others):
            pltpu.make_async_remote_copy(src_ref=in_ref, dst_ref=out_ref.at[2 * px + py], send_sem=send.at[r],
                                         recv_sem=recv.at[r], device_id=(px, py, c), device_id_type=MESH).wait_recv()
        for cp in copies:
            cp.wait_send()
        mine.wait()

    return pl.pallas_call(
        body, name=name, in_specs=[ANY], out_specs=ANY,
        out_shape=jax.ShapeDtypeStruct((N_CHIPS,) + small.shape, small.dtype),
        scratch_shapes=[pltpu.SemaphoreType.DMA((4,))] * 2,
    )(small)


def _join_halves(fulls, name):
    n = len(fulls)

    def body(*refs):
        outs = refs[n:2 * n]
        send, recv = refs[2 * n:]
        x, y, c, _, _ = _place()

        def copy(a, half):
            rows = outs[a].at[_half(half, outs[a].shape[0]), :]
            return pltpu.make_async_remote_copy(src_ref=rows, dst_ref=rows, send_sem=send.at[a], recv_sem=recv.at[a],
                                                device_id=(x, y, 1 - c), device_id_type=MESH)

        for a in range(n):
            copy(a, c).start()
        for a in range(n):
            copy(a, 1 - c).wait_recv()
        for a in range(n):
            copy(a, c).wait_send()

    return pl.pallas_call(
        body, name=name, in_specs=[ANY] * n, out_specs=[ANY] * n,
        out_shape=[jax.ShapeDtypeStruct(f.shape, f.dtype) for f in fulls],
        input_output_aliases={a: a for a in range(n)},
        scratch_shapes=[pltpu.SemaphoreType.DMA((n,))] * 2,
    )(*fulls)


GROUPS = (("ffn1_w_up", "ffn1_w_down"), ("w_in", "w_pool_up", "w_lru_up", "w_out"), ("ffn2_w_up", "ffn2_w_down"))
BIG = GROUPS[0] + GROUPS[1] + GROUPS[2]
SMALL_LAYER = ("norm_ffn1", "norm_mix", "pool_w", "pool_b", "pool_scale", "conv_w", "conv_b", "lru_w_a", "lru_b_a",
               "lru_w_x", "lru_b_x", "lru_lambda", "norm_ffn2")
SMALL = SMALL_LAYER + ("final_norm",)
GATHER_AHEAD = 3
WEIGHTS = ("norm_ffn1", "ffn1_w_up", "ffn1_w_down", "norm_mix", "w_in", "pool_w", "pool_b", "pool_scale", "w_pool_up",
           "conv_w", "conv_b", "lru_w_a", "lru_b_a", "lru_w_x", "lru_b_x", "lru_lambda", "w_lru_up", "w_out", "norm_ffn2",
           "ffn2_w_up", "ffn2_w_down", "final_norm")


def _pack(arrays, rows_multiple):
    flat = jnp.concatenate([a.reshape(-1) for a in arrays])
    rows = -(-flat.shape[0] // LANES)
    rows = -(-rows // rows_multiple) * rows_multiple
    return jnp.pad(flat, (0, rows * LANES - flat.shape[0])).reshape(rows, LANES)


def _unpack(packed, like):
    flat, out, at = packed.reshape(-1), [], 0
    for a in like:
        out.append(flat[at:at + a.size].reshape(a.shape))
        at += a.size
    return out


def kernel(x, norm_ffn1, ffn1_w_up, ffn1_w_down, norm_mix, w_in, pool_w, pool_b, pool_scale, w_pool_up, conv_w, conv_b, lru_w_a, lru_b_a, lru_w_x, lru_b_x, lru_lambda, w_lru_up, w_out, norm_ffn2, ffn2_w_up, ffn2_w_down, final_norm, loss_target, m_norm_ffn1, m_ffn1_w_up, m_ffn1_w_down, m_norm_mix, m_w_in, m_pool_w, m_pool_b, m_pool_scale, m_w_pool_up, m_conv_w, m_conv_b, m_lru_w_a, m_lru_b_a, m_lru_w_x, m_lru_b_x, m_lru_lambda, m_w_lru_up, m_w_out, m_norm_ffn2, m_ffn2_w_up, m_ffn2_w_down, m_final_norm, v_norm_ffn1, v_ffn1_w_up, v_ffn1_w_down, v_norm_mix, v_w_in, v_pool_w, v_pool_b, v_pool_scale, v_w_pool_up, v_conv_w, v_conv_b, v_lru_w_a, v_lru_b_a, v_lru_w_x, v_lru_b_x, v_lru_lambda, v_w_lru_up, v_w_out, v_norm_ffn2, v_ffn2_w_up, v_ffn2_w_down, v_final_norm):
    given = dict(locals())
    W = {n: given[n] for n in WEIGHTS}
    M = {n: given["m_" + n] for n in WEIGHTS}
    V = {n: given["v_" + n] for n in WEIGHTS}
    L = norm_ffn1.shape[0]
    T, D = x.shape[1], x.shape[2]
    G, H = pool_w.shape[1], lru_w_a.shape[1]
    pw, lw = G * LANES, H * LANES
    chip = 2 * lax.axis_index("x") + lax.axis_index("y")

    def tied(gain, token):
        return gain if token is None else gain + token[:1, :1]

    shards = {n: W[n].astype(MM) for n in BIG}
    conv_full = _gather_small(conv_w.reshape(L * CONV_WIDTH, conv_w.shape[2]), "gather_conv_w")
    conv_full = conv_full.reshape(N_CHIPS, L, CONV_WIDTH, -1).transpose(1, 2, 0, 3).reshape(L, CONV_WIDTH, lw)

    order = [(l, g) for l in range(L) for g in range(len(GROUPS))]
    started = {}

    def start_gather(i, after):
        l, g = order[i]
        srcs = [shards[n] for n in GROUPS[g]]
        lands = [lax.empty((N_CHIPS,) + s.shape[1:], s.dtype) for s in srcs]
        build = functools.partial(_gather_copies, layer=l)
        started[i] = (build, srcs) + _split_start(build, 4 * len(srcs), srcs, lands, after, f"gather_start_{l}_{g}")
        return started[i][-1]

    def finish_gather(i, after):
        l, g = order[i]
        build, srcs, sems, lands, _ = started.pop(i)
        lands = _split_wait(build, srcs, lands, sems, after, f"gather_wait_{l}_{g}")
        got = _forward_halves(lands, f"forward_halves_{l}_{g}")
        token = start_gather(i + GATHER_AHEAD, got[0]) if i + GATHER_AHEAD < len(order) else None
        return got, token

    token = conv_full
    for i in range(min(GATHER_AHEAD, len(order))):
        token = start_gather(i, token)

    xs = x.reshape(T, D)
    saved = []
    for l in range(L):
        small = dict(pool_w=pool_w[l], pool_b=pool_b[l].reshape(G, 1, LANES), pool_scale=pool_scale[l].reshape(1, pw),
                     conv_w=conv_full[l], conv_b=conv_b[l].reshape(1, lw), wa=lru_w_a[l],
                     ba=lru_b_a[l].reshape(H, 1, LANES), wx=lru_w_x[l], bx=lru_b_x[l].reshape(H, 1, LANES),
                     lam=lru_lambda[l].reshape(1, lw))
        g1, g2, g3 = norm_ffn1[l].reshape(1, D), norm_mix[l].reshape(1, D), norm_ffn2[l].reshape(1, D)

        x0 = xs
        (w_up1, w_dn1), token = finish_gather(3 * l, token if l == 0 else x0)
        w_dn1 = w_dn1.reshape(-1, D)
        u1 = _norm_matmul_fwd(x0, tied(g1, token), w_up1, f"ffn1_up_{l}")
        x1 = _swiglu_down_fwd(u1, x0, w_dn1, f"ffn1_down_{l}")
        (w_i, w_pu, w_lu, w_o), token = finish_gather(3 * l + 1, x1)
        w_lu, w_o = w_lu.reshape(-1, D), w_o.reshape(-1, D)
        proj = _norm_matmul_fwd(x1, tied(g2, token), w_i, f"mix_in_{l}")
        pm = _pool_fwd(proj, small["pool_w"], small["pool_b"], small["pool_scale"], f"pool_{l}")
        zl = _lru_fwd(proj, small["conv_w"], small["conv_b"], small["wa"], small["ba"], small["wx"], small["bx"],
                      small["lam"], pw, f"lru_{l}")
        x2 = _mix_out_fwd(x1, proj, pm, zl, w_pu, w_lu, w_o, f"mix_out_{l}")
        (w_up2, w_dn2), token = finish_gather(3 * l + 2, x2)
        w_dn2 = w_dn2.reshape(-1, D)
        u2 = _norm_matmul_fwd(x2, tied(g3, token), w_up2, f"ffn2_up_{l}")
        xs = _swiglu_down_fwd(u2, x2, w_dn2, f"ffn2_down_{l}")
        saved.append(dict(x0=x0, u1=u1, x1=x1, proj=proj, pm=pm, zl=zl, x2=x2, u2=u2, small=small, g=(g1, g2, g3),
                          w=(w_up1, w_dn1, w_i, w_pu, w_lu, w_o, w_up2, w_dn2)))

    loss_part, dx, d_final = _loss_head(xs, final_norm.reshape(1, D), loss_target.reshape(T, D), "loss_head")
    loss = lax.psum(loss_part[0, 0], ("x", "y", "c"))

    du_spec_ffn = lambda tm, tn: pl.BlockSpec((None, tm, tn), lambda k, i: (k // 2, i, k % 2))
    du_spec_mix = lambda tm, tn: pl.BlockSpec((tm, tn), lambda k, i: (i, k))
    out = {n: [lax.empty(W[n].shape, F32) for _ in range(4)] for n in BIG}
    small_pieces = [None] * L
    stages = [None, None]

    def advance(new):
        token = None
        first = None
        if new is not None:
            l, names, grads = new
            tag = f"{names[0]}_{l}"
            lands = [lax.empty((N_CHIPS, a.shape[1] // 2, a.shape[2]), F32) for a in grads]
            sems, lands, token = _split_start(_halves_copies, len(grads), grads, lands, grads[0], f"halves_start_{tag}")
            first = (l, names, tag, grads, sems, lands)
        second = None
        if stages[0] is not None:
            l, names, tag, grads, sems, lands = stages[0]
            recv = _split_wait(_halves_copies, grads, lands, sems, grads[0] if token is None else token,
                               f"halves_wait_{tag}")
            sums = [_pair_sum(g, r, F32 if n == "small" else XFER, f"pair_sum_{n}_{l}")
                    for g, r, n in zip(grads, recv, names)]
            pbs = [p[1] for p in sums]
            lands = [lax.empty((N_CHIPS - 1,) + p.shape[1:], p.dtype) for p in pbs]
            sems, lands, token = _split_start(_chip_copies, 3 * len(pbs), pbs, lands, pbs[0], f"chips_start_{tag}")
            second = (l, names, tag, [p[0] for p in sums], pbs, sems, lands)
        if stages[1] is not None:
            l, names, tag, pfs, pbs, sems, lands = stages[1]
            got = _split_wait(_chip_copies, pbs, lands, sems, pbs[0] if token is None else token, f"chips_wait_{tag}")
            whole = _join_halves([_chip_sum(p, q, f"chip_sum_{n}_{l}") for p, q, n in zip(pfs, got, names)],
                                 f"join_halves_{tag}")
            for n, g in zip(names, whole):
                if n == "small":
                    small_pieces[l] = g
                else:
                    out[n] = _adamw_layer(W[n], g, M[n], V[n], out[n], l, f"adamw_{n}_{l}")
        stages[:] = [first, second]
        return token

    token = None
    for l in reversed(range(L)):
        s = saved[l]
        w_up1, w_dn1, w_i, w_pu, w_lu, w_o, w_up2, w_dn2 = s["w"]
        g1, g2, g3 = s["g"]
        sm = s["small"]
        du2, d_dn2 = _swiglu_down_bwd(s["u2"], dx, w_dn2, f"ffn2_down_bwd_{l}")
        dx, d_up2, dg3 = _norm_matmul_bwd(s["x2"], tied(g3, token), du2, du_spec_ffn, w_up2, dx, f"ffn2_up_bwd_{l}")
        token = advance((l, GROUPS[2], [d_up2, d_dn2.reshape(N_CHIPS, -1, D)]))
        dgl, dpm, dzl, d_o, d_pu, d_lu = _mix_out_bwd(dx, s["proj"], s["pm"], s["zl"], w_pu, w_lu, w_o, f"mix_out_bwd_{l}")
        du_pool, d_pool_w, d_pool_b, d_pool_scale = _pool_bwd(s["proj"], dpm, sm["pool_w"], sm["pool_b"], sm["pool_scale"],
                                                            f"pool_bwd_{l}")
        dul, dug, d_cw, d_cb, d_wa, d_ba, d_wx, d_bx, d_lam = _lru_bwd(
            s["proj"], dzl, sm["conv_w"], sm["conv_b"], sm["wa"], sm["ba"], sm["wx"], sm["bx"], sm["lam"], pw, f"lru_bwd_{l}")
        dproj = jnp.concatenate([du_pool, dul, dug, dgl], axis=1)
        dx, d_in, dg2 = _norm_matmul_bwd(s["x1"], tied(g2, token), dproj, du_spec_mix, w_i, dx, f"mix_in_bwd_{l}")
        token = advance((l, GROUPS[1], [d_in, d_pu, d_lu.reshape(N_CHIPS, -1, D), d_o.reshape(N_CHIPS, -1, D)]))
        du1, d_dn1 = _swiglu_down_bwd(s["u1"], dx, w_dn1, f"ffn1_down_bwd_{l}")
        dx, d_up1, dg1 = _norm_matmul_bwd(s["x0"], tied(g1, token), du1, du_spec_ffn, w_up1, dx, f"ffn1_up_bwd_{l}")
        small = [dg1, dg2, d_pool_w, d_pool_b, d_pool_scale, d_cw, d_cb, d_wa, d_ba, d_wx, d_bx, d_lam, dg3]
        small = _pack(small + ([d_final] if l == L - 1 else []), 2 * 16 * N_CHIPS).reshape(N_CHIPS, -1, LANES)
        token = advance((l, GROUPS[0] + ("small",), [d_up1, d_dn1.reshape(N_CHIPS, -1, D), small]))
    while stages[0] is not None or stages[1] is not None:
        advance(None)

    pieces = _gather_small(jnp.concatenate(small_pieces), "gather_small_grads")
    pieces = pieces.reshape(N_CHIPS, L, -1)
    like = {n: jax.ShapeDtypeStruct(W[n].shape[1:] if n != "conv_w" else (CONV_WIDTH, lw), F32) for n in SMALL_LAYER}
    per_layer = []
    for l in range(L):
        names = SMALL_LAYER + (("final_norm",) if l == L - 1 else ())
        shapes = [like[n] if n in like else jax.ShapeDtypeStruct((D,), F32) for n in names]
        per_layer.append(dict(zip(names, _unpack(pieces[:, l], shapes))))
    small_sum = {n: jnp.stack([per_layer[l][n] for l in range(L)]) for n in SMALL_LAYER}
    small_sum["final_norm"] = per_layer[L - 1]["final_norm"]
    cs = conv_w.shape[2]
    small_sum["conv_w"] = lax.dynamic_slice_in_dim(small_sum["conv_w"], chip * cs, cs, axis=2)
    packs = [_pack([d[n] for n in SMALL], 1024) for d in (W, small_sum, M, V)]
    res = _adamw(*packs, "adamw_small")
    like = [W[n] for n in SMALL]
    unpacked = [_unpack(r, like) for r in res]
    for j, n in enumerate(SMALL):
        out[n] = tuple(u[j] for u in unpacked)

    return (loss, dx.reshape(x.shape), *[out[n][0] for n in WEIGHTS], *[out[n][1] for n in WEIGHTS],
            *[out[n][2] for n in WEIGHTS], *[out[n][3] for n in WEIGHTS])
```

```python
import functools

import jax
import jax.numpy as jnp
from jax import lax
from jax.experimental import pallas as pl
from jax.experimental.pallas import tpu as pltpu

F32 = jnp.float32
MM = jnp.bfloat16
XFER = jnp.bfloat16

EPS = 1e-6
LRU_C = 8.0
POOL_WINDOWS = (2, 4, 8, 16)
CONV_WIDTH = 4
ADAM_LR, ADAM_B1, ADAM_B2, ADAM_EPS, ADAM_WD, ADAM_STEP = 0.001, 0.9, 0.999, 1e-08, 0.01, 10

N_CHIPS = 4
LANES = 128
SHIFT_PAD = 8
TOKEN_TILE = 512
VMEM_LIMIT = 60 * 1024 * 1024
MESH = pl.DeviceIdType.MESH
ANY = pl.BlockSpec(memory_space=pl.ANY)


def _params(sem=None):
    return pltpu.CompilerParams(dimension_semantics=sem, vmem_limit_bytes=VMEM_LIMIT)


def _dot(a, b):
    return jnp.dot(a.astype(MM), b.astype(MM), preferred_element_type=F32)


def _dot_nt(a, b):
    return lax.dot_general(a.astype(MM), b.astype(MM), (((1,), (1,)), ((), ())), preferred_element_type=F32)


def _dot_tn(a, b):
    return lax.dot_general(a.astype(MM), b.astype(MM), (((0,), (0,)), ((), ())), preferred_element_type=F32)


def _rms(x, g):
    r = lax.rsqrt(jnp.mean(x * x, axis=-1, keepdims=True) + EPS)
    xh = x * r
    return r, xh, xh * g


def _rms_bwd(dh, xh, r, g):
    dxh = dh * g
    return r * (dxh - xh * jnp.mean(dxh * xh, axis=-1, keepdims=True))


def _accumulate(ref, val, first):
    @pl.when(first)
    def _():
        ref[...] = val

    @pl.when(jnp.logical_not(first))
    def _():
        ref[...] += val


def _row_block(rows, cols, itemsize=4, budget=1 << 20):
    best = None
    for rb in range(16, rows + 1, 16):
        if rows % rb == 0 and rb * cols * itemsize <= budget:
            best = rb
    return best if best is not None else rows


def _norm_matmul_fwd(x, gain, w, name):
    T, D = x.shape
    K, _, tn = w.shape
    tm = min(TOKEN_TILE, T)

    def body(x_ref, g_ref, w_ref, u_ref):
        _, _, h = _rms(x_ref[...], g_ref[...])
        u_ref[...] = _dot(h, w_ref[...])

    return pl.pallas_call(
        body, name=name, grid=(K, T // tm),
        in_specs=[pl.BlockSpec((tm, D), lambda k, i: (i, 0)), pl.BlockSpec((1, D), lambda k, i: (0, 0)),
                  pl.BlockSpec((None, D, tn), lambda k, i: (k, 0, 0))],
        out_specs=pl.BlockSpec((tm, tn), lambda k, i: (i, k)),
        out_shape=jax.ShapeDtypeStruct((T, K * tn), F32),
        compiler_params=_params(("arbitrary", "arbitrary")),
    )(x, gain, w)


def _norm_matmul_bwd(x, gain, du, du_spec, w, dres, name):
    T, D = x.shape
    K, _, tn = w.shape
    tm = min(TOKEN_TILE, T)
    ni = T // tm

    def body(x_ref, g_ref, du_ref, w_ref, dres_ref, dx_ref, dw_ref, dg_ref, dh_acc, dg_acc):
        k, i = pl.program_id(0), pl.program_id(1)
        g = g_ref[...]
        r, xh, h = _rms(x_ref[...], g)
        du_t = du_ref[...].astype(MM)
        rows = pl.ds(pl.multiple_of(i * tm, tm), tm)
        part = _dot_nt(du_t, w_ref[...])

        @pl.when(k == 0)
        def _():
            dh_acc[rows, :] = part

        @pl.when(k > 0)
        def _():
            dh_acc[rows, :] += part

        _accumulate(dw_ref, _dot_tn(h, du_t), i == 0)

        @pl.when(k == K - 1)
        def _():
            dh = dh_acc[rows, :]
            _accumulate(dg_acc, jnp.sum(dh * xh, axis=0, keepdims=True), i == 0)
            dx_ref[...] = dres_ref[...] + _rms_bwd(dh, xh, r, g)

            @pl.when(i == ni - 1)
            def _():
                dg_ref[...] = dg_acc[...]

    def last(k, i):
        return (jnp.where(k == K - 1, i, 0), 0)

    return pl.pallas_call(
        body, name=name, grid=(K, ni),
        in_specs=[pl.BlockSpec((tm, D), lambda k, i: (i, 0)), pl.BlockSpec((1, D), lambda k, i: (0, 0)),
                  du_spec(tm, tn), pl.BlockSpec((None, D, tn), lambda k, i: (k, 0, 0)),
                  pl.BlockSpec((tm, D), last)],
        out_specs=[pl.BlockSpec((tm, D), last), pl.BlockSpec((None, D, tn), lambda k, i: (k, 0, 0)),
                   pl.BlockSpec((1, D), lambda k, i: (0, 0))],
        out_shape=[jax.ShapeDtypeStruct((T, D), F32), jax.ShapeDtypeStruct((K, D, tn), F32),
                   jax.ShapeDtypeStruct((1, D), F32)],
        scratch_shapes=[pltpu.VMEM((T, D), F32), pltpu.VMEM((1, D), F32)],
        compiler_params=_params(("arbitrary", "arbitrary")),
    )(x, gain, du, w, dres)


def _swiglu_down_fwd(u, x, wd, name):
    T, D = x.shape
    Fh = wd.shape[0]
    tm = min(TOKEN_TILE, T)

    def body(a_ref, b_ref, x_ref, wd_ref, o_ref):
        a = a_ref[...]
        s = a * jax.nn.sigmoid(a) * b_ref[...]
        o_ref[...] = x_ref[...] + 0.5 * _dot(s, wd_ref[...])

    return pl.pallas_call(
        body, name=name, grid=(T // tm,),
        in_specs=[pl.BlockSpec((tm, Fh), lambda i: (i, 0)), pl.BlockSpec((tm, Fh), lambda i: (i, 1)),
                  pl.BlockSpec((tm, D), lambda i: (i, 0)), pl.BlockSpec((Fh, D), lambda i: (0, 0))],
        out_specs=pl.BlockSpec((tm, D), lambda i: (i, 0)),
        out_shape=jax.ShapeDtypeStruct((T, D), F32),
        compiler_params=_params(("arbitrary",)),
    )(u, u, x, wd)


def _swiglu_down_bwd(u, dxn, wd, name):
    T, D = dxn.shape
    Fh = wd.shape[0]
    tm = min(TOKEN_TILE, T)
    nj = 2 if Fh % (2 * LANES) == 0 else 1
    tf = Fh // nj

    def body(a_ref, b_ref, dxn_ref, wd_ref, du_ref, dwd_ref):
        i = pl.program_id(1)
        a, b = a_ref[...], b_ref[...]
        dyh = (0.5 * dxn_ref[...]).astype(MM)
        ds = _dot_nt(dyh, wd_ref[...])
        sig = jax.nn.sigmoid(a)
        sa = a * sig
        _accumulate(dwd_ref, _dot_tn(sa * b, dyh), i == 0)
        du_ref[0] = (ds * b * (sig * (1.0 + a * (1.0 - sig)))).astype(MM)
        du_ref[1] = (ds * sa).astype(MM)

    return pl.pallas_call(
        body, name=name, grid=(nj, T // tm),
        in_specs=[pl.BlockSpec((tm, tf), lambda j, i: (i, j)), pl.BlockSpec((tm, tf), lambda j, i: (i, j + nj)),
                  pl.BlockSpec((tm, D), lambda j, i: (i, 0)), pl.BlockSpec((tf, D), lambda j, i: (j, 0))],
        out_specs=[pl.BlockSpec((2, tm, tf), lambda j, i: (0, i, j)), pl.BlockSpec((tf, D), lambda j, i: (j, 0))],
        out_shape=[jax.ShapeDtypeStruct((2, T, Fh), MM), jax.ShapeDtypeStruct((Fh, D), F32)],
        compiler_params=_params(("arbitrary", "arbitrary")),
    )(u, u, dxn, wd)


def _mix_branches(pm, zl, gp_logit, gr_logit, wpu_ref, wlu_ref):
    y_pool = jnp.concatenate([_dot(pm, wpu_ref[k]) for k in range(N_CHIPS)], axis=1)
    y_lru = _dot(zl, wlu_ref[...])
    return y_pool, y_lru, jax.nn.sigmoid(gp_logit), jax.nn.sigmoid(gr_logit)


def _mix_out_specs(tm, D, pw, lw, cs):
    gate0 = (pw + 2 * lw) // D
    return [pl.BlockSpec((tm, D), lambda i: (i, gate0)), pl.BlockSpec((tm, D), lambda i: (i, gate0 + 1)),
            pl.BlockSpec((tm, pw), lambda i: (i, 0)), pl.BlockSpec((tm, lw), lambda i: (i, 0)),
            pl.BlockSpec((N_CHIPS, pw, cs), lambda i: (0, 0, 0)), pl.BlockSpec((lw, D), lambda i: (0, 0)),
            pl.BlockSpec((D, D), lambda i: (0, 0))]


def _mix_out_fwd(x, proj, pm, zl, wpu, wlu, wo, name):
    T, D = x.shape
    pw, lw, cs = pm.shape[1], zl.shape[1], wpu.shape[2]
    assert (pw + 2 * lw) % D == 0
    tm = min(TOKEN_TILE, T)

    def body(x_ref, gp_ref, gr_ref, pm_ref, zl_ref, wpu_ref, wlu_ref, wo_ref, o_ref):
        y_pool, y_lru, gp, gr = _mix_branches(pm_ref[...], zl_ref[...], gp_ref[...], gr_ref[...], wpu_ref, wlu_ref)
        o_ref[...] = x_ref[...] + _dot(gp * y_pool + gr * y_lru, wo_ref[...])

    return pl.pallas_call(
        body, name=name, grid=(T // tm,),
        in_specs=[pl.BlockSpec((tm, D), lambda i: (i, 0))] + _mix_out_specs(tm, D, pw, lw, cs),
        out_specs=pl.BlockSpec((tm, D), lambda i: (i, 0)),
        out_shape=jax.ShapeDtypeStruct((T, D), F32),
        compiler_params=_params(("arbitrary",)),
    )(x, proj, proj, pm, zl, wpu, wlu, wo)


def _mix_out_bwd(dxn, proj, pm, zl, wpu, wlu, wo, name):
    T, D = dxn.shape
    pw, lw, cs = pm.shape[1], zl.shape[1], wpu.shape[2]
    tm = min(TOKEN_TILE // 2, T)
    ni = T // tm

    def body(dxn_ref, gp_ref, gr_ref, pm_ref, zl_ref, wpu_ref, wlu_ref, wo_ref,
             dgl_ref, dpm_ref, dzl_ref, dwo_hbm, dwpu_hbm, dwlu_hbm, acc_o, acc_pu, acc_lu, sem):
        i = pl.program_id(0)
        pm, zl = pm_ref[...], zl_ref[...]
        y_pool, y_lru, gp, gr = _mix_branches(pm, zl, gp_ref[...], gr_ref[...], wpu_ref, wlu_ref)
        dxn_t = dxn_ref[...].astype(MM)
        dmerged = _dot_nt(dxn_t, wo_ref[...])
        _accumulate(acc_o, _dot_tn(gp * y_pool + gr * y_lru, dxn_t), i == 0)
        dgl_ref[:, :D] = (dmerged * y_pool * (gp * (1.0 - gp))).astype(MM)
        dgl_ref[:, D:] = (dmerged * y_lru * (gr * (1.0 - gr))).astype(MM)
        dyp = (dmerged * gp).astype(MM)
        dyl = (dmerged * gr).astype(MM)
        dpm = None
        for k in range(N_CHIPS):
            dyp_k = dyp[:, k * cs:(k + 1) * cs]
            part = _dot_nt(dyp_k, wpu_ref[k])
            dpm = part if dpm is None else dpm + part
            _accumulate(acc_pu.at[k], _dot_tn(pm, dyp_k), i == 0)
        dpm_ref[...] = dpm
        dzl_ref[...] = _dot_nt(dyl, wlu_ref[...])
        _accumulate(acc_lu, _dot_tn(zl, dyl), i == 0)

        @pl.when(i == ni - 1)
        def _():
            copies = [pltpu.make_async_copy(acc_o, dwo_hbm, sem.at[0]), pltpu.make_async_copy(acc_pu, dwpu_hbm, sem.at[1]),
                      pltpu.make_async_copy(acc_lu, dwlu_hbm, sem.at[2])]
            for cp in copies:
                cp.start()
            for cp in copies:
                cp.wait()

    return pl.pallas_call(
        body, name=name, grid=(ni,),
        in_specs=[pl.BlockSpec((tm, D), lambda i: (i, 0))] + _mix_out_specs(tm, D, pw, lw, cs),
        out_specs=[pl.BlockSpec((tm, 2 * D), lambda i: (i, 0)), pl.BlockSpec((tm, pw), lambda i: (i, 0)),
                   pl.BlockSpec((tm, lw), lambda i: (i, 0)), ANY, ANY, ANY],
        out_shape=[jax.ShapeDtypeStruct((T, 2 * D), MM), jax.ShapeDtypeStruct((T, pw), F32),
                   jax.ShapeDtypeStruct((T, lw), F32), jax.ShapeDtypeStruct((D, D), F32),
                   jax.ShapeDtypeStruct((N_CHIPS, pw, cs), F32), jax.ShapeDtypeStruct((lw, D), F32)],
        scratch_shapes=[pltpu.VMEM((D, D), F32), pltpu.VMEM((N_CHIPS, pw, cs), F32), pltpu.VMEM((lw, D), F32),
                        pltpu.SemaphoreType.DMA((3,))],
        compiler_params=_params(("arbitrary",)),
    )(dxn, proj, proj, pm, zl, wpu, wlu, wo)


def _shifted(buf, val, shift, S):
    buf[pl.ds(SHIFT_PAD, S), :] = val
    return buf[pl.ds(SHIFT_PAD - shift, S), :]


def _zero_pads(buf, S):
    z = jnp.zeros((SHIFT_PAD, buf.shape[1]), F32)
    buf[pl.ds(0, SHIFT_PAD), :] = z
    buf[pl.ds(SHIFT_PAD + S, SHIFT_PAD), :] = z


def _window_sum(buf, val, window, S, lead=False):
    acc, width = val, 1
    while width < window:
        acc = acc + _shifted(buf, acc, -width if lead else width, S)
        width *= 2
    return acc


def _pool_count(S, window):
    t = lax.broadcasted_iota(jnp.int32, (S, LANES), 0)
    return jnp.minimum(t + 1, window).astype(F32)


def _pool_fwd_group(u, g, window, pw_ref, pb_ref, buf, S):
    pooled = _window_sum(buf, u, window, S) / _pool_count(S, window) - u
    return pooled, _dot(pooled, pw_ref[g]) + pb_ref[g]


def _pool_fwd(proj, pool_w, pool_b, pool_scale, name):
    S = proj.shape[0]
    G = pool_w.shape[0]
    pw = G * LANES

    def body(u_ref, pw_ref, pb_ref, ps_ref, pm_ref, buf):
        _zero_pads(buf, S)
        for g in range(G):
            cols = slice(g * LANES, (g + 1) * LANES)
            _, mixed = _pool_fwd_group(u_ref[:, cols], g, POOL_WINDOWS[g], pw_ref, pb_ref, buf, S)
            pm_ref[:, cols] = (mixed * ps_ref[:, cols]).astype(MM)

    return pl.pallas_call(
        body, name=name, grid=(1,),
        in_specs=[pl.BlockSpec((S, pw), lambda i: (0, 0)), pl.BlockSpec((G, LANES, LANES), lambda i: (0, 0, 0)),
                  pl.BlockSpec((G, 1, LANES), lambda i: (0, 0, 0)), pl.BlockSpec((1, pw), lambda i: (0, 0))],
        out_specs=pl.BlockSpec((S, pw), lambda i: (0, 0)),
        out_shape=jax.ShapeDtypeStruct((S, pw), MM),
        scratch_shapes=[pltpu.VMEM((S + 2 * SHIFT_PAD, LANES), F32)],
        compiler_params=_params(("arbitrary",)),
    )(proj, pool_w, pool_b, pool_scale)


def _pool_bwd(proj, dpm, pool_w, pool_b, pool_scale, name):
    S = proj.shape[0]
    G = pool_w.shape[0]
    pw = G * LANES

    def body(u_ref, dpm_ref, pw_ref, pb_ref, ps_ref, du_ref, dpw_ref, dpb_ref, dps_ref, buf):
        _zero_pads(buf, S)
        for g in range(G):
            cols = slice(g * LANES, (g + 1) * LANES)
            window = POOL_WINDOWS[g]
            pooled, mixed = _pool_fwd_group(u_ref[:, cols], g, window, pw_ref, pb_ref, buf, S)
            d_out = dpm_ref[:, cols]
            dmixed = d_out * ps_ref[:, cols]
            dps_ref[:, cols] = jnp.sum(d_out * mixed, axis=0, keepdims=True)
            dpb_ref[g] = jnp.sum(dmixed, axis=0, keepdims=True)
            dpw_ref[g] = _dot_tn(pooled, dmixed)
            dpooled = _dot_nt(dmixed, pw_ref[g])
            q = dpooled / _pool_count(S, window)
            du_ref[:, cols] = (_window_sum(buf, q, window, S, lead=True) - dpooled).astype(MM)

    return pl.pallas_call(
        body, name=name, grid=(1,),
        in_specs=[pl.BlockSpec((S, pw), lambda i: (0, 0)), pl.BlockSpec((S, pw), lambda i: (0, 0)),
                  pl.BlockSpec((G, LANES, LANES), lambda i: (0, 0, 0)),
                  pl.BlockSpec((G, 1, LANES), lambda i: (0, 0, 0)), pl.BlockSpec((1, pw), lambda i: (0, 0))],
        out_specs=[pl.BlockSpec((S, pw), lambda i: (0, 0)), pl.BlockSpec((G, LANES, LANES), lambda i: (0, 0, 0)),
                   pl.BlockSpec((G, 1, LANES), lambda i: (0, 0, 0)), pl.BlockSpec((1, pw), lambda i: (0, 0))],
        out_shape=[jax.ShapeDtypeStruct((S, pw), MM), jax.ShapeDtypeStruct((G, LANES, LANES), F32),
                   jax.ShapeDtypeStruct((G, 1, LANES), F32), jax.ShapeDtypeStruct((1, pw), F32)],
        scratch_shapes=[pltpu.VMEM((S + 2 * SHIFT_PAD, LANES), F32)],
        compiler_params=_params(("arbitrary",)),
    )(proj, dpm, pool_w, pool_b, pool_scale)


def _scan(a, b, bufs, S, reverse=False):
    pad = S // 2
    seq = pl.ds(pad, S)
    A, B = bufs[:2], bufs[2:]
    A[0][seq, :] = a
    B[0][seq, :] = b
    cur, d = 0, 1
    while d < S:
        sh = pl.ds(pad + d if reverse else pad - d, S)
        a_c = A[cur][seq, :]
        B[1 - cur][seq, :] = a_c * B[cur][sh, :] + B[cur][seq, :]
        if 2 * d < S:
            A[1 - cur][seq, :] = a_c * A[cur][sh, :]
        cur, d = 1 - cur, 2 * d
    return B[cur][seq, :]


def _init_scan_pads(bufs, S):
    pad = S // 2
    for n, buf in enumerate(bufs):
        fill = jnp.full((pad, LANES), 1.0 if n < 2 else 0.0, F32)
        buf[pl.ds(0, pad), :] = fill
        buf[pl.ds(pad + S, pad), :] = fill


def _gelu_and_grad(x):
    c = 0.7978845608028654
    x2 = x * x
    th = jnp.tanh(c * (x + 0.044715 * (x2 * x)))
    gelu = 0.5 * x * (1.0 + th)
    grad = 0.5 * (1.0 + th) + 0.5 * x * (1.0 - th * th) * (c * (1.0 + 3 * 0.044715 * x2))
    return gelu, grad


def _lru_head_fwd(ul, cw, cb, wa, ba, wx, bx, lam, sbuf, bufs, S):
    sbuf[pl.ds(SHIFT_PAD, S), :] = ul
    v = None
    for k in range(CONV_WIDTH):
        term = sbuf[pl.ds(SHIFT_PAD - (CONV_WIDTH - 1 - k), S), :] * cw[k:k + 1, :]
        v = term if v is None else v + term
    v = v + cb
    r = jax.nn.sigmoid(_dot(v, wa) + ba)
    ig = jax.nn.sigmoid(_dot(v, wx) + bx)
    sp = jax.nn.softplus(-lam)
    log_a = -LRU_C * r * sp
    a = jnp.exp(log_a)
    mult = jnp.sqrt(-jnp.tanh(log_a) * (1.0 + a * a))
    h = _scan(a, mult * (ig * v), bufs, S)
    return v, r, ig, sp, a, mult, h


def _lru_specs(S, H, lw, pw):
    b0 = pw // LANES
    return [pl.BlockSpec((S, LANES), lambda h: (0, b0 + h)), pl.BlockSpec((S, LANES), lambda h: (0, b0 + H + h)),
            pl.BlockSpec((CONV_WIDTH, LANES), lambda h: (0, h)), pl.BlockSpec((1, LANES), lambda h: (0, h)),
            pl.BlockSpec((None, LANES, LANES), lambda h: (h, 0, 0)), pl.BlockSpec((None, 1, LANES), lambda h: (h, 0, 0)),
            pl.BlockSpec((None, LANES, LANES), lambda h: (h, 0, 0)), pl.BlockSpec((None, 1, LANES), lambda h: (h, 0, 0)),
            pl.BlockSpec((1, LANES), lambda h: (0, h))]


def _lru_scratch(S):
    return [pltpu.VMEM((S + 2 * SHIFT_PAD, LANES), F32)] + [pltpu.VMEM((2 * S, LANES), F32)] * 4


def _lru_fwd(proj, conv_w, conv_b, wa, ba, wx, bx, lam, pw, name):
    S = proj.shape[0]
    H = wa.shape[0]
    lw = H * LANES

    def body(ul_ref, ug_ref, cw_ref, cb_ref, wa_ref, ba_ref, wx_ref, bx_ref, lam_ref, zl_ref, sbuf, *bufs):
        @pl.when(pl.program_id(0) == 0)
        def _():
            _zero_pads(sbuf, S)
            _init_scan_pads(bufs, S)

        h = _lru_head_fwd(ul_ref[...], cw_ref[...], cb_ref[...], wa_ref[...], ba_ref[...], wx_ref[...], bx_ref[...],
                          lam_ref[...], sbuf, bufs, S)[-1]
        zl_ref[...] = (h * jax.nn.gelu(ug_ref[...])).astype(MM)

    return pl.pallas_call(
        body, name=name, grid=(H,),
        in_specs=_lru_specs(S, H, lw, pw),
        out_specs=pl.BlockSpec((S, LANES), lambda h: (0, h)),
        out_shape=jax.ShapeDtypeStruct((S, lw), MM),
        scratch_shapes=_lru_scratch(S),
        compiler_params=_params(("arbitrary",)),
    )(proj, proj, conv_w, conv_b, wa, ba, wx, bx, lam)


def _lru_bwd(proj, dzl, conv_w, conv_b, wa, ba, wx, bx, lam, pw, name):
    S = proj.shape[0]
    H = wa.shape[0]
    lw = H * LANES

    def body(ul_ref, ug_ref, cw_ref, cb_ref, wa_ref, ba_ref, wx_ref, bx_ref, lam_ref, dzl_ref,
             dul_ref, dug_ref, dcw_ref, dcb_ref, dwa_ref, dba_ref, dwx_ref, dbx_ref, dlam_ref, sbuf, *bufs):
        @pl.when(pl.program_id(0) == 0)
        def _():
            _zero_pads(sbuf, S)
            _init_scan_pads(bufs, S)

        ul, cw, lam = ul_ref[...], cw_ref[...], lam_ref[...]
        wa, wx = wa_ref[...], wx_ref[...]
        v, r, ig, sp, a, mult, h = _lru_head_fwd(ul, cw, cb_ref[...], wa, ba_ref[...], wx, bx_ref[...], lam, sbuf, bufs, S)
        gelu, gelu_grad = _gelu_and_grad(ug_ref[...])
        dzl_t = dzl_ref[...]
        dug_ref[...] = (dzl_t * h * gelu_grad).astype(MM)
        a_next = _shifted(sbuf, a, -1, S)
        lam_t = _scan(a_next, dzl_t * gelu, bufs, S, reverse=True)
        da = lam_t * _shifted(sbuf, h, 1, S)
        d_iv = lam_t * mult
        d_log_a = da * a - (lam_t * (ig * v)) * (a * a) / mult
        dlam_ref[...] = jnp.sum(d_log_a * r, axis=0, keepdims=True) * (LRU_C * jax.nn.sigmoid(-lam))
        dra = (d_log_a * (-LRU_C * sp)) * (r * (1.0 - r))
        dia = (d_iv * v) * (ig * (1.0 - ig))
        dba_ref[...] = jnp.sum(dra, axis=0, keepdims=True)
        dbx_ref[...] = jnp.sum(dia, axis=0, keepdims=True)
        dwa_ref[...] = _dot_tn(v, dra)
        dwx_ref[...] = _dot_tn(v, dia)
        dv = d_iv * ig + _dot_nt(dra, wa) + _dot_nt(dia, wx)
        dcb_ref[...] = jnp.sum(dv, axis=0, keepdims=True)
        sbuf[pl.ds(SHIFT_PAD, S), :] = ul
        for k in range(CONV_WIDTH):
            dcw_ref[k:k + 1, :] = jnp.sum(dv * sbuf[pl.ds(SHIFT_PAD - (CONV_WIDTH - 1 - k), S), :], axis=0, keepdims=True)
        sbuf[pl.ds(SHIFT_PAD, S), :] = dv
        dul = None
        for k in range(CONV_WIDTH):
            term = sbuf[pl.ds(SHIFT_PAD + (CONV_WIDTH - 1 - k), S), :] * cw[k:k + 1, :]
            dul = term if dul is None else dul + term
        dul_ref[...] = dul.astype(MM)

    head_mat = pl.BlockSpec((None, LANES, LANES), lambda h: (h, 0, 0))
    head_vec = pl.BlockSpec((None, 1, LANES), lambda h: (h, 0, 0))
    col = pl.BlockSpec((S, LANES), lambda h: (0, h))
    row = pl.BlockSpec((1, LANES), lambda h: (0, h))
    return pl.pallas_call(
        body, name=name, grid=(H,),
        in_specs=_lru_specs(S, H, lw, pw) + [col],
        out_specs=[col, col, pl.BlockSpec((CONV_WIDTH, LANES), lambda h: (0, h)), row, head_mat, head_vec, head_mat,
                   head_vec, row],
        out_shape=[jax.ShapeDtypeStruct((S, lw), MM), jax.ShapeDtypeStruct((S, lw), MM),
                   jax.ShapeDtypeStruct((CONV_WIDTH, lw), F32), jax.ShapeDtypeStruct((1, lw), F32),
                   jax.ShapeDtypeStruct((H, LANES, LANES), F32), jax.ShapeDtypeStruct((H, 1, LANES), F32),
                   jax.ShapeDtypeStruct((H, LANES, LANES), F32), jax.ShapeDtypeStruct((H, 1, LANES), F32),
                   jax.ShapeDtypeStruct((1, lw), F32)],
        scratch_shapes=_lru_scratch(S),
        compiler_params=_params(("arbitrary",)),
    )(proj, proj, conv_w, conv_b, wa, ba, wx, bx, lam, dzl)


def _loss_head(x, gain, target, name):
    T, D = x.shape
    tm = min(TOKEN_TILE, T)
    ni = T // tm

    def body(x_ref, g_ref, t_ref, loss_ref, dx_ref, dg_ref, loss_acc, dg_acc):
        i = pl.program_id(0)
        g = g_ref[...]
        r, xh, y = _rms(x_ref[...], g)
        err = y - t_ref[...]
        part = 0.5 * jnp.sum(jnp.mean(err * err, axis=-1, keepdims=True), axis=0, keepdims=True)
        _accumulate(loss_acc, jnp.broadcast_to(part, (1, LANES)), i == 0)
        dy = err * (1.0 / D)
        _accumulate(dg_acc, jnp.sum(dy * xh, axis=0, keepdims=True), i == 0)
        dx_ref[...] = _rms_bwd(dy, xh, r, g)

        @pl.when(i == ni - 1)
        def _():
            loss_ref[...] = loss_acc[...]
            dg_ref[...] = dg_acc[...]

    return pl.pallas_call(
        body, name=name, grid=(ni,),
        in_specs=[pl.BlockSpec((tm, D), lambda i: (i, 0)), pl.BlockSpec((1, D), lambda i: (0, 0)),
                  pl.BlockSpec((tm, D), lambda i: (i, 0))],
        out_specs=[pl.BlockSpec((1, LANES), lambda i: (0, 0)), pl.BlockSpec((tm, D), lambda i: (i, 0)),
                   pl.BlockSpec((1, D), lambda i: (0, 0))],
        out_shape=[jax.ShapeDtypeStruct((1, LANES), F32), jax.ShapeDtypeStruct((T, D), F32),
                   jax.ShapeDtypeStruct((1, D), F32)],
        scratch_shapes=[pltpu.VMEM((1, LANES), F32), pltpu.VMEM((1, D), F32)],
        compiler_params=_params(("arbitrary",)),
    )(x, gain, target)


def _my_core():
    return lax.axis_index("c")


def _my_chip():
    return 2 * lax.axis_index("x") + lax.axis_index("y")


def _pair_sum(a, recv, xfer, name):
    n, R, C = a.shape
    hr = R // 2

    def body(a_ref, r_ref, pf_ref, pb_ref):
        s = a_ref[...] + r_ref[...]
        pf_ref[...] = s
        pb_ref[...] = s.astype(xfer)

    piece = pl.BlockSpec((None, hr, C), lambda k: (k, 0, 0))
    return pl.pallas_call(
        body, name=name, grid=(n,),
        in_specs=[pl.BlockSpec((None, hr, C), lambda k: (k, _my_core(), 0)), piece],
        out_specs=[piece, piece],
        out_shape=[jax.ShapeDtypeStruct((n, hr, C), F32), jax.ShapeDtypeStruct((n, hr, C), xfer)],
        compiler_params=_params(("arbitrary",)),
    )(a, recv)


def _chip_sum(pf, others, name):
    _, hr, C = pf.shape
    rb = _row_block(hr, C)
    nb = hr // rb

    def body(o_ref, q_ref, g_ref):
        g_ref[...] = ((o_ref[...] + q_ref[0].astype(F32)) + q_ref[1].astype(F32)) + q_ref[2].astype(F32)

    return pl.pallas_call(
        body, name=name, grid=(nb,),
        in_specs=[pl.BlockSpec((None, rb, C), lambda i: (_my_chip(), i, 0)),
                  pl.BlockSpec((N_CHIPS - 1, rb, C), lambda i: (0, i, 0))],
        out_specs=pl.BlockSpec((rb, C), lambda i: (_my_core() * nb + i, 0)),
        out_shape=jax.ShapeDtypeStruct((2 * hr, C), F32),
        compiler_params=_params(("arbitrary",)),
    )(pf, others)


def _adamw_step(w_ref, g_ref, m_ref, v_ref, go_ref, d_ref, mo_ref, vo_ref):
    g_t = g_ref[...]
    m_t = ADAM_B1 * m_ref[...] + (1.0 - ADAM_B1) * g_t
    v_t = ADAM_B2 * v_ref[...] + (1.0 - ADAM_B2) * (g_t * g_t)
    m_hat = m_t / (1.0 - ADAM_B1 ** ADAM_STEP)
    v_hat = v_t / (1.0 - ADAM_B2 ** ADAM_STEP)
    go_ref[...] = g_t
    d_ref[...] = -ADAM_LR * (m_hat / (jnp.sqrt(v_hat) + ADAM_EPS) + ADAM_WD * w_ref[...])
    mo_ref[...] = m_t
    vo_ref[...] = v_t


def _adamw(w, g, m, v, name):
    R, C = w.shape
    rb = _row_block(R, C)
    blk = pl.BlockSpec((rb, C), lambda i: (i, 0))
    return pl.pallas_call(
        functools.partial(_adamw_step), name=name, grid=(R // rb,), in_specs=[blk] * 4, out_specs=[blk] * 4,
        out_shape=[jax.ShapeDtypeStruct((R, C), F32)] * 4,
        compiler_params=_params(("arbitrary",)),
    )(w, g, m, v)


def _adamw_layer(w, g, m, v, outs, layer, name):
    L, R, C = w.shape
    rb = _row_block(R, C)

    def body(w_ref, g_ref, m_ref, v_ref, *rest):
        _adamw_step(w_ref, g_ref, m_ref, v_ref, *rest[4:])

    stacked = pl.BlockSpec((None, rb, C), lambda i: (layer, i, 0))
    return pl.pallas_call(
        body, name=name, grid=(R // rb,),
        in_specs=[stacked, pl.BlockSpec((rb, C), lambda i: (i, 0)), stacked, stacked] + [ANY] * 4,
        out_specs=[stacked] * 4,
        out_shape=[jax.ShapeDtypeStruct((L, R, C), F32)] * 4,
        input_output_aliases={4 + j: j for j in range(4)},
        compiler_params=_params(("arbitrary",)),
    )(w, g, m, v, *outs)


def _place():
    x, y, c = lax.axis_index("x"), lax.axis_index("y"), lax.axis_index("c")
    others = [(1 - x, y), (x, 1 - y), (1 - x, 1 - y)]
    return x, y, c, 2 * x + y, others


def _half(c, rows):
    return pl.ds(pl.multiple_of(c * (rows // 2), 16), rows // 2)


HBM_SPEC = pl.BlockSpec(memory_space=pltpu.HBM)
SEM_SPEC = pl.BlockSpec(memory_space=pltpu.SEMAPHORE)
SPLIT = pltpu.CompilerParams(has_side_effects=pltpu.SideEffectType.DATAFLOW_SIDE_EFFECTING)
TOKEN = jax.ShapeDtypeStruct((8, LANES), F32)


def _in_hbm(a):
    return pltpu.with_memory_space_constraint(a, pltpu.HBM)


def _split_start(build, count, srcs, lands, after, name):
    ns, nl = len(srcs), len(lands)

    def body(*refs):
        for started, _ in build(refs[:ns], refs[ns:ns + nl], refs[ns + nl + 1], refs[ns + nl + 2]):
            started.start()
        refs[-1][...] = jnp.zeros_like(refs[-1])

    res = pl.pallas_call(
        body, name=name,
        in_specs=[HBM_SPEC] * (ns + nl) + [ANY],
        out_specs=[SEM_SPEC] * 2 + [HBM_SPEC] * nl + [pl.BlockSpec(memory_space=pltpu.VMEM)],
        out_shape=[pltpu.SemaphoreType.DMA((count,))] * 2 + [pltpu.HBM(z.shape, z.dtype) for z in lands] + [TOKEN],
        input_output_aliases={ns + a: 2 + a for a in range(nl)},
        compiler_params=SPLIT,
    )(*[_in_hbm(s) for s in srcs], *[_in_hbm(z) for z in lands], after)
    return res[:2], res[2:2 + nl], res[-1]


def _split_wait(build, srcs, lands, sems, after, name):
    ns, nl = len(srcs), len(lands)

    def body(*refs):
        for started, arriving in build(refs[:ns], refs[ns:ns + nl], refs[ns + nl], refs[ns + nl + 1]):
            started.wait_send()
            arriving.wait_recv()

    return pl.pallas_call(
        body, name=name,
        in_specs=[HBM_SPEC] * (ns + nl) + [SEM_SPEC] * 2 + [ANY],
        out_specs=[HBM_SPEC] * nl,
        out_shape=[pltpu.HBM(z.shape, z.dtype) for z in lands],
        input_output_aliases={ns + a: a for a in range(nl)},
        compiler_params=SPLIT,
    )(*[_in_hbm(s) for s in srcs], *lands, *sems, after)


def _gather_copies(srcs, lands, send, recv, layer):
    x, y, c, k, others = _place()
    pairs = []
    for a in range(len(srcs)):
        rows = _half(c, srcs[a].shape[-2])
        for r, (px, py) in enumerate(others):
            def ici(slot, a=a, r=r, px=px, py=py, rows=rows):
                return pltpu.make_async_remote_copy(src_ref=srcs[a].at[layer].at[rows], dst_ref=lands[a].at[slot].at[rows],
                                                    send_sem=send.at[a * 4 + r], recv_sem=recv.at[a * 4 + r],
                                                    device_id=(px, py, c), device_id_type=MESH)
            pairs.append((ici(k), ici(2 * px + py)))
        own = pltpu.make_async_remote_copy(src_ref=srcs[a].at[layer], dst_ref=lands[a].at[k], send_sem=send.at[a * 4 + 3],
                                           recv_sem=recv.at[a * 4 + 3], device_id=(x, y, 1 - c), device_id_type=MESH)
        pairs.append((own, own))
    return pairs


def _halves_copies(srcs, lands, send, recv):
    x, y, c, _, _ = _place()
    pairs = []
    for a in range(len(srcs)):
        cp = pltpu.make_async_remote_copy(src_ref=srcs[a].at[:, _half(1 - c, srcs[a].shape[1]), :], dst_ref=lands[a],
                                          send_sem=send.at[a], recv_sem=recv.at[a], device_id=(x, y, 1 - c),
                                          device_id_type=MESH)
        pairs.append((cp, cp))
    return pairs


def _chip_copies(srcs, lands, send, recv):
    x, y, c, k, others = _place()
    pairs = []
    for a in range(len(srcs)):
        for r, (px, py) in enumerate(others):
            cp = pltpu.make_async_remote_copy(src_ref=srcs[a].at[2 * px + py], dst_ref=lands[a].at[r],
                                              send_sem=send.at[a * 3 + r], recv_sem=recv.at[a * 3 + r],
                                              device_id=(px, py, c), device_id_type=MESH)
            pairs.append((cp, cp))
    return pairs


def _forward_copies(srcs, lands, send, recv):
    x, y, c, _, others = _place()
    pairs = []
    for a in range(len(lands)):
        for r, (px, py) in enumerate(others):
            def copy(half, a=a, r=r, px=px, py=py):
                rows = lands[a].at[2 * px + py].at[_half(half, lands[a].shape[-2])]
                return pltpu.make_async_remote_copy(src_ref=rows, dst_ref=rows, send_sem=send.at[a * 3 + r],
                                                    recv_sem=recv.at[a * 3 + r], device_id=(x, y, 1 - c),
                                                    device_id_type=MESH)
            pairs.append((copy(c), copy(1 - c)))
    return pairs


def _join_copies(srcs, lands, send, recv):
    x, y, c, _, _ = _place()
    pairs = []
    for a in range(len(lands)):
        def copy(half, a=a):
            rows = lands[a].at[_half(half, lands[a].shape[0]), :]
            return pltpu.make_async_remote_copy(src_ref=rows, dst_ref=rows, send_sem=send.at[a], recv_sem=recv.at[a],
                                                device_id=(x, y, 1 - c), device_id_type=MESH)
        pairs.append((copy(c), copy(1 - c)))
    return pairs


def _gather_small(small, name):
    def body(in_ref, out_ref, send, recv):
        x, y, c, k, others = _place()
        mine = pltpu.make_async_remote_copy(src_ref=in_ref, dst_ref=out_ref.at[k], send_sem=send.at[3], recv_sem=recv.at[3],
                                            device_id=(x, y, 1 - c), device_id_type=MESH)
        mine.start()
        copies = []
        for r, (px, py) in enumerate(others):
            copies.append(pltpu.make_async_remote_copy(src_ref=in_ref, dst_ref=out_ref.at[k], send_sem=send.at[r],
                                                       recv_sem=recv.at[r], device_id=(px, py, c), device_id_type=MESH))
            copies[-1].start()
        for r, (px, py) in enumerate(others):
            pltpu.make_async_remote_copy(src_ref=in_ref, dst_ref=out_ref.at[2 * px + py], send_sem=send.at[r],
                                         recv_sem=recv.at[r], device_id=(px, py, c), device_id_type=MESH).wait_recv()
        for cp in copies:
            cp.wait_send()
        mine.wait()

    return pl.pallas_call(
        body, name=name, in_specs=[ANY], out_specs=ANY,
        out_shape=jax.ShapeDtypeStruct((N_CHIPS,) + small.shape, small.dtype),
        scratch_shapes=[pltpu.SemaphoreType.DMA((4,))] * 2,
    )(small)


GROUPS = (("ffn1_w_up", "ffn1_w_down"), ("w_in", "w_pool_up", "w_lru_up", "w_out"), ("ffn2_w_up", "ffn2_w_down"))
BIG = GROUPS[0] + GROUPS[1] + GROUPS[2]
SMALL_LAYER = ("norm_ffn1", "norm_mix", "pool_w", "pool_b", "pool_scale", "conv_w", "conv_b", "lru_w_a", "lru_b_a",
               "lru_w_x", "lru_b_x", "lru_lambda", "norm_ffn2")
SMALL = SMALL_LAYER + ("final_norm",)
GATHER_AHEAD = 3
WEIGHTS = ("norm_ffn1", "ffn1_w_up", "ffn1_w_down", "norm_mix", "w_in", "pool_w", "pool_b", "pool_scale", "w_pool_up",
           "conv_w", "conv_b", "lru_w_a", "lru_b_a", "lru_w_x", "lru_b_x", "lru_lambda", "w_lru_up", "w_out", "norm_ffn2",
           "ffn2_w_up", "ffn2_w_down", "final_norm")


def _pack(arrays, rows_multiple):
    flat = jnp.concatenate([a.reshape(-1) for a in arrays])
    rows = -(-flat.shape[0] // LANES)
    rows = -(-rows // rows_multiple) * rows_multiple
    return jnp.pad(flat, (0, rows * LANES - flat.shape[0])).reshape(rows, LANES)


def _unpack(packed, like):
    flat, out, at = packed.reshape(-1), [], 0
    for a in like:
        out.append(flat[at:at + a.size].reshape(a.shape))
        at += a.size
    return out


def kernel(x, norm_ffn1, ffn1_w_up, ffn1_w_down, norm_mix, w_in, pool_w, pool_b, pool_scale, w_pool_up, conv_w, conv_b, lru_w_a, lru_b_a, lru_w_x, lru_b_x, lru_lambda, w_lru_up, w_out, norm_ffn2, ffn2_w_up, ffn2_w_down, final_norm, loss_target, m_norm_ffn1, m_ffn1_w_up, m_ffn1_w_down, m_norm_mix, m_w_in, m_pool_w, m_pool_b, m_pool_scale, m_w_pool_up, m_conv_w, m_conv_b, m_lru_w_a, m_lru_b_a, m_lru_w_x, m_lru_b_x, m_lru_lambda, m_w_lru_up, m_w_out, m_norm_ffn2, m_ffn2_w_up, m_ffn2_w_down, m_final_norm, v_norm_ffn1, v_ffn1_w_up, v_ffn1_w_down, v_norm_mix, v_w_in, v_pool_w, v_pool_b, v_pool_scale, v_w_pool_up, v_conv_w, v_conv_b, v_lru_w_a, v_lru_b_a, v_lru_w_x, v_lru_b_x, v_lru_lambda, v_w_lru_up, v_w_out, v_norm_ffn2, v_ffn2_w_up, v_ffn2_w_down, v_final_norm):
    given = dict(locals())
    W = {n: given[n] for n in WEIGHTS}
    M = {n: given["m_" + n] for n in WEIGHTS}
    V = {n: given["v_" + n] for n in WEIGHTS}
    L = norm_ffn1.shape[0]
    T, D = x.shape[1], x.shape[2]
    G, H = pool_w.shape[1], lru_w_a.shape[1]
    pw, lw = G * LANES, H * LANES
    chip = 2 * lax.axis_index("x") + lax.axis_index("y")

    def tied(gain, token):
        return gain if token is None else gain + token[:1, :1]

    shards = {n: W[n].astype(MM) for n in BIG}
    conv_full = _gather_small(conv_w.reshape(L * CONV_WIDTH, conv_w.shape[2]), "gather_conv_w")
    conv_full = conv_full.reshape(N_CHIPS, L, CONV_WIDTH, -1).transpose(1, 2, 0, 3).reshape(L, CONV_WIDTH, lw)

    order = [(l, g) for l in range(L) for g in range(len(GROUPS))]
    started = {}

    def start_gather(i, after):
        l, g = order[i]
        srcs = [shards[n] for n in GROUPS[g]]
        lands = [lax.empty((N_CHIPS,) + s.shape[1:], s.dtype) for s in srcs]
        build = functools.partial(_gather_copies, layer=l)
        started[i] = (build, srcs) + _split_start(build, 4 * len(srcs), srcs, lands, after, f"gather_start_{l}_{g}")
        return started[i][-1]

    passing = {}

    def pass_on(i, after):
        l, g = order[i]
        build, srcs, sems, lands, _ = started.pop(i)
        lands = _split_wait(build, srcs, lands, sems, after, f"gather_wait_{l}_{g}")
        sems, lands, token = _split_start(_forward_copies, 3 * len(lands), [], lands, after, f"forward_start_{l}_{g}")
        passing[i] = (sems, lands)
        return token

    def finish_gather(i, after):
        l, g = order[i]
        sems, lands = passing.pop(i)
        got = _split_wait(_forward_copies, [], lands, sems, after, f"forward_wait_{l}_{g}")
        token = pass_on(i + 1, got[0]) if i + 1 < len(order) else None
        if i + GATHER_AHEAD < len(order):
            token = start_gather(i + GATHER_AHEAD, got[0] if token is None else token)
        return got, token

    token = conv_full
    for i in range(min(GATHER_AHEAD, len(order))):
        token = start_gather(i, token)
    token = pass_on(0, token)

    xs = x.reshape(T, D)
    saved = []
    for l in range(L):
        small = dict(pool_w=pool_w[l], pool_b=pool_b[l].reshape(G, 1, LANES), pool_scale=pool_scale[l].reshape(1, pw),
                     conv_w=conv_full[l], conv_b=conv_b[l].reshape(1, lw), wa=lru_w_a[l],
                     ba=lru_b_a[l].reshape(H, 1, LANES), wx=lru_w_x[l], bx=lru_b_x[l].reshape(H, 1, LANES),
                     lam=lru_lambda[l].reshape(1, lw))
        g1, g2, g3 = norm_ffn1[l].reshape(1, D), norm_mix[l].reshape(1, D), norm_ffn2[l].reshape(1, D)

        x0 = xs
        (w_up1, w_dn1), token = finish_gather(3 * l, token if l == 0 else x0)
        w_dn1 = w_dn1.reshape(-1, D)
        u1 = _norm_matmul_fwd(x0, tied(g1, token), w_up1, f"ffn1_up_{l}")
        x1 = _swiglu_down_fwd(u1, x0, w_dn1, f"ffn1_down_{l}")
        (w_i, w_pu, w_lu, w_o), token = finish_gather(3 * l + 1, x1)
        w_lu, w_o = w_lu.reshape(-1, D), w_o.reshape(-1, D)
        proj = _norm_matmul_fwd(x1, tied(g2, token), w_i, f"mix_in_{l}")
        pm = _pool_fwd(proj, small["pool_w"], small["pool_b"], small["pool_scale"], f"pool_{l}")
        zl = _lru_fwd(proj, small["conv_w"], small["conv_b"], small["wa"], small["ba"], small["wx"], small["bx"],
                      small["lam"], pw, f"lru_{l}")
        x2 = _mix_out_fwd(x1, proj, pm, zl, w_pu, w_lu, w_o, f"mix_out_{l}")
        (w_up2, w_dn2), token = finish_gather(3 * l + 2, x2)
        w_dn2 = w_dn2.reshape(-1, D)
        u2 = _norm_matmul_fwd(x2, tied(g3, token), w_up2, f"ffn2_up_{l}")
        xs = _swiglu_down_fwd(u2, x2, w_dn2, f"ffn2_down_{l}")
        saved.append(dict(x0=x0, u1=u1, x1=x1, proj=proj, pm=pm, zl=zl, x2=x2, u2=u2, small=small, g=(g1, g2, g3),
                          w=(w_up1, w_dn1, w_i, w_pu, w_lu, w_o, w_up2, w_dn2)))

    loss_part, dx, d_final = _loss_head(xs, final_norm.reshape(1, D), loss_target.reshape(T, D), "loss_head")
    loss = lax.psum(loss_part[0, 0], ("x", "y", "c"))

    du_spec_ffn = lambda tm, tn: pl.BlockSpec((None, tm, tn), lambda k, i: (k // 2, i, k % 2))
    du_spec_mix = lambda tm, tn: pl.BlockSpec((tm, tn), lambda k, i: (i, k))
    out = {n: [lax.empty(W[n].shape, F32) for _ in range(4)] for n in BIG}
    small_pieces = [None] * L
    stages = [None, None, None]

    def advance(new):
        token = None
        first = None
        if new is not None:
            l, names, grads = new
            tag = f"{names[0]}_{l}"
            lands = [lax.empty((N_CHIPS, a.shape[1] // 2, a.shape[2]), F32) for a in grads]
            sems, lands, token = _split_start(_halves_copies, len(grads), grads, lands, grads[0], f"halves_start_{tag}")
            first = (l, names, tag, grads, sems, lands)
        second = None
        if stages[0] is not None:
            l, names, tag, grads, sems, lands = stages[0]
            recv = _split_wait(_halves_copies, grads, lands, sems, grads[0] if token is None else token,
                               f"halves_wait_{tag}")
            sums = [_pair_sum(g, r, F32 if n == "small" else XFER, f"pair_sum_{n}_{l}")
                    for g, r, n in zip(grads, recv, names)]
            pbs = [p[1] for p in sums]
            lands = [lax.empty((N_CHIPS - 1,) + p.shape[1:], p.dtype) for p in pbs]
            sems, lands, token = _split_start(_chip_copies, 3 * len(pbs), pbs, lands, pbs[0], f"chips_start_{tag}")
            second = (l, names, tag, [p[0] for p in sums], pbs, sems, lands)
        third = None
        if stages[1] is not None:
            l, names, tag, pfs, pbs, sems, lands = stages[1]
            got = _split_wait(_chip_copies, pbs, lands, sems, pbs[0] if token is None else token, f"chips_wait_{tag}")
            halves = [_chip_sum(p, q, f"chip_sum_{n}_{l}") for p, q, n in zip(pfs, got, names)]
            sems, halves, token = _split_start(_join_copies, len(halves), [], halves, pbs[0], f"join_start_{tag}")
            third = (l, names, tag, pbs[0], sems, halves)
        if stages[2] is not None:
            l, names, tag, any_array, sems, halves = stages[2]
            whole = _split_wait(_join_copies, [], halves, sems, any_array if token is None else token, f"join_wait_{tag}")
            for n, g in zip(names, whole):
                if n == "small":
                    small_pieces[l] = g
                else:
                    out[n] = _adamw_layer(W[n], g, M[n], V[n], out[n], l, f"adamw_{n}_{l}")
        stages[:] = [first, second, third]
        return token

    token = None
    for l in reversed(range(L)):
        s = saved[l]
        w_up1, w_dn1, w_i, w_pu, w_lu, w_o, w_up2, w_dn2 = s["w"]
        g1, g2, g3 = s["g"]
        sm = s["small"]
        du2, d_dn2 = _swiglu_down_bwd(s["u2"], dx, w_dn2, f"ffn2_down_bwd_{l}")
        dx, d_up2, dg3 = _norm_matmul_bwd(s["x2"], tied(g3, token), du2, du_spec_ffn, w_up2, dx, f"ffn2_up_bwd_{l}")
        token = advance((l, GROUPS[2], [d_up2, d_dn2.reshape(N_CHIPS, -1, D)]))
        dgl, dpm, dzl, d_o, d_pu, d_lu = _mix_out_bwd(dx, s["proj"], s["pm"], s["zl"], w_pu, w_lu, w_o, f"mix_out_bwd_{l}")
        du_pool, d_pool_w, d_pool_b, d_pool_scale = _pool_bwd(s["proj"], dpm, sm["pool_w"], sm["pool_b"], sm["pool_scale"],
                                                            f"pool_bwd_{l}")
        dul, dug, d_cw, d_cb, d_wa, d_ba, d_wx, d_bx, d_lam = _lru_bwd(
            s["proj"], dzl, sm["conv_w"], sm["conv_b"], sm["wa"], sm["ba"], sm["wx"], sm["bx"], sm["lam"], pw, f"lru_bwd_{l}")
        dproj = jnp.concatenate([du_pool, dul, dug, dgl], axis=1)
        dx, d_in, dg2 = _norm_matmul_bwd(s["x1"], tied(g2, token), dproj, du_spec_mix, w_i, dx, f"mix_in_bwd_{l}")
        token = advance((l, GROUPS[1], [d_in, d_pu, d_lu.reshape(N_CHIPS, -1, D), d_o.reshape(N_CHIPS, -1, D)]))
        du1, d_dn1 = _swiglu_down_bwd(s["u1"], dx, w_dn1, f"ffn1_down_bwd_{l}")
        dx, d_up1, dg1 = _norm_matmul_bwd(s["x0"], tied(g1, token), du1, du_spec_ffn, w_up1, dx, f"ffn1_up_bwd_{l}")
        small = [dg1, dg2, d_pool_w, d_pool_b, d_pool_scale, d_cw, d_cb, d_wa, d_ba, d_wx, d_bx, d_lam, dg3]
        small = _pack(small + ([d_final] if l == L - 1 else []), 2 * 16 * N_CHIPS).reshape(N_CHIPS, -1, LANES)
        token = advance((l, GROUPS[0] + ("small",), [d_up1, d_dn1.reshape(N_CHIPS, -1, D), small]))
    while any(stage is not None for stage in stages):
        advance(None)

    pieces = _gather_small(jnp.concatenate(small_pieces), "gather_small_grads")
    pieces = pieces.reshape(N_CHIPS, L, -1)
    like = {n: jax.ShapeDtypeStruct(W[n].shape[1:] if n != "conv_w" else (CONV_WIDTH, lw), F32) for n in SMALL_LAYER}
    per_layer = []
    for l in range(L):
        names = SMALL_LAYER + (("final_norm",) if l == L - 1 else ())
        shapes = [like[n] if n in like else jax.ShapeDtypeStruct((D,), F32) for n in names]
        per_layer.append(dict(zip(names, _unpack(pieces[:, l], shapes))))
    small_sum = {n: jnp.stack([per_layer[l][n] for l in range(L)]) for n in SMALL_LAYER}
    small_sum["final_norm"] = per_layer[L - 1]["final_norm"]
    cs = conv_w.shape[2]
    small_sum["conv_w"] = lax.dynamic_slice_in_dim(small_sum["conv_w"], chip * cs, cs, axis=2)
    packs = [_pack([d[n] for n in SMALL], 1024) for d in (W, small_sum, M, V)]
    res = _adamw(*packs, "adamw_small")
    like = [W[n] for n in SMALL]
    unpacked = [_unpack(r, like) for r in res]
    for j, n in enumerate(SMALL):
        out[n] = tuple(u[j] for u in unpacked)

    return (loss, dx.reshape(x.shape), *[out[n][0] for n in WEIGHTS], *[out[n][1] for n in WEIGHTS],
            *[out[n][2] for n in WEIGHTS], *[out[n][3] for n in WEIGHTS])
```

```python
import functools

import jax
import jax.numpy as jnp
from jax import lax
from jax.experimental import pallas as pl
from jax.experimental.pallas import tpu as pltpu

F32 = jnp.float32
MM = jnp.bfloat16
XFER = jnp.bfloat16
KEPT = jnp.bfloat16

EPS = 1e-6
LRU_C = 8.0
POOL_WINDOWS = (2, 4, 8, 16)
CONV_WIDTH = 4
ADAM_LR, ADAM_B1, ADAM_B2, ADAM_EPS, ADAM_WD, ADAM_STEP = 0.001, 0.9, 0.999, 1e-08, 0.01, 10

N_CHIPS = 4
LANES = 128
SHIFT_PAD = 8
TOKEN_TILE = 512
VMEM_LIMIT = 60 * 1024 * 1024
MESH = pl.DeviceIdType.MESH
ANY = pl.BlockSpec(memory_space=pl.ANY)


def _params(sem=None):
    return pltpu.CompilerParams(dimension_semantics=sem, vmem_limit_bytes=VMEM_LIMIT)


def _dot(a, b):
    return jnp.dot(a.astype(MM), b.astype(MM), preferred_element_type=F32)


def _dot_nt(a, b):
    return lax.dot_general(a.astype(MM), b.astype(MM), (((1,), (1,)), ((), ())), preferred_element_type=F32)


def _dot_tn(a, b):
    return lax.dot_general(a.astype(MM), b.astype(MM), (((0,), (0,)), ((), ())), preferred_element_type=F32)


def _rms(x, g):
    r = lax.rsqrt(jnp.mean(x * x, axis=-1, keepdims=True) + EPS)
    xh = x * r
    return r, xh, xh * g


def _rms_bwd(dh, xh, r, g):
    dxh = dh * g
    return r * (dxh - xh * jnp.mean(dxh * xh, axis=-1, keepdims=True))


def _accumulate(ref, val, first):
    @pl.when(first)
    def _():
        ref[...] = val

    @pl.when(jnp.logical_not(first))
    def _():
        ref[...] += val


def _row_block(rows, cols, itemsize=4, budget=1 << 20):
    best = None
    for rb in range(16, rows + 1, 16):
        if rows % rb == 0 and rb * cols * itemsize <= budget:
            best = rb
    return best if best is not None else rows


def _norm_matmul_fwd(x, gain, w, name, out_dtype=F32):
    T, D = x.shape
    K, _, tn = w.shape
    tm = min(TOKEN_TILE, T)

    def body(x_ref, g_ref, w_ref, u_ref):
        _, _, h = _rms(x_ref[...], g_ref[...])
        u_ref[...] = _dot(h, w_ref[...]).astype(out_dtype)

    return pl.pallas_call(
        body, name=name, grid=(K, T // tm),
        in_specs=[pl.BlockSpec((tm, D), lambda k, i: (i, 0)), pl.BlockSpec((1, D), lambda k, i: (0, 0)),
                  pl.BlockSpec((None, D, tn), lambda k, i: (k, 0, 0))],
        out_specs=pl.BlockSpec((tm, tn), lambda k, i: (i, k)),
        out_shape=jax.ShapeDtypeStruct((T, K * tn), out_dtype),
        compiler_params=_params(("arbitrary", "arbitrary")),
    )(x, gain, w)


def _norm_matmul_bwd(x, gain, du, du_spec, w, dres, name):
    T, D = x.shape
    K, _, tn = w.shape
    tm = min(TOKEN_TILE, T)
    ni = T // tm

    def body(x_ref, g_ref, du_ref, w_ref, dres_ref, dx_ref, dw_ref, dg_ref, dh_acc, dg_acc):
        k, i = pl.program_id(0), pl.program_id(1)
        g = g_ref[...]
        r, xh, h = _rms(x_ref[...], g)
        du_t = du_ref[...].astype(MM)
        rows = pl.ds(pl.multiple_of(i * tm, tm), tm)
        part = _dot_nt(du_t, w_ref[...])

        @pl.when(k == 0)
        def _():
            dh_acc[rows, :] = part

        @pl.when(k > 0)
        def _():
            dh_acc[rows, :] += part

        _accumulate(dw_ref, _dot_tn(h, du_t), i == 0)

        @pl.when(k == K - 1)
        def _():
            dh = dh_acc[rows, :]
            _accumulate(dg_acc, jnp.sum(dh * xh, axis=0, keepdims=True), i == 0)
            dx_ref[...] = dres_ref[...] + _rms_bwd(dh, xh, r, g)

            @pl.when(i == ni - 1)
            def _():
                dg_ref[...] = dg_acc[...]

    def last(k, i):
        return (jnp.where(k == K - 1, i, 0), 0)

    return pl.pallas_call(
        body, name=name, grid=(K, ni),
        in_specs=[pl.BlockSpec((tm, D), lambda k, i: (i, 0)), pl.BlockSpec((1, D), lambda k, i: (0, 0)),
                  du_spec(tm, tn), pl.BlockSpec((None, D, tn), lambda k, i: (k, 0, 0)),
                  pl.BlockSpec((tm, D), last)],
        out_specs=[pl.BlockSpec((tm, D), last), pl.BlockSpec((None, D, tn), lambda k, i: (k, 0, 0)),
                   pl.BlockSpec((1, D), lambda k, i: (0, 0))],
        out_shape=[jax.ShapeDtypeStruct((T, D), F32), jax.ShapeDtypeStruct((K, D, tn), F32),
                   jax.ShapeDtypeStruct((1, D), F32)],
        scratch_shapes=[pltpu.VMEM((T, D), F32), pltpu.VMEM((1, D), F32)],
        compiler_params=_params(("arbitrary", "arbitrary")),
    )(x, gain, du, w, dres)


def _swiglu_down_fwd(u, x, wd, name):
    T, D = x.shape
    Fh = wd.shape[0]
    tm = min(TOKEN_TILE, T)

    def body(a_ref, b_ref, x_ref, wd_ref, o_ref):
        a = a_ref[...].astype(F32)
        s = a * jax.nn.sigmoid(a) * b_ref[...].astype(F32)
        o_ref[...] = x_ref[...] + 0.5 * _dot(s, wd_ref[...])

    return pl.pallas_call(
        body, name=name, grid=(T // tm,),
        in_specs=[pl.BlockSpec((tm, Fh), lambda i: (i, 0)), pl.BlockSpec((tm, Fh), lambda i: (i, 1)),
                  pl.BlockSpec((tm, D), lambda i: (i, 0)), pl.BlockSpec((Fh, D), lambda i: (0, 0))],
        out_specs=pl.BlockSpec((tm, D), lambda i: (i, 0)),
        out_shape=jax.ShapeDtypeStruct((T, D), F32),
        compiler_params=_params(("arbitrary",)),
    )(u, u, x, wd)


def _swiglu_down_bwd(u, dxn, wd, name):
    T, D = dxn.shape
    Fh = wd.shape[0]
    tm = min(TOKEN_TILE, T)
    nj = 2 if Fh % (2 * LANES) == 0 else 1
    tf = Fh // nj

    def body(a_ref, b_ref, dxn_ref, wd_ref, du_ref, dwd_ref):
        i = pl.program_id(1)
        a, b = a_ref[...].astype(F32), b_ref[...].astype(F32)
        dyh = (0.5 * dxn_ref[...]).astype(MM)
        ds = _dot_nt(dyh, wd_ref[...])
        sig = jax.nn.sigmoid(a)
        sa = a * sig
        _accumulate(dwd_ref, _dot_tn(sa * b, dyh), i == 0)
        du_ref[0] = (ds * b * (sig * (1.0 + a * (1.0 - sig)))).astype(MM)
        du_ref[1] = (ds * sa).astype(MM)

    return pl.pallas_call(
        body, name=name, grid=(nj, T // tm),
        in_specs=[pl.BlockSpec((tm, tf), lambda j, i: (i, j)), pl.BlockSpec((tm, tf), lambda j, i: (i, j + nj)),
                  pl.BlockSpec((tm, D), lambda j, i: (i, 0)), pl.BlockSpec((tf, D), lambda j, i: (j, 0))],
        out_specs=[pl.BlockSpec((2, tm, tf), lambda j, i: (0, i, j)), pl.BlockSpec((tf, D), lambda j, i: (j, 0))],
        out_shape=[jax.ShapeDtypeStruct((2, T, Fh), MM), jax.ShapeDtypeStruct((Fh, D), F32)],
        compiler_params=_params(("arbitrary", "arbitrary")),
    )(u, u, dxn, wd)


def _mix_branches(pm, zl, gp_logit, gr_logit, wpu_ref, wlu_ref):
    y_pool = jnp.concatenate([_dot(pm, wpu_ref[k]) for k in range(N_CHIPS)], axis=1)
    y_lru = _dot(zl, wlu_ref[...])
    return y_pool, y_lru, jax.nn.sigmoid(gp_logit), jax.nn.sigmoid(gr_logit)


def _mix_out_specs(tm, D, pw, lw, cs):
    gate0 = (pw + 2 * lw) // D
    return [pl.BlockSpec((tm, D), lambda i: (i, gate0)), pl.BlockSpec((tm, D), lambda i: (i, gate0 + 1)),
            pl.BlockSpec((tm, pw), lambda i: (i, 0)), pl.BlockSpec((tm, lw), lambda i: (i, 0)),
            pl.BlockSpec((N_CHIPS, pw, cs), lambda i: (0, 0, 0)), pl.BlockSpec((lw, D), lambda i: (0, 0)),
            pl.BlockSpec((D, D), lambda i: (0, 0))]


def _mix_out_fwd(x, proj, pm, zl, wpu, wlu, wo, name):
    T, D = x.shape
    pw, lw, cs = pm.shape[1], zl.shape[1], wpu.shape[2]
    assert (pw + 2 * lw) % D == 0
    tm = min(TOKEN_TILE, T)

    def body(x_ref, gp_ref, gr_ref, pm_ref, zl_ref, wpu_ref, wlu_ref, wo_ref, o_ref):
        y_pool, y_lru, gp, gr = _mix_branches(pm_ref[...], zl_ref[...], gp_ref[...], gr_ref[...], wpu_ref, wlu_ref)
        o_ref[...] = x_ref[...] + _dot(gp * y_pool + gr * y_lru, wo_ref[...])

    return pl.pallas_call(
        body, name=name, grid=(T // tm,),
        in_specs=[pl.BlockSpec((tm, D), lambda i: (i, 0))] + _mix_out_specs(tm, D, pw, lw, cs),
        out_specs=pl.BlockSpec((tm, D), lambda i: (i, 0)),
        out_shape=jax.ShapeDtypeStruct((T, D), F32),
        compiler_params=_params(("arbitrary",)),
    )(x, proj, proj, pm, zl, wpu, wlu, wo)


def _mix_out_bwd(dxn, proj, pm, zl, wpu, wlu, wo, name):
    T, D = dxn.shape
    pw, lw, cs = pm.shape[1], zl.shape[1], wpu.shape[2]
    tm = min(TOKEN_TILE // 2, T)
    ni = T // tm

    def body(dxn_ref, gp_ref, gr_ref, pm_ref, zl_ref, wpu_ref, wlu_ref, wo_ref,
             dgl_ref, dpm_ref, dzl_ref, dwo_hbm, dwpu_hbm, dwlu_hbm, acc_o, acc_pu, acc_lu, sem):
        i = pl.program_id(0)
        pm, zl = pm_ref[...], zl_ref[...]
        y_pool, y_lru, gp, gr = _mix_branches(pm, zl, gp_ref[...], gr_ref[...], wpu_ref, wlu_ref)
        dxn_t = dxn_ref[...].astype(MM)
        dmerged = _dot_nt(dxn_t, wo_ref[...])
        _accumulate(acc_o, _dot_tn(gp * y_pool + gr * y_lru, dxn_t), i == 0)
        dgl_ref[:, :D] = (dmerged * y_pool * (gp * (1.0 - gp))).astype(MM)
        dgl_ref[:, D:] = (dmerged * y_lru * (gr * (1.0 - gr))).astype(MM)
        dyp = (dmerged * gp).astype(MM)
        dyl = (dmerged * gr).astype(MM)
        dpm = None
        for k in range(N_CHIPS):
            dyp_k = dyp[:, k * cs:(k + 1) * cs]
            part = _dot_nt(dyp_k, wpu_ref[k])
            dpm = part if dpm is None else dpm + part
            _accumulate(acc_pu.at[k], _dot_tn(pm, dyp_k), i == 0)
        dpm_ref[...] = dpm
        dzl_ref[...] = _dot_nt(dyl, wlu_ref[...])
        _accumulate(acc_lu, _dot_tn(zl, dyl), i == 0)

        @pl.when(i == ni - 1)
        def _():
            copies = [pltpu.make_async_copy(acc_o, dwo_hbm, sem.at[0]), pltpu.make_async_copy(acc_pu, dwpu_hbm, sem.at[1]),
                      pltpu.make_async_copy(acc_lu, dwlu_hbm, sem.at[2])]
            for cp in copies:
                cp.start()
            for cp in copies:
                cp.wait()

    return pl.pallas_call(
        body, name=name, grid=(ni,),
        in_specs=[pl.BlockSpec((tm, D), lambda i: (i, 0))] + _mix_out_specs(tm, D, pw, lw, cs),
        out_specs=[pl.BlockSpec((tm, 2 * D), lambda i: (i, 0)), pl.BlockSpec((tm, pw), lambda i: (i, 0)),
                   pl.BlockSpec((tm, lw), lambda i: (i, 0)), ANY, ANY, ANY],
        out_shape=[jax.ShapeDtypeStruct((T, 2 * D), MM), jax.ShapeDtypeStruct((T, pw), F32),
                   jax.ShapeDtypeStruct((T, lw), F32), jax.ShapeDtypeStruct((D, D), F32),
                   jax.ShapeDtypeStruct((N_CHIPS, pw, cs), F32), jax.ShapeDtypeStruct((lw, D), F32)],
        scratch_shapes=[pltpu.VMEM((D, D), F32), pltpu.VMEM((N_CHIPS, pw, cs), F32), pltpu.VMEM((lw, D), F32),
                        pltpu.SemaphoreType.DMA((3,))],
        compiler_params=_params(("arbitrary",)),
    )(dxn, proj, proj, pm, zl, wpu, wlu, wo)


def _shifted(buf, val, shift, S):
    buf[pl.ds(SHIFT_PAD, S), :] = val
    return buf[pl.ds(SHIFT_PAD - shift, S), :]


def _zero_pads(buf, S):
    z = jnp.zeros((SHIFT_PAD, buf.shape[1]), F32)
    buf[pl.ds(0, SHIFT_PAD), :] = z
    buf[pl.ds(SHIFT_PAD + S, SHIFT_PAD), :] = z


def _window_sum(buf, val, window, S, lead=False):
    acc, width = val, 1
    while width < window:
        acc = acc + _shifted(buf, acc, -width if lead else width, S)
        width *= 2
    return acc


def _pool_count(S, window):
    t = lax.broadcasted_iota(jnp.int32, (S, LANES), 0)
    return jnp.minimum(t + 1, window).astype(F32)


def _pool_fwd_group(u, g, window, pw_ref, pb_ref, buf, S):
    pooled = _window_sum(buf, u, window, S) / _pool_count(S, window) - u
    return pooled, _dot(pooled, pw_ref[g]) + pb_ref[g]


def _pool_fwd(proj, pool_w, pool_b, pool_scale, name):
    S = proj.shape[0]
    G = pool_w.shape[0]
    pw = G * LANES

    def body(u_ref, pw_ref, pb_ref, ps_ref, pm_ref, buf):
        _zero_pads(buf, S)
        for g in range(G):
            cols = slice(g * LANES, (g + 1) * LANES)
            _, mixed = _pool_fwd_group(u_ref[:, cols], g, POOL_WINDOWS[g], pw_ref, pb_ref, buf, S)
            pm_ref[:, cols] = (mixed * ps_ref[:, cols]).astype(MM)

    return pl.pallas_call(
        body, name=name, grid=(1,),
        in_specs=[pl.BlockSpec((S, pw), lambda i: (0, 0)), pl.BlockSpec((G, LANES, LANES), lambda i: (0, 0, 0)),
                  pl.BlockSpec((G, 1, LANES), lambda i: (0, 0, 0)), pl.BlockSpec((1, pw), lambda i: (0, 0))],
        out_specs=pl.BlockSpec((S, pw), lambda i: (0, 0)),
        out_shape=jax.ShapeDtypeStruct((S, pw), MM),
        scratch_shapes=[pltpu.VMEM((S + 2 * SHIFT_PAD, LANES), F32)],
        compiler_params=_params(("arbitrary",)),
    )(proj, pool_w, pool_b, pool_scale)


def _pool_bwd(proj, dpm, pool_w, pool_b, pool_scale, name):
    S = proj.shape[0]
    G = pool_w.shape[0]
    pw = G * LANES

    def body(u_ref, dpm_ref, pw_ref, pb_ref, ps_ref, du_ref, dpw_ref, dpb_ref, dps_ref, buf):
        _zero_pads(buf, S)
        for g in range(G):
            cols = slice(g * LANES, (g + 1) * LANES)
            window = POOL_WINDOWS[g]
            pooled, mixed = _pool_fwd_group(u_ref[:, cols], g, window, pw_ref, pb_ref, buf, S)
            d_out = dpm_ref[:, cols]
            dmixed = d_out * ps_ref[:, cols]
            dps_ref[:, cols] = jnp.sum(d_out * mixed, axis=0, keepdims=True)
            dpb_ref[g] = jnp.sum(dmixed, axis=0, keepdims=True)
            dpw_ref[g] = _dot_tn(pooled, dmixed)
            dpooled = _dot_nt(dmixed, pw_ref[g])
            q = dpooled / _pool_count(S, window)
            du_ref[:, cols] = (_window_sum(buf, q, window, S, lead=True) - dpooled).astype(MM)

    return pl.pallas_call(
        body, name=name, grid=(1,),
        in_specs=[pl.BlockSpec((S, pw), lambda i: (0, 0)), pl.BlockSpec((S, pw), lambda i: (0, 0)),
                  pl.BlockSpec((G, LANES, LANES), lambda i: (0, 0, 0)),
                  pl.BlockSpec((G, 1, LANES), lambda i: (0, 0, 0)), pl.BlockSpec((1, pw), lambda i: (0, 0))],
        out_specs=[pl.BlockSpec((S, pw), lambda i: (0, 0)), pl.BlockSpec((G, LANES, LANES), lambda i: (0, 0, 0)),
                   pl.BlockSpec((G, 1, LANES), lambda i: (0, 0, 0)), pl.BlockSpec((1, pw), lambda i: (0, 0))],
        out_shape=[jax.ShapeDtypeStruct((S, pw), MM), jax.ShapeDtypeStruct((G, LANES, LANES), F32),
                   jax.ShapeDtypeStruct((G, 1, LANES), F32), jax.ShapeDtypeStruct((1, pw), F32)],
        scratch_shapes=[pltpu.VMEM((S + 2 * SHIFT_PAD, LANES), F32)],
        compiler_params=_params(("arbitrary",)),
    )(proj, dpm, pool_w, pool_b, pool_scale)


def _scan(a, b, bufs, S, reverse=False):
    pad = S // 2
    seq = pl.ds(pad, S)
    A, B = bufs[:2], bufs[2:]
    A[0][seq, :] = a
    B[0][seq, :] = b
    cur, d = 0, 1
    while d < S:
        sh = pl.ds(pad + d if reverse else pad - d, S)
        a_c = A[cur][seq, :]
        B[1 - cur][seq, :] = a_c * B[cur][sh, :] + B[cur][seq, :]
        if 2 * d < S:
            A[1 - cur][seq, :] = a_c * A[cur][sh, :]
        cur, d = 1 - cur, 2 * d
    return B[cur][seq, :]


def _init_scan_pads(bufs, S):
    pad = S // 2
    for n, buf in enumerate(bufs):
        fill = jnp.full((pad, LANES), 1.0 if n < 2 else 0.0, F32)
        buf[pl.ds(0, pad), :] = fill
        buf[pl.ds(pad + S, pad), :] = fill


def _gelu_and_grad(x):
    c = 0.7978845608028654
    x2 = x * x
    th = jnp.tanh(c * (x + 0.044715 * (x2 * x)))
    gelu = 0.5 * x * (1.0 + th)
    grad = 0.5 * (1.0 + th) + 0.5 * x * (1.0 - th * th) * (c * (1.0 + 3 * 0.044715 * x2))
    return gelu, grad


def _lru_head_fwd(ul, cw, cb, wa, ba, wx, bx, lam, sbuf, S):
    sbuf[pl.ds(SHIFT_PAD, S), :] = ul
    v = None
    for k in range(CONV_WIDTH):
        term = sbuf[pl.ds(SHIFT_PAD - (CONV_WIDTH - 1 - k), S), :] * cw[k:k + 1, :]
        v = term if v is None else v + term
    v = v + cb
    r = jax.nn.sigmoid(_dot(v, wa) + ba)
    ig = jax.nn.sigmoid(_dot(v, wx) + bx)
    sp = jax.nn.softplus(-lam)
    log_a = -LRU_C * r * sp
    a = jnp.exp(log_a)
    mult = jnp.sqrt(-jnp.tanh(log_a) * (1.0 + a * a))
    return v, r, ig, sp, a, mult


def _lru_specs(S, H, lw, pw):
    b0 = pw // LANES
    return [pl.BlockSpec((S, LANES), lambda h: (0, b0 + h)), pl.BlockSpec((S, LANES), lambda h: (0, b0 + H + h)),
            pl.BlockSpec((CONV_WIDTH, LANES), lambda h: (0, h)), pl.BlockSpec((1, LANES), lambda h: (0, h)),
            pl.BlockSpec((None, LANES, LANES), lambda h: (h, 0, 0)), pl.BlockSpec((None, 1, LANES), lambda h: (h, 0, 0)),
            pl.BlockSpec((None, LANES, LANES), lambda h: (h, 0, 0)), pl.BlockSpec((None, 1, LANES), lambda h: (h, 0, 0)),
            pl.BlockSpec((1, LANES), lambda h: (0, h))]


def _lru_scratch(S):
    return [pltpu.VMEM((S + 2 * SHIFT_PAD, LANES), F32)] + [pltpu.VMEM((2 * S, LANES), F32)] * 4


def _lru_fwd(proj, conv_w, conv_b, wa, ba, wx, bx, lam, pw, name):
    S = proj.shape[0]
    H = wa.shape[0]
    lw = H * LANES

    def body(ul_ref, ug_ref, cw_ref, cb_ref, wa_ref, ba_ref, wx_ref, bx_ref, lam_ref, zl_ref, h_ref, sbuf, *bufs):
        @pl.when(pl.program_id(0) == 0)
        def _():
            _zero_pads(sbuf, S)
            _init_scan_pads(bufs, S)

        v, _, ig, _, a, mult = _lru_head_fwd(ul_ref[...], cw_ref[...], cb_ref[...], wa_ref[...], ba_ref[...], wx_ref[...],
                                             bx_ref[...], lam_ref[...], sbuf, S)
        h = _scan(a, mult * (ig * v), bufs, S)
        h_ref[...] = h
        zl_ref[...] = (h * jax.nn.gelu(ug_ref[...])).astype(MM)

    col = pl.BlockSpec((S, LANES), lambda h: (0, h))
    return pl.pallas_call(
        body, name=name, grid=(H,),
        in_specs=_lru_specs(S, H, lw, pw),
        out_specs=[col, col],
        out_shape=[jax.ShapeDtypeStruct((S, lw), MM), jax.ShapeDtypeStruct((S, lw), F32)],
        scratch_shapes=_lru_scratch(S),
        compiler_params=_params(("arbitrary",)),
    )(proj, proj, conv_w, conv_b, wa, ba, wx, bx, lam)


def _lru_bwd(proj, h_all, dzl, conv_w, conv_b, wa, ba, wx, bx, lam, pw, name):
    S = proj.shape[0]
    H = wa.shape[0]
    lw = H * LANES

    def body(ul_ref, ug_ref, cw_ref, cb_ref, wa_ref, ba_ref, wx_ref, bx_ref, lam_ref, h_ref, dzl_ref,
             dul_ref, dug_ref, dcw_ref, dcb_ref, dwa_ref, dba_ref, dwx_ref, dbx_ref, dlam_ref, sbuf, *bufs):
        @pl.when(pl.program_id(0) == 0)
        def _():
            _zero_pads(sbuf, S)
            _init_scan_pads(bufs, S)

        ul, cw, lam = ul_ref[...], cw_ref[...], lam_ref[...]
        wa, wx = wa_ref[...], wx_ref[...]
        v, r, ig, sp, a, mult = _lru_head_fwd(ul, cw, cb_ref[...], wa, ba_ref[...], wx, bx_ref[...], lam, sbuf, S)
        h = h_ref[...]
        gelu, gelu_grad = _gelu_and_grad(ug_ref[...])
        dzl_t = dzl_ref[...]
        dug_ref[...] = (dzl_t * h * gelu_grad).astype(MM)
        a_next = _shifted(sbuf, a, -1, S)
        lam_t = _scan(a_next, dzl_t * gelu, bufs, S, reverse=True)
        da = lam_t * _shifted(sbuf, h, 1, S)
        d_iv = lam_t * mult
        d_log_a = da * a - (lam_t * (ig * v)) * (a * a) / mult
        dlam_ref[...] = jnp.sum(d_log_a * r, axis=0, keepdims=True) * (LRU_C * jax.nn.sigmoid(-lam))
        dra = (d_log_a * (-LRU_C * sp)) * (r * (1.0 - r))
        dia = (d_iv * v) * (ig * (1.0 - ig))
        dba_ref[...] = jnp.sum(dra, axis=0, keepdims=True)
        dbx_ref[...] = jnp.sum(dia, axis=0, keepdims=True)
        dwa_ref[...] = _dot_tn(v, dra)
        dwx_ref[...] = _dot_tn(v, dia)
        dv = d_iv * ig + _dot_nt(dra, wa) + _dot_nt(dia, wx)
        dcb_ref[...] = jnp.sum(dv, axis=0, keepdims=True)
        sbuf[pl.ds(SHIFT_PAD, S), :] = ul
        for k in range(CONV_WIDTH):
            dcw_ref[k:k + 1, :] = jnp.sum(dv * sbuf[pl.ds(SHIFT_PAD - (CONV_WIDTH - 1 - k), S), :], axis=0, keepdims=True)
        sbuf[pl.ds(SHIFT_PAD, S), :] = dv
        dul = None
        for k in range(CONV_WIDTH):
            term = sbuf[pl.ds(SHIFT_PAD + (CONV_WIDTH - 1 - k), S), :] * cw[k:k + 1, :]
            dul = term if dul is None else dul + term
        dul_ref[...] = dul.astype(MM)

    head_mat = pl.BlockSpec((None, LANES, LANES), lambda h: (h, 0, 0))
    head_vec = pl.BlockSpec((None, 1, LANES), lambda h: (h, 0, 0))
    col = pl.BlockSpec((S, LANES), lambda h: (0, h))
    row = pl.BlockSpec((1, LANES), lambda h: (0, h))
    return pl.pallas_call(
        body, name=name, grid=(H,),
        in_specs=_lru_specs(S, H, lw, pw) + [col, col],
        out_specs=[col, col, pl.BlockSpec((CONV_WIDTH, LANES), lambda h: (0, h)), row, head_mat, head_vec, head_mat,
                   head_vec, row],
        out_shape=[jax.ShapeDtypeStruct((S, lw), MM), jax.ShapeDtypeStruct((S, lw), MM),
                   jax.ShapeDtypeStruct((CONV_WIDTH, lw), F32), jax.ShapeDtypeStruct((1, lw), F32),
                   jax.ShapeDtypeStruct((H, LANES, LANES), F32), jax.ShapeDtypeStruct((H, 1, LANES), F32),
                   jax.ShapeDtypeStruct((H, LANES, LANES), F32), jax.ShapeDtypeStruct((H, 1, LANES), F32),
                   jax.ShapeDtypeStruct((1, lw), F32)],
        scratch_shapes=_lru_scratch(S),
        compiler_params=_params(("arbitrary",)),
    )(proj, proj, conv_w, conv_b, wa, ba, wx, bx, lam, h_all, dzl)


def _loss_head(x, gain, target, name):
    T, D = x.shape
    tm = min(TOKEN_TILE, T)
    ni = T // tm

    def body(x_ref, g_ref, t_ref, loss_ref, dx_ref, dg_ref, loss_acc, dg_acc):
        i = pl.program_id(0)
        g = g_ref[...]
        r, xh, y = _rms(x_ref[...], g)
        err = y - t_ref[...]
        part = 0.5 * jnp.sum(jnp.mean(err * err, axis=-1, keepdims=True), axis=0, keepdims=True)
        _accumulate(loss_acc, jnp.broadcast_to(part, (1, LANES)), i == 0)
        dy = err * (1.0 / D)
        _accumulate(dg_acc, jnp.sum(dy * xh, axis=0, keepdims=True), i == 0)
        dx_ref[...] = _rms_bwd(dy, xh, r, g)

        @pl.when(i == ni - 1)
        def _():
            loss_ref[...] = loss_acc[...]
            dg_ref[...] = dg_acc[...]

    return pl.pallas_call(
        body, name=name, grid=(ni,),
        in_specs=[pl.BlockSpec((tm, D), lambda i: (i, 0)), pl.BlockSpec((1, D), lambda i: (0, 0)),
                  pl.BlockSpec((tm, D), lambda i: (i, 0))],
        out_specs=[pl.BlockSpec((1, LANES), lambda i: (0, 0)), pl.BlockSpec((tm, D), lambda i: (i, 0)),
                   pl.BlockSpec((1, D), lambda i: (0, 0))],
        out_shape=[jax.ShapeDtypeStruct((1, LANES), F32), jax.ShapeDtypeStruct((T, D), F32),
                   jax.ShapeDtypeStruct((1, D), F32)],
        scratch_shapes=[pltpu.VMEM((1, LANES), F32), pltpu.VMEM((1, D), F32)],
        compiler_params=_params(("arbitrary",)),
    )(x, gain, target)


def _my_core():
    return lax.axis_index("c")


def _my_chip():
    return 2 * lax.axis_index("x") + lax.axis_index("y")


def _pair_sum(a, recv, xfer, name):
    n, R, C = a.shape
    hr = R // 2

    def body(a_ref, r_ref, pf_ref, pb_ref):
        s = a_ref[...] + r_ref[...]
        pf_ref[...] = s
        pb_ref[...] = s.astype(xfer)

    piece = pl.BlockSpec((None, hr, C), lambda k: (k, 0, 0))
    return pl.pallas_call(
        body, name=name, grid=(n,),
        in_specs=[pl.BlockSpec((None, hr, C), lambda k: (k, _my_core(), 0)), piece],
        out_specs=[piece, piece],
        out_shape=[jax.ShapeDtypeStruct((n, hr, C), F32), jax.ShapeDtypeStruct((n, hr, C), xfer)],
        compiler_params=_params(("arbitrary",)),
    )(a, recv)


def _chip_sum(pf, others, name):
    _, hr, C = pf.shape
    rb = _row_block(hr, C)
    nb = hr // rb

    def body(o_ref, q_ref, g_ref):
        g_ref[...] = ((o_ref[...] + q_ref[0].astype(F32)) + q_ref[1].astype(F32)) + q_ref[2].astype(F32)

    return pl.pallas_call(
        body, name=name, grid=(nb,),
        in_specs=[pl.BlockSpec((None, rb, C), lambda i: (_my_chip(), i, 0)),
                  pl.BlockSpec((N_CHIPS - 1, rb, C), lambda i: (0, i, 0))],
        out_specs=pl.BlockSpec((rb, C), lambda i: (_my_core() * nb + i, 0)),
        out_shape=jax.ShapeDtypeStruct((2 * hr, C), F32),
        compiler_params=_params(("arbitrary",)),
    )(pf, others)


def _adamw_step(w_ref, g_ref, m_ref, v_ref, go_ref, d_ref, mo_ref, vo_ref):
    g_t = g_ref[...]
    m_t = ADAM_B1 * m_ref[...] + (1.0 - ADAM_B1) * g_t
    v_t = ADAM_B2 * v_ref[...] + (1.0 - ADAM_B2) * (g_t * g_t)
    m_hat = m_t / (1.0 - ADAM_B1 ** ADAM_STEP)
    v_hat = v_t / (1.0 - ADAM_B2 ** ADAM_STEP)
    go_ref[...] = g_t
    d_ref[...] = -ADAM_LR * (m_hat / (jnp.sqrt(v_hat) + ADAM_EPS) + ADAM_WD * w_ref[...])
    mo_ref[...] = m_t
    vo_ref[...] = v_t


def _adamw(w, g, m, v, name):
    R, C = w.shape
    rb = _row_block(R, C)
    blk = pl.BlockSpec((rb, C), lambda i: (i, 0))
    return pl.pallas_call(
        functools.partial(_adamw_step), name=name, grid=(R // rb,), in_specs=[blk] * 4, out_specs=[blk] * 4,
        out_shape=[jax.ShapeDtypeStruct((R, C), F32)] * 4,
        compiler_params=_params(("arbitrary",)),
    )(w, g, m, v)


def _adamw_layer(w, g, m, v, outs, layer, name):
    L, R, C = w.shape
    rb = _row_block(R, C)

    def body(w_ref, g_ref, m_ref, v_ref, *rest):
        _adamw_step(w_ref, g_ref, m_ref, v_ref, *rest[4:])

    stacked = pl.BlockSpec((None, rb, C), lambda i: (layer, i, 0))
    return pl.pallas_call(
        body, name=name, grid=(R // rb,),
        in_specs=[stacked, pl.BlockSpec((rb, C), lambda i: (i, 0)), stacked, stacked] + [ANY] * 4,
        out_specs=[stacked] * 4,
        out_shape=[jax.ShapeDtypeStruct((L, R, C), F32)] * 4,
        input_output_aliases={4 + j: j for j in range(4)},
        compiler_params=_params(("arbitrary",)),
    )(w, g, m, v, *outs)


def _place():
    x, y, c = lax.axis_index("x"), lax.axis_index("y"), lax.axis_index("c")
    others = [(1 - x, y), (x, 1 - y), (1 - x, 1 - y)]
    return x, y, c, 2 * x + y, others


def _half(c, rows):
    return pl.ds(pl.multiple_of(c * (rows // 2), 16), rows // 2)


HBM_SPEC = pl.BlockSpec(memory_space=pltpu.HBM)
SEM_SPEC = pl.BlockSpec(memory_space=pltpu.SEMAPHORE)
SPLIT = pltpu.CompilerParams(has_side_effects=pltpu.SideEffectType.DATAFLOW_SIDE_EFFECTING)
TOKEN = jax.ShapeDtypeStruct((8, LANES), F32)


def _in_hbm(a):
    return pltpu.with_memory_space_constraint(a, pltpu.HBM)


def _split_start(build, count, srcs, lands, after, name):
    ns, nl = len(srcs), len(lands)

    def body(*refs):
        for started, _ in build(refs[:ns], refs[ns:ns + nl], refs[ns + nl + 1], refs[ns + nl + 2]):
            started.start()
        refs[-1][...] = jnp.zeros_like(refs[-1])

    res = pl.pallas_call(
        body, name=name,
        in_specs=[HBM_SPEC] * (ns + nl) + [ANY],
        out_specs=[SEM_SPEC] * 2 + [HBM_SPEC] * nl + [pl.BlockSpec(memory_space=pltpu.VMEM)],
        out_shape=[pltpu.SemaphoreType.DMA((count,))] * 2 + [pltpu.HBM(z.shape, z.dtype) for z in lands] + [TOKEN],
        input_output_aliases={ns + a: 2 + a for a in range(nl)},
        compiler_params=SPLIT,
    )(*[_in_hbm(s) for s in srcs], *[_in_hbm(z) for z in lands], after)
    return res[:2], res[2:2 + nl], res[-1]


def _split_wait(build, srcs, lands, sems, after, name):
    ns, nl = len(srcs), len(lands)

    def body(*refs):
        for started, arriving in build(refs[:ns], refs[ns:ns + nl], refs[ns + nl], refs[ns + nl + 1]):
            started.wait_send()
            arriving.wait_recv()

    return pl.pallas_call(
        body, name=name,
        in_specs=[HBM_SPEC] * (ns + nl) + [SEM_SPEC] * 2 + [ANY],
        out_specs=[HBM_SPEC] * nl,
        out_shape=[pltpu.HBM(z.shape, z.dtype) for z in lands],
        input_output_aliases={ns + a: a for a in range(nl)},
        compiler_params=SPLIT,
    )(*[_in_hbm(s) for s in srcs], *lands, *sems, after)


def _gather_copies(srcs, lands, send, recv, layer):
    x, y, c, k, others = _place()
    pairs = []
    for a in range(len(srcs)):
        rows = _half(c, srcs[a].shape[-2])
        for r, (px, py) in enumerate(others):
            def ici(slot, a=a, r=r, px=px, py=py, rows=rows):
                return pltpu.make_async_remote_copy(src_ref=srcs[a].at[layer].at[rows], dst_ref=lands[a].at[slot].at[rows],
                                                    send_sem=send.at[a * 4 + r], recv_sem=recv.at[a * 4 + r],
                                                    device_id=(px, py, c), device_id_type=MESH)
            pairs.append((ici(k), ici(2 * px + py)))
        own = pltpu.make_async_remote_copy(src_ref=srcs[a].at[layer], dst_ref=lands[a].at[k], send_sem=send.at[a * 4 + 3],
                                           recv_sem=recv.at[a * 4 + 3], device_id=(x, y, 1 - c), device_id_type=MESH)
        pairs.append((own, own))
    return pairs


def _halves_copies(srcs, lands, send, recv):
    x, y, c, _, _ = _place()
    pairs = []
    for a in range(len(srcs)):
        cp = pltpu.make_async_remote_copy(src_ref=srcs[a].at[:, _half(1 - c, srcs[a].shape[1]), :], dst_ref=lands[a],
                                          send_sem=send.at[a], recv_sem=recv.at[a], device_id=(x, y, 1 - c),
                                          device_id_type=MESH)
        pairs.append((cp, cp))
    return pairs


def _chip_copies(srcs, lands, send, recv):
    x, y, c, k, others = _place()
    pairs = []
    for a in range(len(srcs)):
        for r, (px, py) in enumerate(others):
            cp = pltpu.make_async_remote_copy(src_ref=srcs[a].at[2 * px + py], dst_ref=lands[a].at[r],
                                              send_sem=send.at[a * 3 + r], recv_sem=recv.at[a * 3 + r],
                                              device_id=(px, py, c), device_id_type=MESH)
            pairs.append((cp, cp))
    return pairs


def _forward_halves(lands, name):
    n = len(lands)

    def body(*refs):
        outs = refs[n:2 * n]
        send, recv = refs[2 * n:]
        x, y, c, _, others = _place()

        def copy(a, r, half):
            px, py = others[r]
            rows = outs[a].at[2 * px + py].at[_half(half, outs[a].shape[-2])]
            return pltpu.make_async_remote_copy(src_ref=rows, dst_ref=rows, send_sem=send.at[a * 3 + r],
                                                recv_sem=recv.at[a * 3 + r], device_id=(x, y, 1 - c), device_id_type=MESH)

        every = [(a, r) for a in range(n) for r in range(3)]
        for a, r in every:
            copy(a, r, c).start()
        for a, r in every:
            copy(a, r, 1 - c).wait_recv()
        for a, r in every:
            copy(a, r, c).wait_send()

    return pl.pallas_call(
        body, name=name, in_specs=[ANY] * n, out_specs=[ANY] * n,
        out_shape=[jax.ShapeDtypeStruct(z.shape, z.dtype) for z in lands],
        input_output_aliases={a: a for a in range(n)},
        scratch_shapes=[pltpu.SemaphoreType.DMA((3 * n,))] * 2,
    )(*lands)


def _join_halves(fulls, name):
    n = len(fulls)

    def body(*refs):
        outs = refs[n:2 * n]
        send, recv = refs[2 * n:]
        x, y, c, _, _ = _place()

        def copy(a, half):
            rows = outs[a].at[_half(half, outs[a].shape[0]), :]
            return pltpu.make_async_remote_copy(src_ref=rows, dst_ref=rows, send_sem=send.at[a], recv_sem=recv.at[a],
                                                device_id=(x, y, 1 - c), device_id_type=MESH)

        for a in range(n):
            copy(a, c).start()
        for a in range(n):
            copy(a, 1 - c).wait_recv()
        for a in range(n):
            copy(a, c).wait_send()

    return pl.pallas_call(
        body, name=name, in_specs=[ANY] * n, out_specs=[ANY] * n,
        out_shape=[jax.ShapeDtypeStruct(f.shape, f.dtype) for f in fulls],
        input_output_aliases={a: a for a in range(n)},
        scratch_shapes=[pltpu.SemaphoreType.DMA((n,))] * 2,
    )(*fulls)


def _gather_small(small, name):
    def body(in_ref, out_ref, send, recv):
        x, y, c, k, others = _place()
        mine = pltpu.make_async_remote_copy(src_ref=in_ref, dst_ref=out_ref.at[k], send_sem=send.at[3], recv_sem=recv.at[3],
                                            device_id=(x, y, 1 - c), device_id_type=MESH)
        mine.start()
        copies = []
        for r, (px, py) in enumerate(others):
            copies.append(pltpu.make_async_remote_copy(src_ref=in_ref, dst_ref=out_ref.at[k], send_sem=send.at[r],
                                                       recv_sem=recv.at[r], device_id=(px, py, c), device_id_type=MESH))
            copies[-1].start()
        for r, (px, py) in enumerate(others):
            pltpu.make_async_remote_copy(src_ref=in_ref, dst_ref=out_ref.at[2 * px + py], send_sem=send.at[r],
                                         recv_sem=recv.at[r], device_id=(px, py, c), device_id_type=MESH).wait_recv()
        for cp in copies:
            cp.wait_send()
        mine.wait()

    return pl.pallas_call(
        body, name=name, in_specs=[ANY], out_specs=ANY,
        out_shape=jax.ShapeDtypeStruct((N_CHIPS,) + small.shape, small.dtype),
        scratch_shapes=[pltpu.SemaphoreType.DMA((4,))] * 2,
    )(small)


GROUPS = (("ffn1_w_up", "ffn1_w_down"), ("w_in", "w_pool_up", "w_lru_up", "w_out"), ("ffn2_w_up", "ffn2_w_down"))
BIG = GROUPS[0] + GROUPS[1] + GROUPS[2]
SMALL_LAYER = ("norm_ffn1", "norm_mix", "pool_w", "pool_b", "pool_scale", "conv_w", "conv_b", "lru_w_a", "lru_b_a",
               "lru_w_x", "lru_b_x", "lru_lambda", "norm_ffn2")
SMALL = SMALL_LAYER + ("final_norm",)
GATHER_AHEAD = 3
WEIGHTS = ("norm_ffn1", "ffn1_w_up", "ffn1_w_down", "norm_mix", "w_in", "pool_w", "pool_b", "pool_scale", "w_pool_up",
           "conv_w", "conv_b", "lru_w_a", "lru_b_a", "lru_w_x", "lru_b_x", "lru_lambda", "w_lru_up", "w_out", "norm_ffn2",
           "ffn2_w_up", "ffn2_w_down", "final_norm")


def _pack(arrays, rows_multiple):
    flat = jnp.concatenate([a.reshape(-1) for a in arrays])
    rows = -(-flat.shape[0] // LANES)
    rows = -(-rows // rows_multiple) * rows_multiple
    return jnp.pad(flat, (0, rows * LANES - flat.shape[0])).reshape(rows, LANES)


def _unpack(packed, like):
    flat, out, at = packed.reshape(-1), [], 0
    for a in like:
        out.append(flat[at:at + a.size].reshape(a.shape))
        at += a.size
    return out


def kernel(x, norm_ffn1, ffn1_w_up, ffn1_w_down, norm_mix, w_in, pool_w, pool_b, pool_scale, w_pool_up, conv_w, conv_b, lru_w_a, lru_b_a, lru_w_x, lru_b_x, lru_lambda, w_lru_up, w_out, norm_ffn2, ffn2_w_up, ffn2_w_down, final_norm, loss_target, m_norm_ffn1, m_ffn1_w_up, m_ffn1_w_down, m_norm_mix, m_w_in, m_pool_w, m_pool_b, m_pool_scale, m_w_pool_up, m_conv_w, m_conv_b, m_lru_w_a, m_lru_b_a, m_lru_w_x, m_lru_b_x, m_lru_lambda, m_w_lru_up, m_w_out, m_norm_ffn2, m_ffn2_w_up, m_ffn2_w_down, m_final_norm, v_norm_ffn1, v_ffn1_w_up, v_ffn1_w_down, v_norm_mix, v_w_in, v_pool_w, v_pool_b, v_pool_scale, v_w_pool_up, v_conv_w, v_conv_b, v_lru_w_a, v_lru_b_a, v_lru_w_x, v_lru_b_x, v_lru_lambda, v_w_lru_up, v_w_out, v_norm_ffn2, v_ffn2_w_up, v_ffn2_w_down, v_final_norm):
    given = dict(locals())
    W = {n: given[n] for n in WEIGHTS}
    M = {n: given["m_" + n] for n in WEIGHTS}
    V = {n: given["v_" + n] for n in WEIGHTS}
    L = norm_ffn1.shape[0]
    T, D = x.shape[1], x.shape[2]
    G, H = pool_w.shape[1], lru_w_a.shape[1]
    pw, lw = G * LANES, H * LANES
    chip = 2 * lax.axis_index("x") + lax.axis_index("y")

    def tied(gain, token):
        return gain if token is None else gain + token[:1, :1]

    shards = {n: W[n].astype(MM) for n in BIG}
    conv_full = _gather_small(conv_w.reshape(L * CONV_WIDTH, conv_w.shape[2]), "gather_conv_w")
    conv_full = conv_full.reshape(N_CHIPS, L, CONV_WIDTH, -1).transpose(1, 2, 0, 3).reshape(L, CONV_WIDTH, lw)

    order = [(l, g) for l in range(L) for g in range(len(GROUPS))]
    started = {}

    def start_gather(i, after):
        l, g = order[i]
        srcs = [shards[n] for n in GROUPS[g]]
        lands = [lax.empty((N_CHIPS,) + s.shape[1:], s.dtype) for s in srcs]
        build = functools.partial(_gather_copies, layer=l)
        started[i] = (build, srcs) + _split_start(build, 4 * len(srcs), srcs, lands, after, f"gather_start_{l}_{g}")
        return started[i][-1]

    def finish_gather(i, after):
        l, g = order[i]
        build, srcs, sems, lands, _ = started.pop(i)
        lands = _split_wait(build, srcs, lands, sems, after, f"gather_wait_{l}_{g}")
        got = _forward_halves(lands, f"forward_halves_{l}_{g}")
        token = start_gather(i + GATHER_AHEAD, got[0]) if i + GATHER_AHEAD < len(order) else None
        return got, token

    token = conv_full
    for i in range(min(GATHER_AHEAD, len(order))):
        token = start_gather(i, token)

    xs = x.reshape(T, D)
    saved = []
    for l in range(L):
        small = dict(pool_w=pool_w[l], pool_b=pool_b[l].reshape(G, 1, LANES), pool_scale=pool_scale[l].reshape(1, pw),
                     conv_w=conv_full[l], conv_b=conv_b[l].reshape(1, lw), wa=lru_w_a[l],
                     ba=lru_b_a[l].reshape(H, 1, LANES), wx=lru_w_x[l], bx=lru_b_x[l].reshape(H, 1, LANES),
                     lam=lru_lambda[l].reshape(1, lw))
        g1, g2, g3 = norm_ffn1[l].reshape(1, D), norm_mix[l].reshape(1, D), norm_ffn2[l].reshape(1, D)

        x0 = xs
        (w_up1, w_dn1), token = finish_gather(3 * l, token if l == 0 else x0)
        w_dn1 = w_dn1.reshape(-1, D)
        u1 = _norm_matmul_fwd(x0, tied(g1, token), w_up1, f"ffn1_up_{l}", KEPT)
        x1 = _swiglu_down_fwd(u1, x0, w_dn1, f"ffn1_down_{l}")
        (w_i, w_pu, w_lu, w_o), token = finish_gather(3 * l + 1, x1)
        w_lu, w_o = w_lu.reshape(-1, D), w_o.reshape(-1, D)
        proj = _norm_matmul_fwd(x1, tied(g2, token), w_i, f"mix_in_{l}")
        pm = _pool_fwd(proj, small["pool_w"], small["pool_b"], small["pool_scale"], f"pool_{l}")
        zl, hs = _lru_fwd(proj, small["conv_w"], small["conv_b"], small["wa"], small["ba"], small["wx"], small["bx"],
                          small["lam"], pw, f"lru_{l}")
        x2 = _mix_out_fwd(x1, proj, pm, zl, w_pu, w_lu, w_o, f"mix_out_{l}")
        (w_up2, w_dn2), token = finish_gather(3 * l + 2, x2)
        w_dn2 = w_dn2.reshape(-1, D)
        u2 = _norm_matmul_fwd(x2, tied(g3, token), w_up2, f"ffn2_up_{l}", KEPT)
        xs = _swiglu_down_fwd(u2, x2, w_dn2, f"ffn2_down_{l}")
        saved.append(dict(x0=x0, u1=u1, x1=x1, proj=proj, pm=pm, zl=zl, hs=hs, x2=x2, u2=u2, small=small, g=(g1, g2, g3),
                          w=(w_up1, w_dn1, w_i, w_pu, w_lu, w_o, w_up2, w_dn2)))

    loss_part, dx, d_final = _loss_head(xs, final_norm.reshape(1, D), loss_target.reshape(T, D), "loss_head")
    loss = lax.psum(loss_part[0, 0], ("x", "y", "c"))

    du_spec_ffn = lambda tm, tn: pl.BlockSpec((None, tm, tn), lambda k, i: (k // 2, i, k % 2))
    du_spec_mix = lambda tm, tn: pl.BlockSpec((tm, tn), lambda k, i: (i, k))
    out = {n: [lax.empty(W[n].shape, F32) for _ in range(4)] for n in BIG}
    small_pieces = [None] * L
    stages = [None, None]

    def advance(new):
        token = None
        first = None
        if new is not None:
            l, names, grads = new
            tag = f"{names[0]}_{l}"
            lands = [lax.empty((N_CHIPS, a.shape[1] // 2, a.shape[2]), F32) for a in grads]
            sems, lands, token = _split_start(_halves_copies, len(grads), grads, lands, grads[0], f"halves_start_{tag}")
            first = (l, names, tag, grads, sems, lands)
        second = None
        if stages[0] is not None:
            l, names, tag, grads, sems, lands = stages[0]
            recv = _split_wait(_halves_copies, grads, lands, sems, grads[0] if token is None else token,
                               f"halves_wait_{tag}")
            sums = [_pair_sum(g, r, F32 if n == "small" else XFER, f"pair_sum_{n}_{l}")
                    for g, r, n in zip(grads, recv, names)]
            pbs = [p[1] for p in sums]
            lands = [lax.empty((N_CHIPS - 1,) + p.shape[1:], p.dtype) for p in pbs]
            sems, lands, token = _split_start(_chip_copies, 3 * len(pbs), pbs, lands, pbs[0], f"chips_start_{tag}")
            second = (l, names, tag, [p[0] for p in sums], pbs, sems, lands)
        if stages[1] is not None:
            l, names, tag, pfs, pbs, sems, lands = stages[1]
            got = _split_wait(_chip_copies, pbs, lands, sems, pbs[0] if token is None else token, f"chips_wait_{tag}")
            whole = _join_halves([_chip_sum(p, q, f"chip_sum_{n}_{l}") for p, q, n in zip(pfs, got, names)],
                                 f"join_halves_{tag}")
            for n, g in zip(names, whole):
                if n == "small":
                    small_pieces[l] = g
                else:
                    out[n] = _adamw_layer(W[n], g, M[n], V[n], out[n], l, f"adamw_{n}_{l}")
        stages[:] = [first, second]
        return token

    token = None
    for l in reversed(range(L)):
        s = saved[l]
        w_up1, w_dn1, w_i, w_pu, w_lu, w_o, w_up2, w_dn2 = s["w"]
        g1, g2, g3 = s["g"]
        sm = s["small"]
        du2, d_dn2 = _swiglu_down_bwd(s["u2"], dx, w_dn2, f"ffn2_down_bwd_{l}")
        dx, d_up2, dg3 = _norm_matmul_bwd(s["x2"], tied(g3, token), du2, du_spec_ffn, w_up2, dx, f"ffn2_up_bwd_{l}")
        token = advance((l, GROUPS[2], [d_up2, d_dn2.reshape(N_CHIPS, -1, D)]))
        dgl, dpm, dzl, d_o, d_pu, d_lu = _mix_out_bwd(dx, s["proj"], s["pm"], s["zl"], w_pu, w_lu, w_o, f"mix_out_bwd_{l}")
        du_pool, d_pool_w, d_pool_b, d_pool_scale = _pool_bwd(s["proj"], dpm, sm["pool_w"], sm["pool_b"], sm["pool_scale"],
                                                            f"pool_bwd_{l}")
        dul, dug, d_cw, d_cb, d_wa, d_ba, d_wx, d_bx, d_lam = _lru_bwd(
            s["proj"], s["hs"], dzl, sm["conv_w"], sm["conv_b"], sm["wa"], sm["ba"], sm["wx"], sm["bx"], sm["lam"], pw,
            f"lru_bwd_{l}")
        dproj = jnp.concatenate([du_pool, dul, dug, dgl], axis=1)
        dx, d_in, dg2 = _norm_matmul_bwd(s["x1"], tied(g2, token), dproj, du_spec_mix, w_i, dx, f"mix_in_bwd_{l}")
        token = advance((l, GROUPS[1], [d_in, d_pu, d_lu.reshape(N_CHIPS, -1, D), d_o.reshape(N_CHIPS, -1, D)]))
        du1, d_dn1 = _swiglu_down_bwd(s["u1"], dx, w_dn1, f"ffn1_down_bwd_{l}")
        dx, d_up1, dg1 = _norm_matmul_bwd(s["x0"], tied(g1, token), du1, du_spec_ffn, w_up1, dx, f"ffn1_up_bwd_{l}")
        small = [dg1, dg2, d_pool_w, d_pool_b, d_pool_scale, d_cw, d_cb, d_wa, d_ba, d_wx, d_bx, d_lam, dg3]
        small = _pack(small + ([d_final] if l == L - 1 else []), 2 * 16 * N_CHIPS).reshape(N_CHIPS, -1, LANES)
        token = advance((l, GROUPS[0] + ("small",), [d_up1, d_dn1.reshape(N_CHIPS, -1, D), small]))
    while any(stage is not None for stage in stages):
        advance(None)

    pieces = _gather_small(jnp.concatenate(small_pieces), "gather_small_grads")
    pieces = pieces.reshape(N_CHIPS, L, -1)
    like = {n: jax.ShapeDtypeStruct(W[n].shape[1:] if n != "conv_w" else (CONV_WIDTH, lw), F32) for n in SMALL_LAYER}
    per_layer = []
    for l in range(L):
        names = SMALL_LAYER + (("final_norm",) if l == L - 1 else ())
        shapes = [like[n] if n in like else jax.ShapeDtypeStruct((D,), F32) for n in names]
        per_layer.append(dict(zip(names, _unpack(pieces[:, l], shapes))))
    small_sum = {n: jnp.stack([per_layer[l][n] for l in range(L)]) for n in SMALL_LAYER}
    small_sum["final_norm"] = per_layer[L - 1]["final_norm"]
    cs = conv_w.shape[2]
    small_sum["conv_w"] = lax.dynamic_slice_in_dim(small_sum["conv_w"], chip * cs, cs, axis=2)
    packs = [_pack([d[n] for n in SMALL], 1024) for d in (W, small_sum, M, V)]
    res = _adamw(*packs, "adamw_small")
    like = [W[n] for n in SMALL]
    unpacked = [_unpack(r, like) for r in res]
    for j, n in enumerate(SMALL):
        out[n] = tuple(u[j] for u in unpacked)

    return (loss, dx.reshape(x.shape), *[out[n][0] for n in WEIGHTS], *[out[n][1] for n in WEIGHTS],
            *[out[n][2] for n in WEIGHTS], *[out[n][3] for n in WEIGHTS])
```

```python
import functools

import jax
import jax.numpy as jnp
from jax import lax
from jax.experimental import pallas as pl
from jax.experimental.pallas import tpu as pltpu

F32 = jnp.float32
MM = jnp.bfloat16
XFER = jnp.bfloat16
KEPT = jnp.bfloat16

EPS = 1e-6
LRU_C = 8.0
POOL_WINDOWS = (2, 4, 8, 16)
CONV_WIDTH = 4
ADAM_LR, ADAM_B1, ADAM_B2, ADAM_EPS, ADAM_WD, ADAM_STEP = 0.001, 0.9, 0.999, 1e-08, 0.01, 10

N_CHIPS = 4
LANES = 128
SHIFT_PAD = 8
TOKEN_TILE = 512
VMEM_LIMIT = 60 * 1024 * 1024
MESH = pl.DeviceIdType.MESH
ANY = pl.BlockSpec(memory_space=pl.ANY)


def _params(sem=None):
    return pltpu.CompilerParams(dimension_semantics=sem, vmem_limit_bytes=VMEM_LIMIT)


def _dot(a, b):
    return jnp.dot(a.astype(MM), b.astype(MM), preferred_element_type=F32)


def _dot_nt(a, b):
    return lax.dot_general(a.astype(MM), b.astype(MM), (((1,), (1,)), ((), ())), preferred_element_type=F32)


def _dot_tn(a, b):
    return lax.dot_general(a.astype(MM), b.astype(MM), (((0,), (0,)), ((), ())), preferred_element_type=F32)


def _rms(x, g):
    r = lax.rsqrt(jnp.mean(x * x, axis=-1, keepdims=True) + EPS)
    xh = x * r
    return r, xh, xh * g


def _rms_bwd(dh, xh, r, g):
    dxh = dh * g
    return r * (dxh - xh * jnp.mean(dxh * xh, axis=-1, keepdims=True))


def _accumulate(ref, val, first):
    @pl.when(first)
    def _():
        ref[...] = val

    @pl.when(jnp.logical_not(first))
    def _():
        ref[...] += val


def _row_block(rows, cols, itemsize=4, budget=1 << 20):
    best = None
    for rb in range(16, rows + 1, 16):
        if rows % rb == 0 and rb * cols * itemsize <= budget:
            best = rb
    return best if best is not None else rows


def _norm_matmul_fwd(x, gain, w, name, out_dtype=F32):
    T, D = x.shape
    K, _, tn = w.shape
    tm = min(TOKEN_TILE, T)

    def body(x_ref, g_ref, w_ref, u_ref):
        _, _, h = _rms(x_ref[...], g_ref[...])
        u_ref[...] = _dot(h, w_ref[...]).astype(out_dtype)

    return pl.pallas_call(
        body, name=name, grid=(K, T // tm),
        in_specs=[pl.BlockSpec((tm, D), lambda k, i: (i, 0)), pl.BlockSpec((1, D), lambda k, i: (0, 0)),
                  pl.BlockSpec((None, D, tn), lambda k, i: (k, 0, 0))],
        out_specs=pl.BlockSpec((tm, tn), lambda k, i: (i, k)),
        out_shape=jax.ShapeDtypeStruct((T, K * tn), out_dtype),
        compiler_params=_params(("arbitrary", "arbitrary")),
    )(x, gain, w)


def _norm_matmul_bwd(x, gain, du, du_spec, w, dres, name):
    T, D = x.shape
    K, _, tn = w.shape
    tm = min(TOKEN_TILE, T)
    ni = T // tm

    def body(x_ref, g_ref, du_ref, w_ref, dres_ref, dx_ref, dw_ref, dg_ref, dh_acc, dg_acc):
        k, i = pl.program_id(0), pl.program_id(1)
        g = g_ref[...]
        r, xh, h = _rms(x_ref[...], g)
        du_t = du_ref[...].astype(MM)
        rows = pl.ds(pl.multiple_of(i * tm, tm), tm)
        part = _dot_nt(du_t, w_ref[...])

        @pl.when(k == 0)
        def _():
            dh_acc[rows, :] = part

        @pl.when(k > 0)
        def _():
            dh_acc[rows, :] += part

        _accumulate(dw_ref, _dot_tn(h, du_t), i == 0)

        @pl.when(k == K - 1)
        def _():
            dh = dh_acc[rows, :]
            _accumulate(dg_acc, jnp.sum(dh * xh, axis=0, keepdims=True), i == 0)
            dx_ref[...] = dres_ref[...] + _rms_bwd(dh, xh, r, g)

            @pl.when(i == ni - 1)
            def _():
                dg_ref[...] = dg_acc[...]

    def last(k, i):
        return (jnp.where(k == K - 1, i, 0), 0)

    return pl.pallas_call(
        body, name=name, grid=(K, ni),
        in_specs=[pl.BlockSpec((tm, D), lambda k, i: (i, 0)), pl.BlockSpec((1, D), lambda k, i: (0, 0)),
                  du_spec(tm, tn), pl.BlockSpec((None, D, tn), lambda k, i: (k, 0, 0)),
                  pl.BlockSpec((tm, D), last)],
        out_specs=[pl.BlockSpec((tm, D), last), pl.BlockSpec((None, D, tn), lambda k, i: (k, 0, 0)),
                   pl.BlockSpec((1, D), lambda k, i: (0, 0))],
        out_shape=[jax.ShapeDtypeStruct((T, D), F32), jax.ShapeDtypeStruct((K, D, tn), F32),
                   jax.ShapeDtypeStruct((1, D), F32)],
        scratch_shapes=[pltpu.VMEM((T, D), F32), pltpu.VMEM((1, D), F32)],
        compiler_params=_params(("arbitrary", "arbitrary")),
    )(x, gain, du, w, dres)


def _swiglu_down_fwd(u, x, wd, name):
    T, D = x.shape
    Fh = wd.shape[0]
    tm = min(TOKEN_TILE, T)

    def body(a_ref, b_ref, x_ref, wd_ref, o_ref):
        a = a_ref[...].astype(F32)
        s = a * jax.nn.sigmoid(a) * b_ref[...].astype(F32)
        o_ref[...] = x_ref[...] + 0.5 * _dot(s, wd_ref[...])

    return pl.pallas_call(
        body, name=name, grid=(T // tm,),
        in_specs=[pl.BlockSpec((tm, Fh), lambda i: (i, 0)), pl.BlockSpec((tm, Fh), lambda i: (i, 1)),
                  pl.BlockSpec((tm, D), lambda i: (i, 0)), pl.BlockSpec((Fh, D), lambda i: (0, 0))],
        out_specs=pl.BlockSpec((tm, D), lambda i: (i, 0)),
        out_shape=jax.ShapeDtypeStruct((T, D), F32),
        compiler_params=_params(("arbitrary",)),
    )(u, u, x, wd)


def _swiglu_down_bwd(u, dxn, wd, name):
    T, D = dxn.shape
    Fh = wd.shape[0]
    tm = min(TOKEN_TILE, T)
    nj = 2 if Fh % (2 * LANES) == 0 else 1
    tf = Fh // nj

    def body(a_ref, b_ref, dxn_ref, wd_ref, du_ref, dwd_ref):
        i = pl.program_id(1)
        a, b = a_ref[...].astype(F32), b_ref[...].astype(F32)
        dyh = (0.5 * dxn_ref[...]).astype(MM)
        ds = _dot_nt(dyh, wd_ref[...])
        sig = jax.nn.sigmoid(a)
        sa = a * sig
        _accumulate(dwd_ref, _dot_tn(sa * b, dyh), i == 0)
        du_ref[0] = (ds * b * (sig * (1.0 + a * (1.0 - sig)))).astype(MM)
        du_ref[1] = (ds * sa).astype(MM)

    return pl.pallas_call(
        body, name=name, grid=(nj, T // tm),
        in_specs=[pl.BlockSpec((tm, tf), lambda j, i: (i, j)), pl.BlockSpec((tm, tf), lambda j, i: (i, j + nj)),
                  pl.BlockSpec((tm, D), lambda j, i: (i, 0)), pl.BlockSpec((tf, D), lambda j, i: (j, 0))],
        out_specs=[pl.BlockSpec((2, tm, tf), lambda j, i: (0, i, j)), pl.BlockSpec((tf, D), lambda j, i: (j, 0))],
        out_shape=[jax.ShapeDtypeStruct((2, T, Fh), MM), jax.ShapeDtypeStruct((Fh, D), F32)],
        compiler_params=_params(("arbitrary", "arbitrary")),
    )(u, u, dxn, wd)


def _mix_branches(pm, zl, gp_logit, gr_logit, wpu_ref, wlu_ref):
    y_pool = jnp.concatenate([_dot(pm, wpu_ref[k]) for k in range(N_CHIPS)], axis=1)
    y_lru = _dot(zl, wlu_ref[...])
    return y_pool, y_lru, jax.nn.sigmoid(gp_logit), jax.nn.sigmoid(gr_logit)


def _mix_out_specs(tm, D, pw, lw, cs):
    gate0 = (pw + 2 * lw) // D
    return [pl.BlockSpec((tm, D), lambda i: (i, gate0)), pl.BlockSpec((tm, D), lambda i: (i, gate0 + 1)),
            pl.BlockSpec((tm, pw), lambda i: (i, 0)), pl.BlockSpec((tm, lw), lambda i: (i, 0)),
            pl.BlockSpec((N_CHIPS, pw, cs), lambda i: (0, 0, 0)), pl.BlockSpec((lw, D), lambda i: (0, 0)),
            pl.BlockSpec((D, D), lambda i: (0, 0))]


def _mix_out_fwd(x, proj, pm, zl, wpu, wlu, wo, name):
    T, D = x.shape
    pw, lw, cs = pm.shape[1], zl.shape[1], wpu.shape[2]
    assert (pw + 2 * lw) % D == 0
    tm = min(TOKEN_TILE, T)

    def body(x_ref, gp_ref, gr_ref, pm_ref, zl_ref, wpu_ref, wlu_ref, wo_ref, o_ref):
        y_pool, y_lru, gp, gr = _mix_branches(pm_ref[...], zl_ref[...], gp_ref[...], gr_ref[...], wpu_ref, wlu_ref)
        o_ref[...] = x_ref[...] + _dot(gp * y_pool + gr * y_lru, wo_ref[...])

    return pl.pallas_call(
        body, name=name, grid=(T // tm,),
        in_specs=[pl.BlockSpec((tm, D), lambda i: (i, 0))] + _mix_out_specs(tm, D, pw, lw, cs),
        out_specs=pl.BlockSpec((tm, D), lambda i: (i, 0)),
        out_shape=jax.ShapeDtypeStruct((T, D), F32),
        compiler_params=_params(("arbitrary",)),
    )(x, proj, proj, pm, zl, wpu, wlu, wo)


def _mix_out_bwd(dxn, proj, pm, zl, wpu, wlu, wo, name):
    T, D = dxn.shape
    pw, lw, cs = pm.shape[1], zl.shape[1], wpu.shape[2]
    tm = min(TOKEN_TILE // 2, T)
    ni = T // tm

    def body(dxn_ref, gp_ref, gr_ref, pm_ref, zl_ref, wpu_ref, wlu_ref, wo_ref,
             dgl_ref, dpm_ref, dzl_ref, dwo_hbm, dwpu_hbm, dwlu_hbm, acc_o, acc_pu, acc_lu, sem):
        i = pl.program_id(0)
        pm, zl = pm_ref[...], zl_ref[...]
        y_pool, y_lru, gp, gr = _mix_branches(pm, zl, gp_ref[...], gr_ref[...], wpu_ref, wlu_ref)
        dxn_t = dxn_ref[...].astype(MM)
        dmerged = _dot_nt(dxn_t, wo_ref[...])
        _accumulate(acc_o, _dot_tn(gp * y_pool + gr * y_lru, dxn_t), i == 0)
        dgl_ref[:, :D] = (dmerged * y_pool * (gp * (1.0 - gp))).astype(MM)
        dgl_ref[:, D:] = (dmerged * y_lru * (gr * (1.0 - gr))).astype(MM)
        dyp = (dmerged * gp).astype(MM)
        dyl = (dmerged * gr).astype(MM)
        dpm = None
        for k in range(N_CHIPS):
            dyp_k = dyp[:, k * cs:(k + 1) * cs]
            part = _dot_nt(dyp_k, wpu_ref[k])
            dpm = part if dpm is None else dpm + part
            _accumulate(acc_pu.at[k], _dot_tn(pm, dyp_k), i == 0)
        dpm_ref[...] = dpm
        dzl_ref[...] = _dot_nt(dyl, wlu_ref[...])
        _accumulate(acc_lu, _dot_tn(zl, dyl), i == 0)

        @pl.when(i == ni - 1)
        def _():
            copies = [pltpu.make_async_copy(acc_o, dwo_hbm, sem.at[0]), pltpu.make_async_copy(acc_pu, dwpu_hbm, sem.at[1]),
                      pltpu.make_async_copy(acc_lu, dwlu_hbm, sem.at[2])]
            for cp in copies:
                cp.start()
            for cp in copies:
                cp.wait()

    return pl.pallas_call(
        body, name=name, grid=(ni,),
        in_specs=[pl.BlockSpec((tm, D), lambda i: (i, 0))] + _mix_out_specs(tm, D, pw, lw, cs),
        out_specs=[pl.BlockSpec((tm, 2 * D), lambda i: (i, 0)), pl.BlockSpec((tm, pw), lambda i: (i, 0)),
                   pl.BlockSpec((tm, lw), lambda i: (i, 0)), ANY, ANY, ANY],
        out_shape=[jax.ShapeDtypeStruct((T, 2 * D), MM), jax.ShapeDtypeStruct((T, pw), F32),
                   jax.ShapeDtypeStruct((T, lw), F32), jax.ShapeDtypeStruct((D, D), F32),
                   jax.ShapeDtypeStruct((N_CHIPS, pw, cs), F32), jax.ShapeDtypeStruct((lw, D), F32)],
        scratch_shapes=[pltpu.VMEM((D, D), F32), pltpu.VMEM((N_CHIPS, pw, cs), F32), pltpu.VMEM((lw, D), F32),
                        pltpu.SemaphoreType.DMA((3,))],
        compiler_params=_params(("arbitrary",)),
    )(dxn, proj, proj, pm, zl, wpu, wlu, wo)


def _shifted(buf, val, shift, S):
    buf[pl.ds(SHIFT_PAD, S), :] = val
    return buf[pl.ds(SHIFT_PAD - shift, S), :]


def _zero_pads(buf, S):
    z = jnp.zeros((SHIFT_PAD, buf.shape[1]), F32)
    buf[pl.ds(0, SHIFT_PAD), :] = z
    buf[pl.ds(SHIFT_PAD + S, SHIFT_PAD), :] = z


def _window_sum(buf, val, window, S, lead=False):
    acc, width = val, 1
    while width < window:
        acc = acc + _shifted(buf, acc, -width if lead else width, S)
        width *= 2
    return acc


def _pool_count(S, window):
    t = lax.broadcasted_iota(jnp.int32, (S, LANES), 0)
    return jnp.minimum(t + 1, window).astype(F32)


def _pool_fwd_group(u, g, window, pw_ref, pb_ref, buf, S):
    pooled = _window_sum(buf, u, window, S) / _pool_count(S, window) - u
    return pooled, _dot(pooled, pw_ref[g]) + pb_ref[g]


def _pool_fwd(proj, pool_w, pool_b, pool_scale, name):
    S = proj.shape[0]
    G = pool_w.shape[0]
    pw = G * LANES

    def body(u_ref, pw_ref, pb_ref, ps_ref, pm_ref, buf):
        _zero_pads(buf, S)
        for g in range(G):
            cols = slice(g * LANES, (g + 1) * LANES)
            _, mixed = _pool_fwd_group(u_ref[:, cols], g, POOL_WINDOWS[g], pw_ref, pb_ref, buf, S)
            pm_ref[:, cols] = (mixed * ps_ref[:, cols]).astype(MM)

    return pl.pallas_call(
        body, name=name, grid=(1,),
        in_specs=[pl.BlockSpec((S, pw), lambda i: (0, 0)), pl.BlockSpec((G, LANES, LANES), lambda i: (0, 0, 0)),
                  pl.BlockSpec((G, 1, LANES), lambda i: (0, 0, 0)), pl.BlockSpec((1, pw), lambda i: (0, 0))],
        out_specs=pl.BlockSpec((S, pw), lambda i: (0, 0)),
        out_shape=jax.ShapeDtypeStruct((S, pw), MM),
        scratch_shapes=[pltpu.VMEM((S + 2 * SHIFT_PAD, LANES), F32)],
        compiler_params=_params(("arbitrary",)),
    )(proj, pool_w, pool_b, pool_scale)


def _pool_bwd(proj, dpm, pool_w, pool_b, pool_scale, name):
    S = proj.shape[0]
    G = pool_w.shape[0]
    pw = G * LANES

    def body(u_ref, dpm_ref, pw_ref, pb_ref, ps_ref, du_ref, dpw_ref, dpb_ref, dps_ref, buf):
        _zero_pads(buf, S)
        for g in range(G):
            cols = slice(g * LANES, (g + 1) * LANES)
            window = POOL_WINDOWS[g]
            pooled, mixed = _pool_fwd_group(u_ref[:, cols], g, window, pw_ref, pb_ref, buf, S)
            d_out = dpm_ref[:, cols]
            dmixed = d_out * ps_ref[:, cols]
            dps_ref[:, cols] = jnp.sum(d_out * mixed, axis=0, keepdims=True)
            dpb_ref[g] = jnp.sum(dmixed, axis=0, keepdims=True)
            dpw_ref[g] = _dot_tn(pooled, dmixed)
            dpooled = _dot_nt(dmixed, pw_ref[g])
            q = dpooled / _pool_count(S, window)
            du_ref[:, cols] = (_window_sum(buf, q, window, S, lead=True) - dpooled).astype(MM)

    return pl.pallas_call(
        body, name=name, grid=(1,),
        in_specs=[pl.BlockSpec((S, pw), lambda i: (0, 0)), pl.BlockSpec((S, pw), lambda i: (0, 0)),
                  pl.BlockSpec((G, LANES, LANES), lambda i: (0, 0, 0)),
                  pl.BlockSpec((G, 1, LANES), lambda i: (0, 0, 0)), pl.BlockSpec((1, pw), lambda i: (0, 0))],
        out_specs=[pl.BlockSpec((S, pw), lambda i: (0, 0)), pl.BlockSpec((G, LANES, LANES), lambda i: (0, 0, 0)),
                   pl.BlockSpec((G, 1, LANES), lambda i: (0, 0, 0)), pl.BlockSpec((1, pw), lambda i: (0, 0))],
        out_shape=[jax.ShapeDtypeStruct((S, pw), MM), jax.ShapeDtypeStruct((G, LANES, LANES), F32),
                   jax.ShapeDtypeStruct((G, 1, LANES), F32), jax.ShapeDtypeStruct((1, pw), F32)],
        scratch_shapes=[pltpu.VMEM((S + 2 * SHIFT_PAD, LANES), F32)],
        compiler_params=_params(("arbitrary",)),
    )(proj, dpm, pool_w, pool_b, pool_scale)


def _scan(a, b, bufs, S, reverse=False):
    pad = S // 2
    seq = pl.ds(pad, S)
    A, B = bufs[:2], bufs[2:]
    A[0][seq, :] = a
    B[0][seq, :] = b
    cur, d = 0, 1
    while d < S:
        sh = pl.ds(pad + d if reverse else pad - d, S)
        a_c = A[cur][seq, :]
        B[1 - cur][seq, :] = a_c * B[cur][sh, :] + B[cur][seq, :]
        if 2 * d < S:
            A[1 - cur][seq, :] = a_c * A[cur][sh, :]
        cur, d = 1 - cur, 2 * d
    return B[cur][seq, :]


def _init_scan_pads(bufs, S):
    pad = S // 2
    for n, buf in enumerate(bufs):
        fill = jnp.full((pad, LANES), 1.0 if n < 2 else 0.0, F32)
        buf[pl.ds(0, pad), :] = fill
        buf[pl.ds(pad + S, pad), :] = fill


def _gelu_and_grad(x):
    c = 0.7978845608028654
    x2 = x * x
    th = jnp.tanh(c * (x + 0.044715 * (x2 * x)))
    gelu = 0.5 * x * (1.0 + th)
    grad = 0.5 * (1.0 + th) + 0.5 * x * (1.0 - th * th) * (c * (1.0 + 3 * 0.044715 * x2))
    return gelu, grad


def _lru_head_fwd(ul, cw, cb, wa, ba, wx, bx, lam, sbuf, S):
    sbuf[pl.ds(SHIFT_PAD, S), :] = ul
    v = None
    for k in range(CONV_WIDTH):
        term = sbuf[pl.ds(SHIFT_PAD - (CONV_WIDTH - 1 - k), S), :] * cw[k:k + 1, :]
        v = term if v is None else v + term
    v = v + cb
    r = jax.nn.sigmoid(_dot(v, wa) + ba)
    ig = jax.nn.sigmoid(_dot(v, wx) + bx)
    sp = jax.nn.softplus(-lam)
    log_a = -LRU_C * r * sp
    a = jnp.exp(log_a)
    mult = jnp.sqrt(-jnp.tanh(log_a) * (1.0 + a * a))
    return v, r, ig, sp, a, mult


def _lru_specs(S, H, lw, pw):
    b0 = pw // LANES
    return [pl.BlockSpec((S, LANES), lambda h: (0, b0 + h)), pl.BlockSpec((S, LANES), lambda h: (0, b0 + H + h)),
            pl.BlockSpec((CONV_WIDTH, LANES), lambda h: (0, h)), pl.BlockSpec((1, LANES), lambda h: (0, h)),
            pl.BlockSpec((None, LANES, LANES), lambda h: (h, 0, 0)), pl.BlockSpec((None, 1, LANES), lambda h: (h, 0, 0)),
            pl.BlockSpec((None, LANES, LANES), lambda h: (h, 0, 0)), pl.BlockSpec((None, 1, LANES), lambda h: (h, 0, 0)),
            pl.BlockSpec((1, LANES), lambda h: (0, h))]


def _lru_scratch(S):
    return [pltpu.VMEM((S + 2 * SHIFT_PAD, LANES), F32)] + [pltpu.VMEM((2 * S, LANES), F32)] * 4


def _lru_fwd(proj, conv_w, conv_b, wa, ba, wx, bx, lam, pw, name):
    S = proj.shape[0]
    H = wa.shape[0]
    lw = H * LANES

    def body(ul_ref, ug_ref, cw_ref, cb_ref, wa_ref, ba_ref, wx_ref, bx_ref, lam_ref, zl_ref, h_ref, sbuf, *bufs):
        @pl.when(pl.program_id(0) == 0)
        def _():
            _zero_pads(sbuf, S)
            _init_scan_pads(bufs, S)

        v, _, ig, _, a, mult = _lru_head_fwd(ul_ref[...], cw_ref[...], cb_ref[...], wa_ref[...], ba_ref[...], wx_ref[...],
                                             bx_ref[...], lam_ref[...], sbuf, S)
        h = _scan(a, mult * (ig * v), bufs, S)
        h_ref[...] = h
        zl_ref[...] = (h * jax.nn.gelu(ug_ref[...])).astype(MM)

    col = pl.BlockSpec((S, LANES), lambda h: (0, h))
    return pl.pallas_call(
        body, name=name, grid=(H,),
        in_specs=_lru_specs(S, H, lw, pw),
        out_specs=[col, col],
        out_shape=[jax.ShapeDtypeStruct((S, lw), MM), jax.ShapeDtypeStruct((S, lw), F32)],
        scratch_shapes=_lru_scratch(S),
        compiler_params=_params(("arbitrary",)),
    )(proj, proj, conv_w, conv_b, wa, ba, wx, bx, lam)


def _lru_bwd(proj, h_all, dzl, conv_w, conv_b, wa, ba, wx, bx, lam, pw, name):
    S = proj.shape[0]
    H = wa.shape[0]
    lw = H * LANES

    def body(ul_ref, ug_ref, cw_ref, cb_ref, wa_ref, ba_ref, wx_ref, bx_ref, lam_ref, h_ref, dzl_ref,
             dul_ref, dug_ref, dcw_ref, dcb_ref, dwa_ref, dba_ref, dwx_ref, dbx_ref, dlam_ref, sbuf, *bufs):
        @pl.when(pl.program_id(0) == 0)
        def _():
            _zero_pads(sbuf, S)
            _init_scan_pads(bufs, S)

        ul, cw, lam = ul_ref[...], cw_ref[...], lam_ref[...]
        wa, wx = wa_ref[...], wx_ref[...]
        v, r, ig, sp, a, mult = _lru_head_fwd(ul, cw, cb_ref[...], wa, ba_ref[...], wx, bx_ref[...], lam, sbuf, S)
        h = h_ref[...]
        gelu, gelu_grad = _gelu_and_grad(ug_ref[...])
        dzl_t = dzl_ref[...]
        dug_ref[...] = (dzl_t * h * gelu_grad).astype(MM)
        a_next = _shifted(sbuf, a, -1, S)
        lam_t = _scan(a_next, dzl_t * gelu, bufs, S, reverse=True)
        da = lam_t * _shifted(sbuf, h, 1, S)
        d_iv = lam_t * mult
        d_log_a = da * a - (lam_t * (ig * v)) * (a * a) / mult
        dlam_ref[...] = jnp.sum(d_log_a * r, axis=0, keepdims=True) * (LRU_C * jax.nn.sigmoid(-lam))
        dra = (d_log_a * (-LRU_C * sp)) * (r * (1.0 - r))
        dia = (d_iv * v) * (ig * (1.0 - ig))
        dba_ref[...] = jnp.sum(dra, axis=0, keepdims=True)
        dbx_ref[...] = jnp.sum(dia, axis=0, keepdims=True)
        dwa_ref[...] = _dot_tn(v, dra)
        dwx_ref[...] = _dot_tn(v, dia)
        dv = d_iv * ig + _dot_nt(dra, wa) + _dot_nt(dia, wx)
        dcb_ref[...] = jnp.sum(dv, axis=0, keepdims=True)
        sbuf[pl.ds(SHIFT_PAD, S), :] = ul
        for k in range(CONV_WIDTH):
            dcw_ref[k:k + 1, :] = jnp.sum(dv * sbuf[pl.ds(SHIFT_PAD - (CONV_WIDTH - 1 - k), S), :], axis=0, keepdims=True)
        sbuf[pl.ds(SHIFT_PAD, S), :] = dv
        dul = None
        for k in range(CONV_WIDTH):
            term = sbuf[pl.ds(SHIFT_PAD + (CONV_WIDTH - 1 - k), S), :] * cw[k:k + 1, :]
            dul = term if dul is None else dul + term
        dul_ref[...] = dul.astype(MM)

    head_mat = pl.BlockSpec((None, LANES, LANES), lambda h: (h, 0, 0))
    head_vec = pl.BlockSpec((None, 1, LANES), lambda h: (h, 0, 0))
    col = pl.BlockSpec((S, LANES), lambda h: (0, h))
    row = pl.BlockSpec((1, LANES), lambda h: (0, h))
    return pl.pallas_call(
        body, name=name, grid=(H,),
        in_specs=_lru_specs(S, H, lw, pw) + [col, col],
        out_specs=[col, col, pl.BlockSpec((CONV_WIDTH, LANES), lambda h: (0, h)), row, head_mat, head_vec, head_mat,
                   head_vec, row],
        out_shape=[jax.ShapeDtypeStruct((S, lw), MM), jax.ShapeDtypeStruct((S, lw), MM),
                   jax.ShapeDtypeStruct((CONV_WIDTH, lw), F32), jax.ShapeDtypeStruct((1, lw), F32),
                   jax.ShapeDtypeStruct((H, LANES, LANES), F32), jax.ShapeDtypeStruct((H, 1, LANES), F32),
                   jax.ShapeDtypeStruct((H, LANES, LANES), F32), jax.ShapeDtypeStruct((H, 1, LANES), F32),
                   jax.ShapeDtypeStruct((1, lw), F32)],
        scratch_shapes=_lru_scratch(S),
        compiler_params=_params(("arbitrary",)),
    )(proj, proj, conv_w, conv_b, wa, ba, wx, bx, lam, h_all, dzl)


def _loss_head(x, gain, target, name):
    T, D = x.shape
    tm = min(TOKEN_TILE, T)
    ni = T // tm

    def body(x_ref, g_ref, t_ref, loss_ref, dx_ref, dg_ref, loss_acc, dg_acc):
        i = pl.program_id(0)
        g = g_ref[...]
        r, xh, y = _rms(x_ref[...], g)
        err = y - t_ref[...]
        part = 0.5 * jnp.sum(jnp.mean(err * err, axis=-1, keepdims=True), axis=0, keepdims=True)
        _accumulate(loss_acc, jnp.broadcast_to(part, (1, LANES)), i == 0)
        dy = err * (1.0 / D)
        _accumulate(dg_acc, jnp.sum(dy * xh, axis=0, keepdims=True), i == 0)
        dx_ref[...] = _rms_bwd(dy, xh, r, g)

        @pl.when(i == ni - 1)
        def _():
            loss_ref[...] = loss_acc[...]
            dg_ref[...] = dg_acc[...]

    return pl.pallas_call(
        body, name=name, grid=(ni,),
        in_specs=[pl.BlockSpec((tm, D), lambda i: (i, 0)), pl.BlockSpec((1, D), lambda i: (0, 0)),
                  pl.BlockSpec((tm, D), lambda i: (i, 0))],
        out_specs=[pl.BlockSpec((1, LANES), lambda i: (0, 0)), pl.BlockSpec((tm, D), lambda i: (i, 0)),
                   pl.BlockSpec((1, D), lambda i: (0, 0))],
        out_shape=[jax.ShapeDtypeStruct((1, LANES), F32), jax.ShapeDtypeStruct((T, D), F32),
                   jax.ShapeDtypeStruct((1, D), F32)],
        scratch_shapes=[pltpu.VMEM((1, LANES), F32), pltpu.VMEM((1, D), F32)],
        compiler_params=_params(("arbitrary",)),
    )(x, gain, target)


def _my_core():
    return lax.axis_index("c")


def _my_chip():
    return 2 * lax.axis_index("x") + lax.axis_index("y")


def _pair_sum(a, recv, xfer, name):
    n, R, C = a.shape
    hr = R // 2

    def body(a_ref, r_ref, own_ref, pb_ref):
        s = a_ref[...] + r_ref[...]
        pb_ref[...] = s.astype(xfer)

        @pl.when(pl.program_id(0) == _my_chip())
        def _():
            own_ref[...] = s

    piece = pl.BlockSpec((None, hr, C), lambda k: (k, 0, 0))
    return pl.pallas_call(
        body, name=name, grid=(n,),
        in_specs=[pl.BlockSpec((None, hr, C), lambda k: (k, _my_core(), 0)), piece],
        out_specs=[pl.BlockSpec((hr, C), lambda k: (0, 0)), piece],
        out_shape=[jax.ShapeDtypeStruct((hr, C), F32), jax.ShapeDtypeStruct((n, hr, C), xfer)],
        compiler_params=_params(("arbitrary",)),
    )(a, recv)


def _chip_sum(own, others, name):
    hr, C = own.shape
    rb = _row_block(hr, C)
    nb = hr // rb

    def body(o_ref, q_ref, g_ref):
        g_ref[...] = ((o_ref[...] + q_ref[0].astype(F32)) + q_ref[1].astype(F32)) + q_ref[2].astype(F32)

    return pl.pallas_call(
        body, name=name, grid=(nb,),
        in_specs=[pl.BlockSpec((rb, C), lambda i: (i, 0)),
                  pl.BlockSpec((N_CHIPS - 1, rb, C), lambda i: (0, i, 0))],
        out_specs=pl.BlockSpec((rb, C), lambda i: (_my_core() * nb + i, 0)),
        out_shape=jax.ShapeDtypeStruct((2 * hr, C), F32),
        compiler_params=_params(("arbitrary",)),
    )(own, others)


def _adamw_step(w_ref, g_ref, m_ref, v_ref, go_ref, d_ref, mo_ref, vo_ref):
    g_t = g_ref[...]
    m_t = ADAM_B1 * m_ref[...] + (1.0 - ADAM_B1) * g_t
    v_t = ADAM_B2 * v_ref[...] + (1.0 - ADAM_B2) * (g_t * g_t)
    m_hat = m_t / (1.0 - ADAM_B1 ** ADAM_STEP)
    v_hat = v_t / (1.0 - ADAM_B2 ** ADAM_STEP)
    go_ref[...] = g_t
    d_ref[...] = -ADAM_LR * (m_hat / (jnp.sqrt(v_hat) + ADAM_EPS) + ADAM_WD * w_ref[...])
    mo_ref[...] = m_t
    vo_ref[...] = v_t


def _adamw(w, g, m, v, name):
    R, C = w.shape
    rb = _row_block(R, C)
    blk = pl.BlockSpec((rb, C), lambda i: (i, 0))
    return pl.pallas_call(
        functools.partial(_adamw_step), name=name, grid=(R // rb,), in_specs=[blk] * 4, out_specs=[blk] * 4,
        out_shape=[jax.ShapeDtypeStruct((R, C), F32)] * 4,
        compiler_params=_params(("arbitrary",)),
    )(w, g, m, v)


def _adamw_layer(w, g, m, v, outs, layer, name):
    L, R, C = w.shape
    rb = _row_block(R, C)

    def body(w_ref, g_ref, m_ref, v_ref, *rest):
        _adamw_step(w_ref, g_ref, m_ref, v_ref, *rest[4:])

    stacked = pl.BlockSpec((None, rb, C), lambda i: (layer, i, 0))
    return pl.pallas_call(
        body, name=name, grid=(R // rb,),
        in_specs=[stacked, pl.BlockSpec((rb, C), lambda i: (i, 0)), stacked, stacked] + [ANY] * 4,
        out_specs=[stacked] * 4,
        out_shape=[jax.ShapeDtypeStruct((L, R, C), F32)] * 4,
        input_output_aliases={4 + j: j for j in range(4)},
        compiler_params=_params(("arbitrary",)),
    )(w, g, m, v, *outs)


def _place():
    x, y, c = lax.axis_index("x"), lax.axis_index("y"), lax.axis_index("c")
    others = [(1 - x, y), (x, 1 - y), (1 - x, 1 - y)]
    return x, y, c, 2 * x + y, others


def _half(c, rows):
    return pl.ds(pl.multiple_of(c * (rows // 2), 16), rows // 2)


HBM_SPEC = pl.BlockSpec(memory_space=pltpu.HBM)
SEM_SPEC = pl.BlockSpec(memory_space=pltpu.SEMAPHORE)
SPLIT = pltpu.CompilerParams(has_side_effects=pltpu.SideEffectType.DATAFLOW_SIDE_EFFECTING)
TOKEN = jax.ShapeDtypeStruct((8, LANES), F32)


def _in_hbm(a):
    return pltpu.with_memory_space_constraint(a, pltpu.HBM)


def _split_start(build, count, srcs, lands, after, name):
    ns, nl = len(srcs), len(lands)

    def body(*refs):
        for started, _ in build(refs[:ns], refs[ns:ns + nl], refs[ns + nl + 1], refs[ns + nl + 2]):
            started.start()
        refs[-1][...] = jnp.zeros_like(refs[-1])

    res = pl.pallas_call(
        body, name=name,
        in_specs=[HBM_SPEC] * (ns + nl) + [ANY],
        out_specs=[SEM_SPEC] * 2 + [HBM_SPEC] * nl + [pl.BlockSpec(memory_space=pltpu.VMEM)],
        out_shape=[pltpu.SemaphoreType.DMA((count,))] * 2 + [pltpu.HBM(z.shape, z.dtype) for z in lands] + [TOKEN],
        input_output_aliases={ns + a: 2 + a for a in range(nl)},
        compiler_params=SPLIT,
    )(*[_in_hbm(s) for s in srcs], *[_in_hbm(z) for z in lands], after)
    return res[:2], res[2:2 + nl], res[-1]


def _split_wait(build, srcs, lands, sems, after, name):
    ns, nl = len(srcs), len(lands)

    def body(*refs):
        for started, arriving in build(refs[:ns], refs[ns:ns + nl], refs[ns + nl], refs[ns + nl + 1]):
            started.wait_send()
            arriving.wait_recv()

    return pl.pallas_call(
        body, name=name,
        in_specs=[HBM_SPEC] * (ns + nl) + [SEM_SPEC] * 2 + [ANY],
        out_specs=[HBM_SPEC] * nl,
        out_shape=[pltpu.HBM(z.shape, z.dtype) for z in lands],
        input_output_aliases={ns + a: a for a in range(nl)},
        compiler_params=SPLIT,
    )(*[_in_hbm(s) for s in srcs], *lands, *sems, after)


def _gather_copies(srcs, lands, send, recv, layer):
    x, y, c, k, others = _place()
    pairs = []
    for a in range(len(srcs)):
        rows = _half(c, srcs[a].shape[-2])
        for r, (px, py) in enumerate(others):
            def ici(slot, a=a, r=r, px=px, py=py, rows=rows):
                return pltpu.make_async_remote_copy(src_ref=srcs[a].at[layer].at[rows], dst_ref=lands[a].at[slot].at[rows],
                                                    send_sem=send.at[a * 4 + r], recv_sem=recv.at[a * 4 + r],
                                                    device_id=(px, py, c), device_id_type=MESH)
            pairs.append((ici(k), ici(2 * px + py)))
        own = pltpu.make_async_remote_copy(src_ref=srcs[a].at[layer], dst_ref=lands[a].at[k], send_sem=send.at[a * 4 + 3],
                                           recv_sem=recv.at[a * 4 + 3], device_id=(x, y, 1 - c), device_id_type=MESH)
        pairs.append((own, own))
    return pairs


def _halves_copies(srcs, lands, send, recv):
    x, y, c, _, _ = _place()
    pairs = []
    for a in range(len(srcs)):
        cp = pltpu.make_async_remote_copy(src_ref=srcs[a].at[:, _half(1 - c, srcs[a].shape[1]), :], dst_ref=lands[a],
                                          send_sem=send.at[a], recv_sem=recv.at[a], device_id=(x, y, 1 - c),
                                          device_id_type=MESH)
        pairs.append((cp, cp))
    return pairs


def _chip_copies(srcs, lands, send, recv):
    x, y, c, k, others = _place()
    pairs = []
    for a in range(len(srcs)):
        for r, (px, py) in enumerate(others):
            cp = pltpu.make_async_remote_copy(src_ref=srcs[a].at[2 * px + py], dst_ref=lands[a].at[r],
                                              send_sem=send.at[a * 3 + r], recv_sem=recv.at[a * 3 + r],
                                              device_id=(px, py, c), device_id_type=MESH)
            pairs.append((cp, cp))
    return pairs


def _forward_halves(lands, name):
    n = len(lands)

    def body(*refs):
        outs = refs[n:2 * n]
        send, recv = refs[2 * n:]
        x, y, c, _, others = _place()

        def copy(a, r, half):
            px, py = others[r]
            rows = outs[a].at[2 * px + py].at[_half(half, outs[a].shape[-2])]
            return pltpu.make_async_remote_copy(src_ref=rows, dst_ref=rows, send_sem=send.at[a * 3 + r],
                                                recv_sem=recv.at[a * 3 + r], device_id=(x, y, 1 - c), device_id_type=MESH)

        every = [(a, r) for a in range(n) for r in range(3)]
        for a, r in every:
            copy(a, r, c).start()
        for a, r in every:
            copy(a, r, 1 - c).wait_recv()
        for a, r in every:
            copy(a, r, c).wait_send()

    return pl.pallas_call(
        body, name=name, in_specs=[ANY] * n, out_specs=[ANY] * n,
        out_shape=[jax.ShapeDtypeStruct(z.shape, z.dtype) for z in lands],
        input_output_aliases={a: a for a in range(n)},
        scratch_shapes=[pltpu.SemaphoreType.DMA((3 * n,))] * 2,
    )(*lands)


def _join_halves(fulls, name):
    n = len(fulls)

    def body(*refs):
        outs = refs[n:2 * n]
        send, recv = refs[2 * n:]
        x, y, c, _, _ = _place()

        def copy(a, half):
            rows = outs[a].at[_half(half, outs[a].shape[0]), :]
            return pltpu.make_async_remote_copy(src_ref=rows, dst_ref=rows, send_sem=send.at[a], recv_sem=recv.at[a],
                                                device_id=(x, y, 1 - c), device_id_type=MESH)

        for a in range(n):
            copy(a, c).start()
        for a in range(n):
            copy(a, 1 - c).wait_recv()
        for a in range(n):
            copy(a, c).wait_send()

    return pl.pallas_call(
        body, name=name, in_specs=[ANY] * n, out_specs=[ANY] * n,
        out_shape=[jax.ShapeDtypeStruct(f.shape, f.dtype) for f in fulls],
        input_output_aliases={a: a for a in range(n)},
        scratch_shapes=[pltpu.SemaphoreType.DMA((n,))] * 2,
    )(*fulls)


def _gather_small(small, name):
    def body(in_ref, out_ref, send, recv):
        x, y, c, k, others = _place()
        mine = pltpu.make_async_remote_copy(src_ref=in_ref, dst_ref=out_ref.at[k], send_sem=send.at[3], recv_sem=recv.at[3],
                                            device_id=(x, y, 1 - c), device_id_type=MESH)
        mine.start()
        copies = []
        for r, (px, py) in enumerate(others):
            copies.append(pltpu.make_async_remote_copy(src_ref=in_ref, dst_ref=out_ref.at[k], send_sem=send.at[r],
                                                       recv_sem=recv.at[r], device_id=(px, py, c), device_id_type=MESH))
            copies[-1].start()
        for r, (px, py) in enumerate(others):
            pltpu.make_async_remote_copy(src_ref=in_ref, dst_ref=out_ref.at[2 * px + py], send_sem=send.at[r],
                                         recv_sem=recv.at[r], device_id=(px, py, c), device_id_type=MESH).wait_recv()
        for cp in copies:
            cp.wait_send()
        mine.wait()

    return pl.pallas_call(
        body, name=name, in_specs=[ANY], out_specs=ANY,
        out_shape=jax.ShapeDtypeStruct((N_CHIPS,) + small.shape, small.dtype),
        scratch_shapes=[pltpu.SemaphoreType.DMA((4,))] * 2,
    )(small)


GROUPS = (("ffn1_w_up", "ffn1_w_down"), ("w_in", "w_pool_up", "w_lru_up", "w_out"), ("ffn2_w_up", "ffn2_w_down"))
BIG = GROUPS[0] + GROUPS[1] + GROUPS[2]
SMALL_LAYER = ("norm_ffn1", "norm_mix", "pool_w", "pool_b", "pool_scale", "conv_w", "conv_b", "lru_w_a", "lru_b_a",
               "lru_w_x", "lru_b_x", "lru_lambda", "norm_ffn2")
SMALL = SMALL_LAYER + ("final_norm",)
GATHER_AHEAD = 3
WEIGHTS = ("norm_ffn1", "ffn1_w_up", "ffn1_w_down", "norm_mix", "w_in", "pool_w", "pool_b", "pool_scale", "w_pool_up",
           "conv_w", "conv_b", "lru_w_a", "lru_b_a", "lru_w_x", "lru_b_x", "lru_lambda", "w_lru_up", "w_out", "norm_ffn2",
           "ffn2_w_up", "ffn2_w_down", "final_norm")


def _pack(arrays, rows_multiple):
    flat = jnp.concatenate([a.reshape(-1) for a in arrays])
    rows = -(-flat.shape[0] // LANES)
    rows = -(-rows // rows_multiple) * rows_multiple
    return jnp.pad(flat, (0, rows * LANES - flat.shape[0])).reshape(rows, LANES)


def _unpack(packed, like):
    flat, out, at = packed.reshape(-1), [], 0
    for a in like:
        out.append(flat[at:at + a.size].reshape(a.shape))
        at += a.size
    return out


def kernel(x, norm_ffn1, ffn1_w_up, ffn1_w_down, norm_mix, w_in, pool_w, pool_b, pool_scale, w_pool_up, conv_w, conv_b, lru_w_a, lru_b_a, lru_w_x, lru_b_x, lru_lambda, w_lru_up, w_out, norm_ffn2, ffn2_w_up, ffn2_w_down, final_norm, loss_target, m_norm_ffn1, m_ffn1_w_up, m_ffn1_w_down, m_norm_mix, m_w_in, m_pool_w, m_pool_b, m_pool_scale, m_w_pool_up, m_conv_w, m_conv_b, m_lru_w_a, m_lru_b_a, m_lru_w_x, m_lru_b_x, m_lru_lambda, m_w_lru_up, m_w_out, m_norm_ffn2, m_ffn2_w_up, m_ffn2_w_down, m_final_norm, v_norm_ffn1, v_ffn1_w_up, v_ffn1_w_down, v_norm_mix, v_w_in, v_pool_w, v_pool_b, v_pool_scale, v_w_pool_up, v_conv_w, v_conv_b, v_lru_w_a, v_lru_b_a, v_lru_w_x, v_lru_b_x, v_lru_lambda, v_w_lru_up, v_w_out, v_norm_ffn2, v_ffn2_w_up, v_ffn2_w_down, v_final_norm):
    given = dict(locals())
    W = {n: given[n] for n in WEIGHTS}
    M = {n: given["m_" + n] for n in WEIGHTS}
    V = {n: given["v_" + n] for n in WEIGHTS}
    L = norm_ffn1.shape[0]
    T, D = x.shape[1], x.shape[2]
    G, H = pool_w.shape[1], lru_w_a.shape[1]
    pw, lw = G * LANES, H * LANES
    chip = 2 * lax.axis_index("x") + lax.axis_index("y")

    def tied(gain, token):
        return gain if token is None else gain + token[:1, :1]

    shards = {n: W[n].astype(MM) for n in BIG}
    conv_full = _gather_small(conv_w.reshape(L * CONV_WIDTH, conv_w.shape[2]), "gather_conv_w")
    conv_full = conv_full.reshape(N_CHIPS, L, CONV_WIDTH, -1).transpose(1, 2, 0, 3).reshape(L, CONV_WIDTH, lw)

    order = [(l, g) for l in range(L) for g in range(len(GROUPS))]
    started = {}

    def start_gather(i, after):
        l, g = order[i]
        srcs = [shards[n] for n in GROUPS[g]]
        lands = [lax.empty((N_CHIPS,) + s.shape[1:], s.dtype) for s in srcs]
        build = functools.partial(_gather_copies, layer=l)
        started[i] = (build, srcs) + _split_start(build, 4 * len(srcs), srcs, lands, after, f"gather_start_{l}_{g}")
        return started[i][-1]

    def finish_gather(i, after):
        l, g = order[i]
        build, srcs, sems, lands, _ = started.pop(i)
        lands = _split_wait(build, srcs, lands, sems, after, f"gather_wait_{l}_{g}")
        got = _forward_halves(lands, f"forward_halves_{l}_{g}")
        token = start_gather(i + GATHER_AHEAD, got[0]) if i + GATHER_AHEAD < len(order) else None
        return got, token

    token = conv_full
    for i in range(min(GATHER_AHEAD, len(order))):
        token = start_gather(i, token)

    xs = x.reshape(T, D)
    saved = []
    for l in range(L):
        small = dict(pool_w=pool_w[l], pool_b=pool_b[l].reshape(G, 1, LANES), pool_scale=pool_scale[l].reshape(1, pw),
                     conv_w=conv_full[l], conv_b=conv_b[l].reshape(1, lw), wa=lru_w_a[l],
                     ba=lru_b_a[l].reshape(H, 1, LANES), wx=lru_w_x[l], bx=lru_b_x[l].reshape(H, 1, LANES),
                     lam=lru_lambda[l].reshape(1, lw))
        g1, g2, g3 = norm_ffn1[l].reshape(1, D), norm_mix[l].reshape(1, D), norm_ffn2[l].reshape(1, D)

        x0 = xs
        (w_up1, w_dn1), token = finish_gather(3 * l, token if l == 0 else x0)
        w_dn1 = w_dn1.reshape(-1, D)
        u1 = _norm_matmul_fwd(x0, tied(g1, token), w_up1, f"ffn1_up_{l}", KEPT)
        x1 = _swiglu_down_fwd(u1, x0, w_dn1, f"ffn1_down_{l}")
        (w_i, w_pu, w_lu, w_o), token = finish_gather(3 * l + 1, x1)
        w_lu, w_o = w_lu.reshape(-1, D), w_o.reshape(-1, D)
        proj = _norm_matmul_fwd(x1, tied(g2, token), w_i, f"mix_in_{l}")
        pm = _pool_fwd(proj, small["pool_w"], small["pool_b"], small["pool_scale"], f"pool_{l}")
        zl, hs = _lru_fwd(proj, small["conv_w"], small["conv_b"], small["wa"], small["ba"], small["wx"], small["bx"],
                          small["lam"], pw, f"lru_{l}")
        x2 = _mix_out_fwd(x1, proj, pm, zl, w_pu, w_lu, w_o, f"mix_out_{l}")
        (w_up2, w_dn2), token = finish_gather(3 * l + 2, x2)
        w_dn2 = w_dn2.reshape(-1, D)
        u2 = _norm_matmul_fwd(x2, tied(g3, token), w_up2, f"ffn2_up_{l}", KEPT)
        xs = _swiglu_down_fwd(u2, x2, w_dn2, f"ffn2_down_{l}")
        saved.append(dict(x0=x0, u1=u1, x1=x1, proj=proj, pm=pm, zl=zl, hs=hs, x2=x2, u2=u2, small=small, g=(g1, g2, g3),
                          w=(w_up1, w_dn1, w_i, w_pu, w_lu, w_o, w_up2, w_dn2)))

    loss_part, dx, d_final = _loss_head(xs, final_norm.reshape(1, D), loss_target.reshape(T, D), "loss_head")
    loss = lax.psum(loss_part[0, 0], ("x", "y", "c"))

    du_spec_ffn = lambda tm, tn: pl.BlockSpec((None, tm, tn), lambda k, i: (k // 2, i, k % 2))
    du_spec_mix = lambda tm, tn: pl.BlockSpec((tm, tn), lambda k, i: (i, k))
    out = {n: [lax.empty(W[n].shape, F32) for _ in range(4)] for n in BIG}
    small_pieces = [None] * L
    stages = [None, None]

    def advance(new):
        token = None
        first = None
        if new is not None:
            l, names, grads = new
            tag = f"{names[0]}_{l}"
            lands = [lax.empty((N_CHIPS, a.shape[1] // 2, a.shape[2]), F32) for a in grads]
            sems, lands, token = _split_start(_halves_copies, len(grads), grads, lands, grads[0], f"halves_start_{tag}")
            first = (l, names, tag, grads, sems, lands)
        second = None
        if stages[0] is not None:
            l, names, tag, grads, sems, lands = stages[0]
            recv = _split_wait(_halves_copies, grads, lands, sems, grads[0] if token is None else token,
                               f"halves_wait_{tag}")
            sums = [_pair_sum(g, r, F32 if n == "small" else XFER, f"pair_sum_{n}_{l}")
                    for g, r, n in zip(grads, recv, names)]
            pbs = [p[1] for p in sums]
            lands = [lax.empty((N_CHIPS - 1,) + p.shape[1:], p.dtype) for p in pbs]
            sems, lands, token = _split_start(_chip_copies, 3 * len(pbs), pbs, lands, pbs[0], f"chips_start_{tag}")
            second = (l, names, tag, [p[0] for p in sums], pbs, sems, lands)
        if stages[1] is not None:
            l, names, tag, pfs, pbs, sems, lands = stages[1]
            got = _split_wait(_chip_copies, pbs, lands, sems, pbs[0] if token is None else token, f"chips_wait_{tag}")
            whole = _join_halves([_chip_sum(p, q, f"chip_sum_{n}_{l}") for p, q, n in zip(pfs, got, names)],
                                 f"join_halves_{tag}")
            for n, g in zip(names, whole):
                if n == "small":
                    small_pieces[l] = g
                else:
                    out[n] = _adamw_layer(W[n], g, M[n], V[n], out[n], l, f"adamw_{n}_{l}")
        stages[:] = [first, second]
        return token

    token = None
    for l in reversed(range(L)):
        s = saved[l]
        w_up1, w_dn1, w_i, w_pu, w_lu, w_o, w_up2, w_dn2 = s["w"]
        g1, g2, g3 = s["g"]
        sm = s["small"]
        du2, d_dn2 = _swiglu_down_bwd(s["u2"], dx, w_dn2, f"ffn2_down_bwd_{l}")
        dx, d_up2, dg3 = _norm_matmul_bwd(s["x2"], tied(g3, token), du2, du_spec_ffn, w_up2, dx, f"ffn2_up_bwd_{l}")
        token = advance((l, GROUPS[2], [d_up2, d_dn2.reshape(N_CHIPS, -1, D)]))
        dgl, dpm, dzl, d_o, d_pu, d_lu = _mix_out_bwd(dx, s["proj"], s["pm"], s["zl"], w_pu, w_lu, w_o, f"mix_out_bwd_{l}")
        du_pool, d_pool_w, d_pool_b, d_pool_scale = _pool_bwd(s["proj"], dpm, sm["pool_w"], sm["pool_b"], sm["pool_scale"],
                                                            f"pool_bwd_{l}")
        dul, dug, d_cw, d_cb, d_wa, d_ba, d_wx, d_bx, d_lam = _lru_bwd(
            s["proj"], s["hs"], dzl, sm["conv_w"], sm["conv_b"], sm["wa"], sm["ba"], sm["wx"], sm["bx"], sm["lam"], pw,
            f"lru_bwd_{l}")
        dproj = jnp.concatenate([du_pool, dul, dug, dgl], axis=1)
        dx, d_in, dg2 = _norm_matmul_bwd(s["x1"], tied(g2, token), dproj, du_spec_mix, w_i, dx, f"mix_in_bwd_{l}")
        token = advance((l, GROUPS[1], [d_in, d_pu, d_lu.reshape(N_CHIPS, -1, D), d_o.reshape(N_CHIPS, -1, D)]))
        du1, d_dn1 = _swiglu_down_bwd(s["u1"], dx, w_dn1, f"ffn1_down_bwd_{l}")
        dx, d_up1, dg1 = _norm_matmul_bwd(s["x0"], tied(g1, token), du1, du_spec_ffn, w_up1, dx, f"ffn1_up_bwd_{l}")
        small = [dg1, dg2, d_pool_w, d_pool_b, d_pool_scale, d_cw, d_cb, d_wa, d_ba, d_wx, d_bx, d_lam, dg3]
        small = _pack(small + ([d_final] if l == L - 1 else []), 2 * 16 * N_CHIPS).reshape(N_CHIPS, -1, LANES)
        token = advance((l, GROUPS[0] + ("small",), [d_up1, d_dn1.reshape(N_CHIPS, -1, D), small]))
    while any(stage is not None for stage in stages):
        advance(None)

    pieces = _gather_small(jnp.concatenate(small_pieces), "gather_small_grads")
    pieces = pieces.reshape(N_CHIPS, L, -1)
    like = {n: jax.ShapeDtypeStruct(W[n].shape[1:] if n != "conv_w" else (CONV_WIDTH, lw), F32) for n in SMALL_LAYER}
    per_layer = []
    for l in range(L):
        names = SMALL_LAYER + (("final_norm",) if l == L - 1 else ())
        shapes = [like[n] if n in like else jax.ShapeDtypeStruct((D,), F32) for n in names]
        per_layer.append(dict(zip(names, _unpack(pieces[:, l], shapes))))
    small_sum = {n: jnp.stack([per_layer[l][n] for l in range(L)]) for n in SMALL_LAYER}
    small_sum["final_norm"] = per_layer[L - 1]["final_norm"]
    cs = conv_w.shape[2]
    small_sum["conv_w"] = lax.dynamic_slice_in_dim(small_sum["conv_w"], chip * cs, cs, axis=2)
    packs = [_pack([d[n] for n in SMALL], 1024) for d in (W, small_sum, M, V)]
    res = _adamw(*packs, "adamw_small")
    like = [W[n] for n in SMALL]
    unpacked = [_unpack(r, like) for r in res]
    for j, n in enumerate(SMALL):
        out[n] = tuple(u[j] for u in unpacked)

    return (loss, dx.reshape(x.shape), *[out[n][0] for n in WEIGHTS], *[out[n][1] for n in WEIGHTS],
            *[out[n][2] for n in WEIGHTS], *[out[n][3] for n in WEIGHTS])
```

```python
import functools

import jax
import jax.numpy as jnp
from jax import lax
from jax.experimental import pallas as pl
from jax.experimental.pallas import tpu as pltpu

F32 = jnp.float32
MM = jnp.bfloat16
XFER = jnp.bfloat16
KEPT = jnp.bfloat16

EPS = 1e-6
LRU_C = 8.0
POOL_WINDOWS = (2, 4, 8, 16)
CONV_WIDTH = 4
ADAM_LR, ADAM_B1, ADAM_B2, ADAM_EPS, ADAM_WD, ADAM_STEP = 0.001, 0.9, 0.999, 1e-08, 0.01, 10

N_CHIPS = 4
LANES = 128
SHIFT_PAD = 8
TOKEN_TILE = 512
VMEM_LIMIT = 60 * 1024 * 1024
MESH = pl.DeviceIdType.MESH
ANY = pl.BlockSpec(memory_space=pl.ANY)


def _params(sem=None):
    return pltpu.CompilerParams(dimension_semantics=sem, vmem_limit_bytes=VMEM_LIMIT)


def _dot(a, b):
    return jnp.dot(a.astype(MM), b.astype(MM), preferred_element_type=F32)


def _dot_nt(a, b):
    return lax.dot_general(a.astype(MM), b.astype(MM), (((1,), (1,)), ((), ())), preferred_element_type=F32)


def _dot_tn(a, b):
    return lax.dot_general(a.astype(MM), b.astype(MM), (((0,), (0,)), ((), ())), preferred_element_type=F32)


def _rms(x, g):
    r = lax.rsqrt(jnp.mean(x * x, axis=-1, keepdims=True) + EPS)
    xh = x * r
    return r, xh, xh * g


def _rms_bwd(dh, xh, r, g):
    dxh = dh * g
    return r * (dxh - xh * jnp.mean(dxh * xh, axis=-1, keepdims=True))


def _accumulate(ref, val, first):
    @pl.when(first)
    def _():
        ref[...] = val

    @pl.when(jnp.logical_not(first))
    def _():
        ref[...] += val


def _row_block(rows, cols, itemsize=4, budget=1 << 20):
    best = None
    for rb in range(16, rows + 1, 16):
        if rows % rb == 0 and rb * cols * itemsize <= budget:
            best = rb
    return best if best is not None else rows


def _norm_matmul_fwd(x, gain, w, name, out_dtype=F32):
    T, D = x.shape
    K, _, tn = w.shape
    tm = min(TOKEN_TILE, T)

    def body(x_ref, g_ref, w_ref, u_ref):
        _, _, h = _rms(x_ref[...], g_ref[...])
        u_ref[...] = _dot(h, w_ref[...]).astype(out_dtype)

    return pl.pallas_call(
        body, name=name, grid=(K, T // tm),
        in_specs=[pl.BlockSpec((tm, D), lambda k, i: (i, 0)), pl.BlockSpec((1, D), lambda k, i: (0, 0)),
                  pl.BlockSpec((None, D, tn), lambda k, i: (k, 0, 0))],
        out_specs=pl.BlockSpec((tm, tn), lambda k, i: (i, k)),
        out_shape=jax.ShapeDtypeStruct((T, K * tn), out_dtype),
        compiler_params=_params(("arbitrary", "arbitrary")),
    )(x, gain, w)


def _norm_matmul_bwd(x, gain, du, du_spec, w, dres, name):
    T, D = x.shape
    K, _, tn = w.shape
    tm = min(TOKEN_TILE, T)
    ni = T // tm

    def body(x_ref, g_ref, du_ref, w_ref, dres_ref, dx_ref, dw_ref, dg_ref, dh_acc, dg_acc):
        k, i = pl.program_id(0), pl.program_id(1)
        g = g_ref[...]
        r, xh, h = _rms(x_ref[...], g)
        du_t = du_ref[...].astype(MM)
        rows = pl.ds(pl.multiple_of(i * tm, tm), tm)
        part = _dot_nt(du_t, w_ref[...])

        @pl.when(k == 0)
        def _():
            dh_acc[rows, :] = part

        @pl.when(k > 0)
        def _():
            dh_acc[rows, :] += part

        _accumulate(dw_ref, _dot_tn(h, du_t), i == 0)

        @pl.when(k == K - 1)
        def _():
            dh = dh_acc[rows, :]
            _accumulate(dg_acc, jnp.sum(dh * xh, axis=0, keepdims=True), i == 0)
            dx_ref[...] = dres_ref[...] + _rms_bwd(dh, xh, r, g)

            @pl.when(i == ni - 1)
            def _():
                dg_ref[...] = dg_acc[...]

    def last(k, i):
        return (jnp.where(k == K - 1, i, 0), 0)

    return pl.pallas_call(
        body, name=name, grid=(K, ni),
        in_specs=[pl.BlockSpec((tm, D), lambda k, i: (i, 0)), pl.BlockSpec((1, D), lambda k, i: (0, 0)),
                  du_spec(tm, tn), pl.BlockSpec((None, D, tn), lambda k, i: (k, 0, 0)),
                  pl.BlockSpec((tm, D), last)],
        out_specs=[pl.BlockSpec((tm, D), last), pl.BlockSpec((None, D, tn), lambda k, i: (k, 0, 0)),
                   pl.BlockSpec((1, D), lambda k, i: (0, 0))],
        out_shape=[jax.ShapeDtypeStruct((T, D), F32), jax.ShapeDtypeStruct((K, D, tn), F32),
                   jax.ShapeDtypeStruct((1, D), F32)],
        scratch_shapes=[pltpu.VMEM((T, D), F32), pltpu.VMEM((1, D), F32)],
        compiler_params=_params(("arbitrary", "arbitrary")),
    )(x, gain, du, w, dres)


def _swiglu_down_fwd(u, x, wd, name):
    T, D = x.shape
    Fh = wd.shape[0]
    tm = min(TOKEN_TILE, T)

    def body(a_ref, b_ref, x_ref, wd_ref, o_ref):
        a = a_ref[...].astype(F32)
        s = a * jax.nn.sigmoid(a) * b_ref[...].astype(F32)
        o_ref[...] = x_ref[...] + 0.5 * _dot(s, wd_ref[...])

    return pl.pallas_call(
        body, name=name, grid=(T // tm,),
        in_specs=[pl.BlockSpec((tm, Fh), lambda i: (i, 0)), pl.BlockSpec((tm, Fh), lambda i: (i, 1)),
                  pl.BlockSpec((tm, D), lambda i: (i, 0)), pl.BlockSpec((Fh, D), lambda i: (0, 0))],
        out_specs=pl.BlockSpec((tm, D), lambda i: (i, 0)),
        out_shape=jax.ShapeDtypeStruct((T, D), F32),
        compiler_params=_params(("arbitrary",)),
    )(u, u, x, wd)


def _swiglu_down_bwd(u, dxn, wd, name):
    T, D = dxn.shape
    Fh = wd.shape[0]
    tm = min(TOKEN_TILE, T)
    nj = 2 if Fh % (2 * LANES) == 0 else 1
    tf = Fh // nj

    def body(a_ref, b_ref, dxn_ref, wd_ref, du_ref, dwd_ref):
        i = pl.program_id(1)
        a, b = a_ref[...].astype(F32), b_ref[...].astype(F32)
        dyh = (0.5 * dxn_ref[...]).astype(MM)
        ds = _dot_nt(dyh, wd_ref[...])
        sig = jax.nn.sigmoid(a)
        sa = a * sig
        _accumulate(dwd_ref, _dot_tn(sa * b, dyh), i == 0)
        du_ref[0] = (ds * b * (sig * (1.0 + a * (1.0 - sig)))).astype(MM)
        du_ref[1] = (ds * sa).astype(MM)

    return pl.pallas_call(
        body, name=name, grid=(nj, T // tm),
        in_specs=[pl.BlockSpec((tm, tf), lambda j, i: (i, j)), pl.BlockSpec((tm, tf), lambda j, i: (i, j + nj)),
                  pl.BlockSpec((tm, D), lambda j, i: (i, 0)), pl.BlockSpec((tf, D), lambda j, i: (j, 0))],
        out_specs=[pl.BlockSpec((2, tm, tf), lambda j, i: (0, i, j)), pl.BlockSpec((tf, D), lambda j, i: (j, 0))],
        out_shape=[jax.ShapeDtypeStruct((2, T, Fh), MM), jax.ShapeDtypeStruct((Fh, D), F32)],
        compiler_params=_params(("arbitrary", "arbitrary")),
    )(u, u, dxn, wd)


def _mix_branches(pm, zl, gp_logit, gr_logit, wpu_ref, wlu_ref):
    y_pool = jnp.concatenate([_dot(pm, wpu_ref[k]) for k in range(N_CHIPS)], axis=1)
    y_lru = _dot(zl, wlu_ref[...])
    return y_pool, y_lru, jax.nn.sigmoid(gp_logit), jax.nn.sigmoid(gr_logit)


def _mix_out_specs(tm, D, pw, lw, cs):
    gate0 = (pw + 2 * lw) // D
    return [pl.BlockSpec((tm, D), lambda i: (i, gate0)), pl.BlockSpec((tm, D), lambda i: (i, gate0 + 1)),
            pl.BlockSpec((tm, pw), lambda i: (i, 0)), pl.BlockSpec((tm, lw), lambda i: (i, 0)),
            pl.BlockSpec((N_CHIPS, pw, cs), lambda i: (0, 0, 0)), pl.BlockSpec((lw, D), lambda i: (0, 0)),
            pl.BlockSpec((D, D), lambda i: (0, 0))]


def _mix_out_fwd(x, proj, pm, zl, wpu, wlu, wo, name):
    T, D = x.shape
    pw, lw, cs = pm.shape[1], zl.shape[1], wpu.shape[2]
    assert (pw + 2 * lw) % D == 0
    tm = min(TOKEN_TILE, T)

    def body(x_ref, gp_ref, gr_ref, pm_ref, zl_ref, wpu_ref, wlu_ref, wo_ref, o_ref):
        y_pool, y_lru, gp, gr = _mix_branches(pm_ref[...], zl_ref[...], gp_ref[...], gr_ref[...], wpu_ref, wlu_ref)
        o_ref[...] = x_ref[...] + _dot(gp * y_pool + gr * y_lru, wo_ref[...])

    return pl.pallas_call(
        body, name=name, grid=(T // tm,),
        in_specs=[pl.BlockSpec((tm, D), lambda i: (i, 0))] + _mix_out_specs(tm, D, pw, lw, cs),
        out_specs=pl.BlockSpec((tm, D), lambda i: (i, 0)),
        out_shape=jax.ShapeDtypeStruct((T, D), F32),
        compiler_params=_params(("arbitrary",)),
    )(x, proj, proj, pm, zl, wpu, wlu, wo)


def _mix_out_bwd(dxn, proj, pm, zl, wpu, wlu, wo, name):
    T, D = dxn.shape
    pw, lw, cs = pm.shape[1], zl.shape[1], wpu.shape[2]
    tm = min(TOKEN_TILE // 2, T)
    ni = T // tm

    def body(dxn_ref, gp_ref, gr_ref, pm_ref, zl_ref, wpu_ref, wlu_ref, wo_ref,
             dgl_ref, dpm_ref, dzl_ref, dwo_hbm, dwpu_hbm, dwlu_hbm, acc_o, acc_pu, acc_lu, sem):
        i = pl.program_id(0)
        pm, zl = pm_ref[...], zl_ref[...]
        y_pool, y_lru, gp, gr = _mix_branches(pm, zl, gp_ref[...], gr_ref[...], wpu_ref, wlu_ref)
        dxn_t = dxn_ref[...].astype(MM)
        dmerged = _dot_nt(dxn_t, wo_ref[...])
        _accumulate(acc_o, _dot_tn(gp * y_pool + gr * y_lru, dxn_t), i == 0)
        dgl_ref[:, :D] = (dmerged * y_pool * (gp * (1.0 - gp))).astype(MM)
        dgl_ref[:, D:] = (dmerged * y_lru * (gr * (1.0 - gr))).astype(MM)
        dyp = (dmerged * gp).astype(MM)
        dyl = (dmerged * gr).astype(MM)
        dpm = None
        for k in range(N_CHIPS):
            dyp_k = dyp[:, k * cs:(k + 1) * cs]
            part = _dot_nt(dyp_k, wpu_ref[k])
            dpm = part if dpm is None else dpm + part
            _accumulate(acc_pu.at[k], _dot_tn(pm, dyp_k), i == 0)
        dpm_ref[...] = dpm
        dzl_ref[...] = _dot_nt(dyl, wlu_ref[...])
        _accumulate(acc_lu, _dot_tn(zl, dyl), i == 0)

        @pl.when(i == ni - 1)
        def _():
            copies = [pltpu.make_async_copy(acc_o, dwo_hbm, sem.at[0]), pltpu.make_async_copy(acc_pu, dwpu_hbm, sem.at[1]),
                      pltpu.make_async_copy(acc_lu, dwlu_hbm, sem.at[2])]
            for cp in copies:
                cp.start()
            for cp in copies:
                cp.wait()

    return pl.pallas_call(
        body, name=name, grid=(ni,),
        in_specs=[pl.BlockSpec((tm, D), lambda i: (i, 0))] + _mix_out_specs(tm, D, pw, lw, cs),
        out_specs=[pl.BlockSpec((tm, 2 * D), lambda i: (i, 0)), pl.BlockSpec((tm, pw), lambda i: (i, 0)),
                   pl.BlockSpec((tm, lw), lambda i: (i, 0)), ANY, ANY, ANY],
        out_shape=[jax.ShapeDtypeStruct((T, 2 * D), MM), jax.ShapeDtypeStruct((T, pw), F32),
                   jax.ShapeDtypeStruct((T, lw), F32), jax.ShapeDtypeStruct((D, D), F32),
                   jax.ShapeDtypeStruct((N_CHIPS, pw, cs), F32), jax.ShapeDtypeStruct((lw, D), F32)],
        scratch_shapes=[pltpu.VMEM((D, D), F32), pltpu.VMEM((N_CHIPS, pw, cs), F32), pltpu.VMEM((lw, D), F32),
                        pltpu.SemaphoreType.DMA((3,))],
        compiler_params=_params(("arbitrary",)),
    )(dxn, proj, proj, pm, zl, wpu, wlu, wo)


def _shifted(buf, val, shift, S):
    buf[pl.ds(SHIFT_PAD, S), :] = val
    return buf[pl.ds(SHIFT_PAD - shift, S), :]


def _zero_pads(buf, S):
    z = jnp.zeros((SHIFT_PAD, buf.shape[1]), F32)
    buf[pl.ds(0, SHIFT_PAD), :] = z
    buf[pl.ds(SHIFT_PAD + S, SHIFT_PAD), :] = z


def _window_sum(buf, val, window, S, lead=False):
    acc, width = val, 1
    while width < window:
        acc = acc + _shifted(buf, acc, -width if lead else width, S)
        width *= 2
    return acc


def _pool_count(S, window):
    t = lax.broadcasted_iota(jnp.int32, (S, LANES), 0)
    return jnp.minimum(t + 1, window).astype(F32)


def _pool_fwd_group(u, g, window, pw_ref, pb_ref, buf, S):
    pooled = _window_sum(buf, u, window, S) / _pool_count(S, window) - u
    return pooled, _dot(pooled, pw_ref[g]) + pb_ref[g]


def _pool_fwd(proj, pool_w, pool_b, pool_scale, name):
    S = proj.shape[0]
    G = pool_w.shape[0]
    pw = G * LANES

    def body(u_ref, pw_ref, pb_ref, ps_ref, pm_ref, buf):
        _zero_pads(buf, S)
        for g in range(G):
            cols = slice(g * LANES, (g + 1) * LANES)
            _, mixed = _pool_fwd_group(u_ref[:, cols], g, POOL_WINDOWS[g], pw_ref, pb_ref, buf, S)
            pm_ref[:, cols] = (mixed * ps_ref[:, cols]).astype(MM)

    return pl.pallas_call(
        body, name=name, grid=(1,),
        in_specs=[pl.BlockSpec((S, pw), lambda i: (0, 0)), pl.BlockSpec((G, LANES, LANES), lambda i: (0, 0, 0)),
                  pl.BlockSpec((G, 1, LANES), lambda i: (0, 0, 0)), pl.BlockSpec((1, pw), lambda i: (0, 0))],
        out_specs=pl.BlockSpec((S, pw), lambda i: (0, 0)),
        out_shape=jax.ShapeDtypeStruct((S, pw), MM),
        scratch_shapes=[pltpu.VMEM((S + 2 * SHIFT_PAD, LANES), F32)],
        compiler_params=_params(("arbitrary",)),
    )(proj, pool_w, pool_b, pool_scale)


def _pool_bwd(proj, dpm, pool_w, pool_b, pool_scale, name):
    S = proj.shape[0]
    G = pool_w.shape[0]
    pw = G * LANES

    def body(u_ref, dpm_ref, pw_ref, pb_ref, ps_ref, du_ref, dpw_ref, dpb_ref, dps_ref, buf):
        _zero_pads(buf, S)
        for g in range(G):
            cols = slice(g * LANES, (g + 1) * LANES)
            window = POOL_WINDOWS[g]
            pooled, mixed = _pool_fwd_group(u_ref[:, cols], g, window, pw_ref, pb_ref, buf, S)
            d_out = dpm_ref[:, cols]
            dmixed = d_out * ps_ref[:, cols]
            dps_ref[:, cols] = jnp.sum(d_out * mixed, axis=0, keepdims=True)
            dpb_ref[g] = jnp.sum(dmixed, axis=0, keepdims=True)
            dpw_ref[g] = _dot_tn(pooled, dmixed)
            dpooled = _dot_nt(dmixed, pw_ref[g])
            q = dpooled / _pool_count(S, window)
            du_ref[:, cols] = (_window_sum(buf, q, window, S, lead=True) - dpooled).astype(MM)

    return pl.pallas_call(
        body, name=name, grid=(1,),
        in_specs=[pl.BlockSpec((S, pw), lambda i: (0, 0)), pl.BlockSpec((S, pw), lambda i: (0, 0)),
                  pl.BlockSpec((G, LANES, LANES), lambda i: (0, 0, 0)),
                  pl.BlockSpec((G, 1, LANES), lambda i: (0, 0, 0)), pl.BlockSpec((1, pw), lambda i: (0, 0))],
        out_specs=[pl.BlockSpec((S, pw), lambda i: (0, 0)), pl.BlockSpec((G, LANES, LANES), lambda i: (0, 0, 0)),
                   pl.BlockSpec((G, 1, LANES), lambda i: (0, 0, 0)), pl.BlockSpec((1, pw), lambda i: (0, 0))],
        out_shape=[jax.ShapeDtypeStruct((S, pw), MM), jax.ShapeDtypeStruct((G, LANES, LANES), F32),
                   jax.ShapeDtypeStruct((G, 1, LANES), F32), jax.ShapeDtypeStruct((1, pw), F32)],
        scratch_shapes=[pltpu.VMEM((S + 2 * SHIFT_PAD, LANES), F32)],
        compiler_params=_params(("arbitrary",)),
    )(proj, dpm, pool_w, pool_b, pool_scale)


def _scan(a, b, bufs, S, reverse=False):
    pad = S // 2
    seq = pl.ds(pad, S)
    A, B = bufs[:2], bufs[2:]
    A[0][seq, :] = a
    B[0][seq, :] = b
    cur, d = 0, 1
    while d < S:
        sh = pl.ds(pad + d if reverse else pad - d, S)
        a_c = A[cur][seq, :]
        B[1 - cur][seq, :] = a_c * B[cur][sh, :] + B[cur][seq, :]
        if 2 * d < S:
            A[1 - cur][seq, :] = a_c * A[cur][sh, :]
        cur, d = 1 - cur, 2 * d
    return B[cur][seq, :]


def _init_scan_pads(bufs, S):
    pad = S // 2
    for n, buf in enumerate(bufs):
        fill = jnp.full((pad, LANES), 1.0 if n < 2 else 0.0, F32)
        buf[pl.ds(0, pad), :] = fill
        buf[pl.ds(pad + S, pad), :] = fill


def _gelu_and_grad(x):
    c = 0.7978845608028654
    x2 = x * x
    th = jnp.tanh(c * (x + 0.044715 * (x2 * x)))
    gelu = 0.5 * x * (1.0 + th)
    grad = 0.5 * (1.0 + th) + 0.5 * x * (1.0 - th * th) * (c * (1.0 + 3 * 0.044715 * x2))
    return gelu, grad


def _lru_head_fwd(ul, cw, cb, wa, ba, wx, bx, lam, sbuf, S):
    sbuf[pl.ds(SHIFT_PAD, S), :] = ul
    v = None
    for k in range(CONV_WIDTH):
        term = sbuf[pl.ds(SHIFT_PAD - (CONV_WIDTH - 1 - k), S), :] * cw[k:k + 1, :]
        v = term if v is None else v + term
    v = v + cb
    r = jax.nn.sigmoid(_dot(v, wa) + ba)
    ig = jax.nn.sigmoid(_dot(v, wx) + bx)
    sp = jax.nn.softplus(-lam)
    log_a = -LRU_C * r * sp
    a = jnp.exp(log_a)
    mult = jnp.sqrt(-jnp.tanh(log_a) * (1.0 + a * a))
    return v, r, ig, sp, a, mult


def _lru_specs(S, H, lw, pw):
    b0 = pw // LANES
    return [pl.BlockSpec((S, LANES), lambda h: (0, b0 + h)), pl.BlockSpec((S, LANES), lambda h: (0, b0 + H + h)),
            pl.BlockSpec((CONV_WIDTH, LANES), lambda h: (0, h)), pl.BlockSpec((1, LANES), lambda h: (0, h)),
            pl.BlockSpec((None, LANES, LANES), lambda h: (h, 0, 0)), pl.BlockSpec((None, 1, LANES), lambda h: (h, 0, 0)),
            pl.BlockSpec((None, LANES, LANES), lambda h: (h, 0, 0)), pl.BlockSpec((None, 1, LANES), lambda h: (h, 0, 0)),
            pl.BlockSpec((1, LANES), lambda h: (0, h))]


def _lru_scratch(S):
    return [pltpu.VMEM((S + 2 * SHIFT_PAD, LANES), F32)] + [pltpu.VMEM((2 * S, LANES), F32)] * 4


def _lru_fwd(proj, conv_w, conv_b, wa, ba, wx, bx, lam, pw, name):
    S = proj.shape[0]
    H = wa.shape[0]
    lw = H * LANES

    def body(ul_ref, ug_ref, cw_ref, cb_ref, wa_ref, ba_ref, wx_ref, bx_ref, lam_ref, zl_ref, h_ref, sbuf, *bufs):
        @pl.when(pl.program_id(0) == 0)
        def _():
            _zero_pads(sbuf, S)
            _init_scan_pads(bufs, S)

        v, _, ig, _, a, mult = _lru_head_fwd(ul_ref[...], cw_ref[...], cb_ref[...], wa_ref[...], ba_ref[...], wx_ref[...],
                                             bx_ref[...], lam_ref[...], sbuf, S)
        h = _scan(a, mult * (ig * v), bufs, S)
        h_ref[...] = h
        zl_ref[...] = (h * jax.nn.gelu(ug_ref[...])).astype(MM)

    col = pl.BlockSpec((S, LANES), lambda h: (0, h))
    return pl.pallas_call(
        body, name=name, grid=(H,),
        in_specs=_lru_specs(S, H, lw, pw),
        out_specs=[col, col],
        out_shape=[jax.ShapeDtypeStruct((S, lw), MM), jax.ShapeDtypeStruct((S, lw), F32)],
        scratch_shapes=_lru_scratch(S),
        compiler_params=_params(("arbitrary",)),
    )(proj, proj, conv_w, conv_b, wa, ba, wx, bx, lam)


def _lru_bwd(proj, h_all, dzl, conv_w, conv_b, wa, ba, wx, bx, lam, pw, name):
    S = proj.shape[0]
    H = wa.shape[0]
    lw = H * LANES

    def body(ul_ref, ug_ref, cw_ref, cb_ref, wa_ref, ba_ref, wx_ref, bx_ref, lam_ref, h_ref, dzl_ref,
             dul_ref, dug_ref, dcw_ref, dcb_ref, dwa_ref, dba_ref, dwx_ref, dbx_ref, dlam_ref, sbuf, *bufs):
        @pl.when(pl.program_id(0) == 0)
        def _():
            _zero_pads(sbuf, S)
            _init_scan_pads(bufs, S)

        ul, cw, lam = ul_ref[...], cw_ref[...], lam_ref[...]
        wa, wx = wa_ref[...], wx_ref[...]
        v, r, ig, sp, a, mult = _lru_head_fwd(ul, cw, cb_ref[...], wa, ba_ref[...], wx, bx_ref[...], lam, sbuf, S)
        h = h_ref[...]
        gelu, gelu_grad = _gelu_and_grad(ug_ref[...])
        dzl_t = dzl_ref[...]
        dug_ref[...] = (dzl_t * h * gelu_grad).astype(MM)
        a_next = _shifted(sbuf, a, -1, S)
        lam_t = _scan(a_next, dzl_t * gelu, bufs, S, reverse=True)
        da = lam_t * _shifted(sbuf, h, 1, S)
        d_iv = lam_t * mult
        d_log_a = da * a - (lam_t * (ig * v)) * (a * a) / mult
        dlam_ref[...] = jnp.sum(d_log_a * r, axis=0, keepdims=True) * (LRU_C * jax.nn.sigmoid(-lam))
        dra = (d_log_a * (-LRU_C * sp)) * (r * (1.0 - r))
        dia = (d_iv * v) * (ig * (1.0 - ig))
        dba_ref[...] = jnp.sum(dra, axis=0, keepdims=True)
        dbx_ref[...] = jnp.sum(dia, axis=0, keepdims=True)
        dwa_ref[...] = _dot_tn(v, dra)
        dwx_ref[...] = _dot_tn(v, dia)
        dv = d_iv * ig + _dot_nt(dra, wa) + _dot_nt(dia, wx)
        dcb_ref[...] = jnp.sum(dv, axis=0, keepdims=True)
        sbuf[pl.ds(SHIFT_PAD, S), :] = ul
        for k in range(CONV_WIDTH):
            dcw_ref[k:k + 1, :] = jnp.sum(dv * sbuf[pl.ds(SHIFT_PAD - (CONV_WIDTH - 1 - k), S), :], axis=0, keepdims=True)
        sbuf[pl.ds(SHIFT_PAD, S), :] = dv
        dul = None
        for k in range(CONV_WIDTH):
            term = sbuf[pl.ds(SHIFT_PAD + (CONV_WIDTH - 1 - k), S), :] * cw[k:k + 1, :]
            dul = term if dul is None else dul + term
        dul_ref[...] = dul.astype(MM)

    head_mat = pl.BlockSpec((None, LANES, LANES), lambda h: (h, 0, 0))
    head_vec = pl.BlockSpec((None, 1, LANES), lambda h: (h, 0, 0))
    col = pl.BlockSpec((S, LANES), lambda h: (0, h))
    row = pl.BlockSpec((1, LANES), lambda h: (0, h))
    return pl.pallas_call(
        body, name=name, grid=(H,),
        in_specs=_lru_specs(S, H, lw, pw) + [col, col],
        out_specs=[col, col, pl.BlockSpec((CONV_WIDTH, LANES), lambda h: (0, h)), row, head_mat, head_vec, head_mat,
                   head_vec, row],
        out_shape=[jax.ShapeDtypeStruct((S, lw), MM), jax.ShapeDtypeStruct((S, lw), MM),
                   jax.ShapeDtypeStruct((CONV_WIDTH, lw), F32), jax.ShapeDtypeStruct((1, lw), F32),
                   jax.ShapeDtypeStruct((H, LANES, LANES), F32), jax.ShapeDtypeStruct((H, 1, LANES), F32),
                   jax.ShapeDtypeStruct((H, LANES, LANES), F32), jax.ShapeDtypeStruct((H, 1, LANES), F32),
                   jax.ShapeDtypeStruct((1, lw), F32)],
        scratch_shapes=_lru_scratch(S),
        compiler_params=_params(("arbitrary",)),
    )(proj, proj, conv_w, conv_b, wa, ba, wx, bx, lam, h_all, dzl)


def _loss_head(x, gain, target, name):
    T, D = x.shape
    tm = min(TOKEN_TILE, T)
    ni = T // tm

    def body(x_ref, g_ref, t_ref, loss_ref, dx_ref, dg_ref, loss_acc, dg_acc):
        i = pl.program_id(0)
        g = g_ref[...]
        r, xh, y = _rms(x_ref[...], g)
        err = y - t_ref[...]
        part = 0.5 * jnp.sum(jnp.mean(err * err, axis=-1, keepdims=True), axis=0, keepdims=True)
        _accumulate(loss_acc, jnp.broadcast_to(part, (1, LANES)), i == 0)
        dy = err * (1.0 / D)
        _accumulate(dg_acc, jnp.sum(dy * xh, axis=0, keepdims=True), i == 0)
        dx_ref[...] = _rms_bwd(dy, xh, r, g)

        @pl.when(i == ni - 1)
        def _():
            loss_ref[...] = loss_acc[...]
            dg_ref[...] = dg_acc[...]

    return pl.pallas_call(
        body, name=name, grid=(ni,),
        in_specs=[pl.BlockSpec((tm, D), lambda i: (i, 0)), pl.BlockSpec((1, D), lambda i: (0, 0)),
                  pl.BlockSpec((tm, D), lambda i: (i, 0))],
        out_specs=[pl.BlockSpec((1, LANES), lambda i: (0, 0)), pl.BlockSpec((tm, D), lambda i: (i, 0)),
                   pl.BlockSpec((1, D), lambda i: (0, 0))],
        out_shape=[jax.ShapeDtypeStruct((1, LANES), F32), jax.ShapeDtypeStruct((T, D), F32),
                   jax.ShapeDtypeStruct((1, D), F32)],
        scratch_shapes=[pltpu.VMEM((1, LANES), F32), pltpu.VMEM((1, D), F32)],
        compiler_params=_params(("arbitrary",)),
    )(x, gain, target)


def _my_core():
    return lax.axis_index("c")


def _my_chip():
    return 2 * lax.axis_index("x") + lax.axis_index("y")


def _pair_sum(a, recv, xfer, name):
    n, R, C = a.shape
    hr = R // 2

    def body(a_ref, r_ref, own_ref, pb_ref):
        s = a_ref[...] + r_ref[...]
        pb_ref[...] = s.astype(xfer)

        @pl.when(pl.program_id(0) == _my_chip())
        def _():
            own_ref[...] = s

    piece = pl.BlockSpec((None, hr, C), lambda k: (k, 0, 0))
    return pl.pallas_call(
        body, name=name, grid=(n,),
        in_specs=[pl.BlockSpec((None, hr, C), lambda k: (k, _my_core(), 0)), piece],
        out_specs=[pl.BlockSpec((hr, C), lambda k: (0, 0)), piece],
        out_shape=[jax.ShapeDtypeStruct((hr, C), F32), jax.ShapeDtypeStruct((n, hr, C), xfer)],
        compiler_params=_params(("arbitrary",)),
    )(a, recv)


def _chip_sum(own, others, name):
    hr, C = own.shape
    rb = _row_block(hr, C)
    nb = hr // rb

    def body(o_ref, q_ref, g_ref):
        g_ref[...] = ((o_ref[...] + q_ref[0].astype(F32)) + q_ref[1].astype(F32)) + q_ref[2].astype(F32)

    return pl.pallas_call(
        body, name=name, grid=(nb,),
        in_specs=[pl.BlockSpec((rb, C), lambda i: (i, 0)),
                  pl.BlockSpec((N_CHIPS - 1, rb, C), lambda i: (0, i, 0))],
        out_specs=pl.BlockSpec((rb, C), lambda i: (_my_core() * nb + i, 0)),
        out_shape=jax.ShapeDtypeStruct((2 * hr, C), F32),
        compiler_params=_params(("arbitrary",)),
    )(own, others)


def _adamw_step(w_ref, g_ref, m_ref, v_ref, go_ref, d_ref, mo_ref, vo_ref):
    g_t = g_ref[...]
    m_t = ADAM_B1 * m_ref[...] + (1.0 - ADAM_B1) * g_t
    v_t = ADAM_B2 * v_ref[...] + (1.0 - ADAM_B2) * (g_t * g_t)
    m_hat = m_t / (1.0 - ADAM_B1 ** ADAM_STEP)
    v_hat = v_t / (1.0 - ADAM_B2 ** ADAM_STEP)
    go_ref[...] = g_t
    d_ref[...] = -ADAM_LR * (m_hat / (jnp.sqrt(v_hat) + ADAM_EPS) + ADAM_WD * w_ref[...])
    mo_ref[...] = m_t
    vo_ref[...] = v_t


def _adamw(w, g, m, v, name):
    R, C = w.shape
    rb = _row_block(R, C)
    blk = pl.BlockSpec((rb, C), lambda i: (i, 0))
    return pl.pallas_call(
        functools.partial(_adamw_step), name=name, grid=(R // rb,), in_specs=[blk] * 4, out_specs=[blk] * 4,
        out_shape=[jax.ShapeDtypeStruct((R, C), F32)] * 4,
        compiler_params=_params(("arbitrary",)),
    )(w, g, m, v)


def _adamw_layer(w, g, m, v, outs, layer, name):
    L, R, C = w.shape
    rb = _row_block(R, C)

    def body(w_ref, g_ref, m_ref, v_ref, *rest):
        _adamw_step(w_ref, g_ref, m_ref, v_ref, *rest[4:])

    stacked = pl.BlockSpec((None, rb, C), lambda i: (layer, i, 0))
    return pl.pallas_call(
        body, name=name, grid=(R // rb,),
        in_specs=[stacked, pl.BlockSpec((rb, C), lambda i: (i, 0)), stacked, stacked] + [ANY] * 4,
        out_specs=[stacked] * 4,
        out_shape=[jax.ShapeDtypeStruct((L, R, C), F32)] * 4,
        input_output_aliases={4 + j: j for j in range(4)},
        compiler_params=_params(("arbitrary",)),
    )(w, g, m, v, *outs)


def _place():
    x, y, c = lax.axis_index("x"), lax.axis_index("y"), lax.axis_index("c")
    others = [(1 - x, y), (x, 1 - y), (1 - x, 1 - y)]
    return x, y, c, 2 * x + y, others


def _half(c, rows):
    return pl.ds(pl.multiple_of(c * (rows // 2), 16), rows // 2)


HBM_SPEC = pl.BlockSpec(memory_space=pltpu.HBM)
SEM_SPEC = pl.BlockSpec(memory_space=pltpu.SEMAPHORE)
SPLIT = pltpu.CompilerParams(has_side_effects=pltpu.SideEffectType.DATAFLOW_SIDE_EFFECTING)
TOKEN = jax.ShapeDtypeStruct((8, LANES), F32)


def _in_hbm(a):
    return pltpu.with_memory_space_constraint(a, pltpu.HBM)


def _split_start(build, count, srcs, lands, after, name):
    ns, nl = len(srcs), len(lands)

    def body(*refs):
        for started, _ in build(refs[:ns], refs[ns:ns + nl], refs[ns + nl + 1], refs[ns + nl + 2]):
            started.start()
        refs[-1][...] = jnp.zeros_like(refs[-1])

    res = pl.pallas_call(
        body, name=name,
        in_specs=[HBM_SPEC] * (ns + nl) + [ANY],
        out_specs=[SEM_SPEC] * 2 + [HBM_SPEC] * nl + [pl.BlockSpec(memory_space=pltpu.VMEM)],
        out_shape=[pltpu.SemaphoreType.DMA((count,))] * 2 + [pltpu.HBM(z.shape, z.dtype) for z in lands] + [TOKEN],
        input_output_aliases={ns + a: 2 + a for a in range(nl)},
        compiler_params=SPLIT,
    )(*[_in_hbm(s) for s in srcs], *[_in_hbm(z) for z in lands], after)
    return res[:2], res[2:2 + nl], res[-1]


def _split_wait(build, srcs, lands, sems, after, name):
    ns, nl = len(srcs), len(lands)

    def body(*refs):
        for started, arriving in build(refs[:ns], refs[ns:ns + nl], refs[ns + nl], refs[ns + nl + 1]):
            started.wait_send()
            arriving.wait_recv()

    return pl.pallas_call(
        body, name=name,
        in_specs=[HBM_SPEC] * (ns + nl) + [SEM_SPEC] * 2 + [ANY],
        out_specs=[HBM_SPEC] * nl,
        out_shape=[pltpu.HBM(z.shape, z.dtype) for z in lands],
        input_output_aliases={ns + a: a for a in range(nl)},
        compiler_params=SPLIT,
    )(*[_in_hbm(s) for s in srcs], *lands, *sems, after)


def _gather_copies(srcs, lands, send, recv, layer):
    x, y, c, k, others = _place()
    pairs = []
    for a in range(len(srcs)):
        rows = _half(c, srcs[a].shape[-2])
        for r, (px, py) in enumerate(others):
            def ici(slot, a=a, r=r, px=px, py=py, rows=rows):
                return pltpu.make_async_remote_copy(src_ref=srcs[a].at[layer].at[rows], dst_ref=lands[a].at[slot].at[rows],
                                                    send_sem=send.at[a * 4 + r], recv_sem=recv.at[a * 4 + r],
                                                    device_id=(px, py, c), device_id_type=MESH)
            pairs.append((ici(k), ici(2 * px + py)))
        own = pltpu.make_async_remote_copy(src_ref=srcs[a].at[layer], dst_ref=lands[a].at[k], send_sem=send.at[a * 4 + 3],
                                           recv_sem=recv.at[a * 4 + 3], device_id=(x, y, 1 - c), device_id_type=MESH)
        pairs.append((own, own))
    return pairs


def _halves_copies(srcs, lands, send, recv):
    x, y, c, _, _ = _place()
    pairs = []
    for a in range(len(srcs)):
        cp = pltpu.make_async_remote_copy(src_ref=srcs[a].at[:, _half(1 - c, srcs[a].shape[1]), :], dst_ref=lands[a],
                                          send_sem=send.at[a], recv_sem=recv.at[a], device_id=(x, y, 1 - c),
                                          device_id_type=MESH)
        pairs.append((cp, cp))
    return pairs


def _chip_copies(srcs, lands, send, recv):
    x, y, c, k, others = _place()
    pairs = []
    for a in range(len(srcs)):
        for r, (px, py) in enumerate(others):
            cp = pltpu.make_async_remote_copy(src_ref=srcs[a].at[2 * px + py], dst_ref=lands[a].at[r],
                                              send_sem=send.at[a * 3 + r], recv_sem=recv.at[a * 3 + r],
                                              device_id=(px, py, c), device_id_type=MESH)
            pairs.append((cp, cp))
    return pairs


def _forward_halves(lands, name):
    n = len(lands)

    def body(*refs):
        outs = refs[n:2 * n]
        send, recv = refs[2 * n:]
        x, y, c, _, others = _place()

        def copy(a, r, half):
            px, py = others[r]
            rows = outs[a].at[2 * px + py].at[_half(half, outs[a].shape[-2])]
            return pltpu.make_async_remote_copy(src_ref=rows, dst_ref=rows, send_sem=send.at[a * 3 + r],
                                                recv_sem=recv.at[a * 3 + r], device_id=(x, y, 1 - c), device_id_type=MESH)

        every = [(a, r) for a in range(n) for r in range(3)]
        for a, r in every:
            copy(a, r, c).start()
        for a, r in every:
            copy(a, r, 1 - c).wait_recv()
        for a, r in every:
            copy(a, r, c).wait_send()

    return pl.pallas_call(
        body, name=name, in_specs=[ANY] * n, out_specs=[ANY] * n,
        out_shape=[jax.ShapeDtypeStruct(z.shape, z.dtype) for z in lands],
        input_output_aliases={a: a for a in range(n)},
        scratch_shapes=[pltpu.SemaphoreType.DMA((3 * n,))] * 2,
    )(*lands)


def _join_halves(fulls, name):
    n = len(fulls)

    def body(*refs):
        outs = refs[n:2 * n]
        send, recv = refs[2 * n:]
        x, y, c, _, _ = _place()

        def copy(a, half):
            rows = outs[a].at[_half(half, outs[a].shape[0]), :]
            return pltpu.make_async_remote_copy(src_ref=rows, dst_ref=rows, send_sem=send.at[a], recv_sem=recv.at[a],
                                                device_id=(x, y, 1 - c), device_id_type=MESH)

        for a in range(n):
            copy(a, c).start()
        for a in range(n):
            copy(a, 1 - c).wait_recv()
        for a in range(n):
            copy(a, c).wait_send()

    return pl.pallas_call(
        body, name=name, in_specs=[ANY] * n, out_specs=[ANY] * n,
        out_shape=[jax.ShapeDtypeStruct(f.shape, f.dtype) for f in fulls],
        input_output_aliases={a: a for a in range(n)},
        scratch_shapes=[pltpu.SemaphoreType.DMA((n,))] * 2,
    )(*fulls)


def _gather_small(small, name):
    def body(in_ref, out_ref, send, recv):
        x, y, c, k, others = _place()
        mine = pltpu.make_async_remote_copy(src_ref=in_ref, dst_ref=out_ref.at[k], send_sem=send.at[3], recv_sem=recv.at[3],
                                            device_id=(x, y, 1 - c), device_id_type=MESH)
        mine.start()
        copies = []
        for r, (px, py) in enumerate(others):
            copies.append(pltpu.make_async_remote_copy(src_ref=in_ref, dst_ref=out_ref.at[k], send_sem=send.at[r],
                                                       recv_sem=recv.at[r], device_id=(px, py, c), device_id_type=MESH))
            copies[-1].start()
        for r, (px, py) in enumerate(others):
            pltpu.make_async_remote_copy(src_ref=in_ref, dst_ref=out_ref.at[2 * px + py], send_sem=send.at[r],
                                         recv_sem=recv.at[r], device_id=(px, py, c), device_id_type=MESH).wait_recv()
        for cp in copies:
            cp.wait_send()
        mine.wait()

    return pl.pallas_call(
        body, name=name, in_specs=[ANY], out_specs=ANY,
        out_shape=jax.ShapeDtypeStruct((N_CHIPS,) + small.shape, small.dtype),
        scratch_shapes=[pltpu.SemaphoreType.DMA((4,))] * 2,
    )(small)


GROUPS = (("ffn1_w_up", "ffn1_w_down"), ("w_in", "w_pool_up", "w_lru_up", "w_out"), ("ffn2_w_up", "ffn2_w_down"))
BIG = GROUPS[0] + GROUPS[1] + GROUPS[2]
SMALL_LAYER = ("norm_ffn1", "norm_mix", "pool_w", "pool_b", "pool_scale", "conv_w", "conv_b", "lru_w_a", "lru_b_a",
               "lru_w_x", "lru_b_x", "lru_lambda", "norm_ffn2")
SMALL = SMALL_LAYER + ("final_norm",)
GATHER_AHEAD = 3
WEIGHTS = ("norm_ffn1", "ffn1_w_up", "ffn1_w_down", "norm_mix", "w_in", "pool_w", "pool_b", "pool_scale", "w_pool_up",
           "conv_w", "conv_b", "lru_w_a", "lru_b_a", "lru_w_x", "lru_b_x", "lru_lambda", "w_lru_up", "w_out", "norm_ffn2",
           "ffn2_w_up", "ffn2_w_down", "final_norm")


def _pack(arrays, rows_multiple):
    flat = jnp.concatenate([a.reshape(-1) for a in arrays])
    rows = -(-flat.shape[0] // LANES)
    rows = -(-rows // rows_multiple) * rows_multiple
    return jnp.pad(flat, (0, rows * LANES - flat.shape[0])).reshape(rows, LANES)


def _unpack(packed, like):
    flat, out, at = packed.reshape(-1), [], 0
    for a in like:
        out.append(flat[at:at + a.size].reshape(a.shape))
        at += a.size
    return out


def kernel(x, norm_ffn1, ffn1_w_up, ffn1_w_down, norm_mix, w_in, pool_w, pool_b, pool_scale, w_pool_up, conv_w, conv_b, lru_w_a, lru_b_a, lru_w_x, lru_b_x, lru_lambda, w_lru_up, w_out, norm_ffn2, ffn2_w_up, ffn2_w_down, final_norm, loss_target, m_norm_ffn1, m_ffn1_w_up, m_ffn1_w_down, m_norm_mix, m_w_in, m_pool_w, m_pool_b, m_pool_scale, m_w_pool_up, m_conv_w, m_conv_b, m_lru_w_a, m_lru_b_a, m_lru_w_x, m_lru_b_x, m_lru_lambda, m_w_lru_up, m_w_out, m_norm_ffn2, m_ffn2_w_up, m_ffn2_w_down, m_final_norm, v_norm_ffn1, v_ffn1_w_up, v_ffn1_w_down, v_norm_mix, v_w_in, v_pool_w, v_pool_b, v_pool_scale, v_w_pool_up, v_conv_w, v_conv_b, v_lru_w_a, v_lru_b_a, v_lru_w_x, v_lru_b_x, v_lru_lambda, v_w_lru_up, v_w_out, v_norm_ffn2, v_ffn2_w_up, v_ffn2_w_down, v_final_norm):
    given = dict(locals())
    W = {n: given[n] for n in WEIGHTS}
    M = {n: given["m_" + n] for n in WEIGHTS}
    V = {n: given["v_" + n] for n in WEIGHTS}
    L = norm_ffn1.shape[0]
    T, D = x.shape[1], x.shape[2]
    G, H = pool_w.shape[1], lru_w_a.shape[1]
    pw, lw = G * LANES, H * LANES
    chip = 2 * lax.axis_index("x") + lax.axis_index("y")

    def tied(gain, token):
        return gain if token is None else gain + token[:1, :1]

    shards = {}
    conv_full = _gather_small(conv_w.reshape(L * CONV_WIDTH, conv_w.shape[2]), "gather_conv_w")
    conv_full = conv_full.reshape(N_CHIPS, L, CONV_WIDTH, -1).transpose(1, 2, 0, 3).reshape(L, CONV_WIDTH, lw)

    order = [(l, g) for l in range(L) for g in range(len(GROUPS))]
    started = {}

    def start_gather(i, after, tie=None):
        l, g = order[i]
        for n in GROUPS[g]:
            if n not in shards:
                shards[n] = (W[n] if tie is None else W[n] + tie[:1, :1]).astype(MM)
        srcs = [shards[n] for n in GROUPS[g]]
        lands = [lax.empty((N_CHIPS,) + s.shape[1:], s.dtype) for s in srcs]
        build = functools.partial(_gather_copies, layer=l)
        started[i] = (build, srcs) + _split_start(build, 4 * len(srcs), srcs, lands, after, f"gather_start_{l}_{g}")
        return started[i][-1]

    def finish_gather(i, after):
        l, g = order[i]
        build, srcs, sems, lands, _ = started.pop(i)
        lands = _split_wait(build, srcs, lands, sems, after, f"gather_wait_{l}_{g}")
        got = _forward_halves(lands, f"forward_halves_{l}_{g}")
        token = start_gather(i + GATHER_AHEAD, got[0]) if i + GATHER_AHEAD < len(order) else None
        return got, token

    token = start_gather(0, conv_full)
    for i in range(1, min(GATHER_AHEAD, len(order))):
        token = start_gather(i, token, tie=token)

    xs = x.reshape(T, D)
    saved = []
    for l in range(L):
        small = dict(pool_w=pool_w[l], pool_b=pool_b[l].reshape(G, 1, LANES), pool_scale=pool_scale[l].reshape(1, pw),
                     conv_w=conv_full[l], conv_b=conv_b[l].reshape(1, lw), wa=lru_w_a[l],
                     ba=lru_b_a[l].reshape(H, 1, LANES), wx=lru_w_x[l], bx=lru_b_x[l].reshape(H, 1, LANES),
                     lam=lru_lambda[l].reshape(1, lw))
        g1, g2, g3 = norm_ffn1[l].reshape(1, D), norm_mix[l].reshape(1, D), norm_ffn2[l].reshape(1, D)

        x0 = xs
        (w_up1, w_dn1), token = finish_gather(3 * l, token if l == 0 else x0)
        w_dn1 = w_dn1.reshape(-1, D)
        u1 = _norm_matmul_fwd(x0, tied(g1, token), w_up1, f"ffn1_up_{l}", KEPT)
        x1 = _swiglu_down_fwd(u1, x0, w_dn1, f"ffn1_down_{l}")
        (w_i, w_pu, w_lu, w_o), token = finish_gather(3 * l + 1, x1)
        w_lu, w_o = w_lu.reshape(-1, D), w_o.reshape(-1, D)
        proj = _norm_matmul_fwd(x1, tied(g2, token), w_i, f"mix_in_{l}")
        pm = _pool_fwd(proj, small["pool_w"], small["pool_b"], small["pool_scale"], f"pool_{l}")
        zl, hs = _lru_fwd(proj, small["conv_w"], small["conv_b"], small["wa"], small["ba"], small["wx"], small["bx"],
                          small["lam"], pw, f"lru_{l}")
        x2 = _mix_out_fwd(x1, proj, pm, zl, w_pu, w_lu, w_o, f"mix_out_{l}")
        (w_up2, w_dn2), token = finish_gather(3 * l + 2, x2)
        w_dn2 = w_dn2.reshape(-1, D)
        u2 = _norm_matmul_fwd(x2, tied(g3, token), w_up2, f"ffn2_up_{l}", KEPT)
        xs = _swiglu_down_fwd(u2, x2, w_dn2, f"ffn2_down_{l}")
        saved.append(dict(x0=x0, u1=u1, x1=x1, proj=proj, pm=pm, zl=zl, hs=hs, x2=x2, u2=u2, small=small, g=(g1, g2, g3),
                          w=(w_up1, w_dn1, w_i, w_pu, w_lu, w_o, w_up2, w_dn2)))

    loss_part, dx, d_final = _loss_head(xs, final_norm.reshape(1, D), loss_target.reshape(T, D), "loss_head")
    loss = lax.psum(loss_part[0, 0], ("x", "y", "c"))

    du_spec_ffn = lambda tm, tn: pl.BlockSpec((None, tm, tn), lambda k, i: (k // 2, i, k % 2))
    du_spec_mix = lambda tm, tn: pl.BlockSpec((tm, tn), lambda k, i: (i, k))
    out = {n: [lax.empty(W[n].shape, F32) for _ in range(4)] for n in BIG}
    small_pieces = [None] * L
    stages = [None, None]

    def advance(new):
        token = None
        first = None
        if new is not None:
            l, names, grads = new
            tag = f"{names[0]}_{l}"
            lands = [lax.empty((N_CHIPS, a.shape[1] // 2, a.shape[2]), F32) for a in grads]
            sems, lands, token = _split_start(_halves_copies, len(grads), grads, lands, grads[0], f"halves_start_{tag}")
            first = (l, names, tag, grads, sems, lands)
        second = None
        if stages[0] is not None:
            l, names, tag, grads, sems, lands = stages[0]
            recv = _split_wait(_halves_copies, grads, lands, sems, grads[0] if token is None else token,
                               f"halves_wait_{tag}")
            sums = [_pair_sum(g, r, F32 if n == "small" else XFER, f"pair_sum_{n}_{l}")
                    for g, r, n in zip(grads, recv, names)]
            pbs = [p[1] for p in sums]
            lands = [lax.empty((N_CHIPS - 1,) + p.shape[1:], p.dtype) for p in pbs]
            sems, lands, token = _split_start(_chip_copies, 3 * len(pbs), pbs, lands, pbs[0], f"chips_start_{tag}")
            second = (l, names, tag, [p[0] for p in sums], pbs, sems, lands)
        if stages[1] is not None:
            l, names, tag, pfs, pbs, sems, lands = stages[1]
            got = _split_wait(_chip_copies, pbs, lands, sems, pbs[0] if token is None else token, f"chips_wait_{tag}")
            whole = _join_halves([_chip_sum(p, q, f"chip_sum_{n}_{l}") for p, q, n in zip(pfs, got, names)],
                                 f"join_halves_{tag}")
            for n, g in zip(names, whole):
                if n == "small":
                    small_pieces[l] = g
                else:
                    out[n] = _adamw_layer(W[n], g, M[n], V[n], out[n], l, f"adamw_{n}_{l}")
        stages[:] = [first, second]
        return token

    token = None
    for l in reversed(range(L)):
        s = saved[l]
        w_up1, w_dn1, w_i, w_pu, w_lu, w_o, w_up2, w_dn2 = s["w"]
        g1, g2, g3 = s["g"]
        sm = s["small"]
        du2, d_dn2 = _swiglu_down_bwd(s["u2"], dx, w_dn2, f"ffn2_down_bwd_{l}")
        dx, d_up2, dg3 = _norm_matmul_bwd(s["x2"], tied(g3, token), du2, du_spec_ffn, w_up2, dx, f"ffn2_up_bwd_{l}")
        token = advance((l, GROUPS[2], [d_up2, d_dn2.reshape(N_CHIPS, -1, D)]))
        dgl, dpm, dzl, d_o, d_pu, d_lu = _mix_out_bwd(dx, s["proj"], s["pm"], s["zl"], w_pu, w_lu, w_o, f"mix_out_bwd_{l}")
        du_pool, d_pool_w, d_pool_b, d_pool_scale = _pool_bwd(s["proj"], dpm, sm["pool_w"], sm["pool_b"], sm["pool_scale"],
                                                            f"pool_bwd_{l}")
        dul, dug, d_cw, d_cb, d_wa, d_ba, d_wx, d_bx, d_lam = _lru_bwd(
            s["proj"], s["hs"], dzl, sm["conv_w"], sm["conv_b"], sm["wa"], sm["ba"], sm["wx"], sm["bx"], sm["lam"], pw,
            f"lru_bwd_{l}")
        dproj = jnp.concatenate([du_pool, dul, dug, dgl], axis=1)
        dx, d_in, dg2 = _norm_matmul_bwd(s["x1"], tied(g2, token), dproj, du_spec_mix, w_i, dx, f"mix_in_bwd_{l}")
        token = advance((l, GROUPS[1], [d_in, d_pu, d_lu.reshape(N_CHIPS, -1, D), d_o.reshape(N_CHIPS, -1, D)]))
        du1, d_dn1 = _swiglu_down_bwd(s["u1"], dx, w_dn1, f"ffn1_down_bwd_{l}")
        dx, d_up1, dg1 = _norm_matmul_bwd(s["x0"], tied(g1, token), du1, du_spec_ffn, w_up1, dx, f"ffn1_up_bwd_{l}")
        small = [dg1, dg2, d_pool_w, d_pool_b, d_pool_scale, d_cw, d_cb, d_wa, d_ba, d_wx, d_bx, d_lam, dg3]
        small = _pack(small + ([d_final] if l == L - 1 else []), 2 * 16 * N_CHIPS).reshape(N_CHIPS, -1, LANES)
        token = advance((l, GROUPS[0] + ("small",), [d_up1, d_dn1.reshape(N_CHIPS, -1, D), small]))
    while any(stage is not None for stage in stages):
        advance(None)

    pieces = _gather_small(jnp.concatenate(small_pieces), "gather_small_grads")
    pieces = pieces.reshape(N_CHIPS, L, -1)
    like = {n: jax.ShapeDtypeStruct(W[n].shape[1:] if n != "conv_w" else (CONV_WIDTH, lw), F32) for n in SMALL_LAYER}
    per_layer = []
    for l in range(L):
        names = SMALL_LAYER + (("final_norm",) if l == L - 1 else ())
        shapes = [like[n] if n in like else jax.ShapeDtypeStruct((D,), F32) for n in names]
        per_layer.append(dict(zip(names, _unpack(pieces[:, l], shapes))))
    small_sum = {n: jnp.stack([per_layer[l][n] for l in range(L)]) for n in SMALL_LAYER}
    small_sum["final_norm"] = per_layer[L - 1]["final_norm"]
    cs = conv_w.shape[2]
    small_sum["conv_w"] = lax.dynamic_slice_in_dim(small_sum["conv_w"], chip * cs, cs, axis=2)
    packs = [_pack([d[n] for n in SMALL], 1024) for d in (W, small_sum, M, V)]
    res = _adamw(*packs, "adamw_small")
    like = [W[n] for n in SMALL]
    unpacked = [_unpack(r, like) for r in res]
    for j, n in enumerate(SMALL):
        out[n] = tuple(u[j] for u in unpacked)

    return (loss, dx.reshape(x.shape), *[out[n][0] for n in WEIGHTS], *[out[n][1] for n in WEIGHTS],
            *[out[n][2] for n in WEIGHTS], *[out[n][3] for n in WEIGHTS])
```
